```python
import jax, jax.numpy as jnp
from jax import lax
import numpy as np

D_MODEL = 1024
BATCH = 8
SEQ = 2048
DEPTH = 4

HEAD_DIM = 64
D_MIX = D_MODEL
A_HEADS = 4
B_HEADS = 4
C_GROUPS = 4
D_GROUPS = 4
A_W = A_HEADS * HEAD_DIM
B_W = B_HEADS * HEAD_DIM
C_W = C_GROUPS * HEAD_DIM
D_W = D_GROUPS * HEAD_DIM
N_NSA_KV = 6
CMP_LEN = 32
CMP_STRIDE = 16
CMP_HIDDEN = 256
SEL_BLOCK = 64
SEL_TOPK = 16
SEL_Q_CHUNK = 64
WIN = 512
FORCE = 1e4
DILATED_PAIRS = ((128, 1), (512, 4), (2048, 16))
BAND_BLOCK = 128
POOL_SIZES = (2, 4, 8, 16)
SG_CHUNK = 128
RMS_EPS = 1e-6
LN_EPS = 1e-5
NEG = -1e30
SPLITS = (A_W, N_NSA_KV * HEAD_DIM, 3 * A_HEADS, A_W, 3 * B_W, B_W, C_W, C_W, 2 * D_W, D_W)
D_IN = A_W + N_NSA_KV * HEAD_DIM + 3 * A_HEADS + A_W + 3 * B_W + B_W + C_W + C_W + 2 * D_W + D_W

kernel_name = "hybrid_nsa_dilated_pool_sgu"


def rmsnorm(x, g):
    xf = x.astype(jnp.float32)
    y = xf * lax.rsqrt(jnp.mean(xf * xf, axis=-1, keepdims=True) + RMS_EPS)
    return (y * g.astype(jnp.float32)).astype(x.dtype)


def banded_attention(q, k, v, max_dist):
    Bn, H, L, Dh = q.shape
    Hk = k.shape[1]
    rep = H // Hk
    blk = BAND_BLOCK
    n_prev = -(-max_dist // blk)
    nb = -(-L // blk)
    Lp = nb * blk
    pad = Lp - L
    q = jnp.pad(q, ((0, 0), (0, 0), (0, pad), (0, 0)))
    k = jnp.pad(k, ((0, 0), (0, 0), (n_prev * blk, pad), (0, 0)))
    v = jnp.pad(v, ((0, 0), (0, 0), (n_prev * blk, pad), (0, 0)))
    qb = q.reshape(Bn, Hk, rep, nb, blk, Dh)
    kb = k.reshape(Bn, Hk, nb + n_prev, blk, Dh)
    vb = v.reshape(Bn, Hk, nb + n_prev, blk, Dh)
    kw = jnp.concatenate([kb[:, :, r:r + nb] for r in range(n_prev + 1)], axis=3)
    vw = jnp.concatenate([vb[:, :, r:r + nb] for r in range(n_prev + 1)], axis=3)
    s = jnp.einsum('bgrnqd,bgnkd->bgrnqk', qb, kw).astype(jnp.float32) * (Dh ** -0.5)
    qpos = jnp.arange(nb)[:, None, None] * blk + jnp.arange(blk)[None, :, None]
    kpos = jnp.arange(nb)[:, None, None] * blk + jnp.arange((n_prev + 1) * blk)[None, None, :] - n_prev * blk
    dist = qpos - kpos
    mask = (dist >= 0) & (dist <= max_dist) & (kpos >= 0)
    s = jnp.where(mask, s, NEG)
    lse = jax.nn.logsumexp(s, axis=-1)
    p = jnp.exp(s - lse[..., None])
    o = jnp.einsum('bgrnqk,bgnkd->bgrnqd', p.astype(vw.dtype), vw)
    o = o.reshape(Bn, H, Lp, Dh)[:, :, :L]
    lse = lse.reshape(Bn, H, Lp)[:, :, :L]
    return o, lse


def nsa_mixer(q, kv, gates, pe_cmp, w_cmp1, w_cmp2):
    Bn, S, H, Dh = q.shape
    scale = Dh ** -0.5
    t = jnp.arange(S)
    n_cmp = (S - CMP_LEN) // CMP_STRIDE + 1
    cidx = jnp.arange(n_cmp)[:, None] * CMP_STRIDE + jnp.arange(CMP_LEN)[None, :]
    kvc = kv[:, :, 0:2][:, cidx] + pe_cmp.transpose(1, 0, 2)
    flat = kvc.transpose(0, 1, 3, 2, 4).reshape(Bn, n_cmp, 2, CMP_LEN * Dh)
    hid = jax.nn.gelu(jnp.einsum('bnjf,jfh->bnjh', flat, w_cmp1))
    comp = jnp.einsum('bnjh,jhd->bnjd', hid, w_cmp2)
    k_c, v_c = comp[:, :, 0], comp[:, :, 1]
    s_c = jnp.einsum('bshd,bnd->bhsn', q, k_c).astype(jnp.float32) * scale
    valid_c = cidx[:, -1][None, :] <= t[:, None]
    p_c = jax.nn.softmax(jnp.where(valid_c, s_c, NEG), axis=-1) * valid_c
    o_cmp = jnp.einsum('bhsn,bnd->bshd', p_c.astype(v_c.dtype), v_c)
    n_slc = S // SEL_BLOCK
    cmp_start = jnp.arange(n_cmp) * CMP_STRIDE
    sel_start = jnp.arange(n_slc) * SEL_BLOCK
    overlap = ((cmp_start[:, None] < sel_start[None, :] + SEL_BLOCK)
               & (cmp_start[:, None] + CMP_LEN > sel_start[None, :])).astype(jnp.float32)
    imp = jnp.einsum('bhsn,nj->bsj', p_c, overlap)
    cur = t // SEL_BLOCK
    j = jnp.arange(n_slc)
    forced = (j[None, :] == 0) | (j[None, :] == cur[:, None]) | (j[None, :] == cur[:, None] - 1)
    valid_s = j[None, :] <= cur[:, None]
    imp = jnp.where(forced, FORCE, jnp.where(valid_s, imp, -FORCE))
    k_top = min(SEL_TOPK, n_slc)
    _, sel_idx = lax.top_k(imp, k_top)
    kb = kv[:, :, 2].reshape(Bn, n_slc, SEL_BLOCK, Dh)
    vb = kv[:, :, 3].reshape(Bn, n_slc, SEL_BLOCK, Dh)
    nq = S // SEL_Q_CHUNK
    qc = q.reshape(Bn, nq, SEL_Q_CHUNK, H, Dh).transpose(1, 0, 2, 3, 4)
    ic = sel_idx.reshape(Bn, nq, SEL_Q_CHUNK, k_top).transpose(1, 0, 2, 3)
    tc = t.reshape(nq, SEL_Q_CHUNK)

    def sel_chunk(args):
        qq, ii, tt = args
        kg = jax.vmap(lambda a, b: a[b])(kb, ii)
        vg = jax.vmap(lambda a, b: a[b])(vb, ii)
        kpos = ii[..., None] * SEL_BLOCK + jnp.arange(SEL_BLOCK)
        ok = kpos <= tt[None, :, None, None]
        s = jnp.einsum('bchd,bckld->bhckl', qq, kg).astype(jnp.float32) * scale
        s = jnp.where(ok[:, None], s, NEG).reshape(Bn, H, SEL_Q_CHUNK, k_top * SEL_BLOCK)
        p = jax.nn.softmax(s, axis=-1)
        return jnp.einsum('bhcn,bcnd->bchd', p.astype(vg.dtype),
                          vg.reshape(Bn, SEL_Q_CHUNK, k_top * SEL_BLOCK, Dh))

    o_slc = lax.map(sel_chunk, (qc, ic, tc))
    o_slc = o_slc.transpose(1, 0, 2, 3, 4).reshape(Bn, S, H, Dh)
    o_win, _ = banded_attention(q.transpose(0, 2, 1, 3), kv[:, :, 4][:, None], kv[:, :, 5][:, None], WIN - 1)
    o_win = o_win.transpose(0, 2, 1, 3)
    g = jax.nn.sigmoid(gates.astype(jnp.float32))[..., None]
    o = (g[:, :, 0] * o_cmp.astype(jnp.float32) + g[:, :, 1] * o_slc.astype(jnp.float32)
         + g[:, :, 2] * o_win.astype(jnp.float32))
    return o.reshape(Bn, S, H * Dh)


def dilated_mixer(q, k, v):
    Bn, H, S, Dh = q.shape
    outs, lses = [], []
    for window, dil in DILATED_PAIRS:
        L = S // dil

        def by_stride(a):
            return a.reshape(Bn, H, L, dil, Dh).transpose(0, 1, 3, 2, 4).reshape(Bn, H * dil, L, Dh)

        o, lse = banded_attention(by_stride(q), by_stride(k), by_stride(v), window // dil)
        outs.append(o.reshape(Bn, H, dil, L, Dh).transpose(0, 1, 3, 2, 4).reshape(Bn, H, S, Dh))
        lses.append(lse.reshape(Bn, H, dil, L).transpose(0, 1, 3, 2).reshape(Bn, H, S))
    w = jax.nn.softmax(jnp.stack(lses, axis=0), axis=0)
    o = jnp.sum(w[..., None] * jnp.stack(outs, axis=0).astype(jnp.float32), axis=0)
    return o.transpose(0, 2, 1, 3).reshape(Bn, S, H * Dh)


def pool_mixer(c, w_pool, pool_scale):
    Bn, S, _ = c.shape
    cf = c.astype(jnp.float32).reshape(Bn, S, C_GROUPS, HEAD_DIM)
    cs = jnp.pad(jnp.cumsum(cf, axis=1), ((0, 0), (1, 0), (0, 0), (0, 0)))
    t = jnp.arange(S)
    outs = []
    for g, w in enumerate(POOL_SIZES):
        lo = jnp.maximum(t + 1 - w, 0)
        win_sum = cs[:, 1:, g] - cs[:, lo, g]
        cnt = (t + 1 - lo).astype(jnp.float32)
        outs.append(win_sum / cnt[None, :, None] - cf[:, :, g])
    pooled = jnp.stack(outs, axis=2)
    mixed = jnp.einsum('bsgc,gcd->bsgd', pooled, w_pool.astype(jnp.float32))
    return mixed.reshape(Bn, S, C_W) * pool_scale.astype(jnp.float32)


def spatial_gating(uv, ln_g, ln_b, w_sp, b_sp):
    Bn, S, _ = uv.shape
    u, v = jnp.split(uv, 2, axis=-1)
    vf = v.astype(jnp.float32)
    mu = jnp.mean(vf, axis=-1, keepdims=True)
    var = jnp.mean(jnp.square(vf - mu), axis=-1, keepdims=True)
    vn = (vf - mu) * lax.rsqrt(var + LN_EPS) * ln_g.astype(jnp.float32) + ln_b.astype(jnp.float32)
    nc = S // SG_CHUNK
    vn = vn.reshape(Bn, nc, SG_CHUNK, D_GROUPS, HEAD_DIM)
    w = w_sp.astype(jnp.float32) * jnp.tril(jnp.ones((SG_CHUNK, SG_CHUNK), jnp.float32))
    z = jnp.einsum('gij,bnjgc->bnigc', w, vn) + b_sp.astype(jnp.float32).T[None, None, :, :, None]
    return u.astype(jnp.float32) * z.reshape(Bn, S, D_W)


def hybrid_layer(x, g_pre, w_in, pe_cmp, w_cmp1, w_cmp2, w_pool, pool_scale,
                 sg_ln_g, sg_ln_b, w_sp, b_sp, w_out, g_post):
    Bn, S, _ = x.shape
    h = rmsnorm(x, g_pre)
    proj = h @ w_in
    split_points = np.cumsum(np.array(SPLITS))[:-1].tolist()
    a_q, a_kv, a_g, a_z, b_qkv, b_z, c_in, c_z, d_uv, d_z = jnp.split(proj, split_points, axis=-1)
    y_a = nsa_mixer(a_q.reshape(Bn, S, A_HEADS, HEAD_DIM), a_kv.reshape(Bn, S, N_NSA_KV, HEAD_DIM),
                    a_g.reshape(Bn, S, 3, A_HEADS), pe_cmp, w_cmp1, w_cmp2)
    bqkv = b_qkv.reshape(Bn, S, 3, B_HEADS, HEAD_DIM).transpose(2, 0, 3, 1, 4)
    y_b = dilated_mixer(bqkv[0], bqkv[1], bqkv[2])
    y_c = pool_mixer(c_in, w_pool, pool_scale)
    y_d = spatial_gating(d_uv, sg_ln_g, sg_ln_b, w_sp, b_sp)
    y = jnp.concatenate([
        (y_a * jax.nn.silu(a_z.astype(jnp.float32))).astype(x.dtype),
        (y_b * jax.nn.silu(b_z.astype(jnp.float32))).astype(x.dtype),
        (y_c * jax.nn.silu(c_z.astype(jnp.float32))).astype(x.dtype),
        (y_d * jax.nn.silu(d_z.astype(jnp.float32))).astype(x.dtype),
    ], axis=-1)
    out = y @ w_out
    return x + rmsnorm(out, g_post)


def setup_inputs(seed: int = 0) -> dict:
    key = jax.random.key(seed)
    ks = jax.random.split(key, 16)
    f32 = jnp.float32
    n = lambda k, shape: jax.random.normal(k, shape, f32)
    return {
        "x": n(ks[0], (BATCH, SEQ, D_MODEL)),
        "g_pre": 1.0 + 0.1 * n(ks[1], (DEPTH, D_MODEL)),
        "w_in": n(ks[2], (DEPTH, D_MODEL, D_IN)) * D_MODEL ** -0.5,
        "pe_cmp": 0.2 * n(ks[3], (DEPTH, 2, CMP_LEN, HEAD_DIM)),
        "w_cmp1": n(ks[4], (DEPTH, 2, CMP_LEN * HEAD_DIM, CMP_HIDDEN)) * (CMP_LEN * HEAD_DIM) ** -0.5,
        "w_cmp2": n(ks[5], (DEPTH, 2, CMP_HIDDEN, HEAD_DIM)) * CMP_HIDDEN ** -0.5,
        "w_pool": n(ks[6], (DEPTH, C_GROUPS, HEAD_DIM, HEAD_DIM)) * HEAD_DIM ** -0.5,
        "pool_scale": 1.0 + 0.1 * n(ks[7], (DEPTH, C_W)),
        "sg_ln_g": 1.0 + 0.1 * n(ks[8], (DEPTH, D_W)),
        "sg_ln_b": 0.02 * n(ks[9], (DEPTH, D_W)),
        "w_sp": n(ks[10], (DEPTH, D_GROUPS, SG_CHUNK, SG_CHUNK)) * SG_CHUNK ** -0.5,
        "b_sp": 1.0 + 0.1 * n(ks[11], (DEPTH, D_GROUPS, SG_CHUNK)),
        "w_out": n(ks[12], (DEPTH, D_MIX, D_MODEL)) * D_MIX ** -0.5,
        "g_post": 1.0 + 0.1 * n(ks[13], (DEPTH, D_MODEL)),
    }


def reference(x, g_pre, w_in, pe_cmp, w_cmp1, w_cmp2, w_pool, pool_scale,
              sg_ln_g, sg_ln_b, w_sp, b_sp, w_out, g_post):
    for l in range(DEPTH):
        x = hybrid_layer(x, g_pre[l], w_in[l], pe_cmp[l], w_cmp1[l], w_cmp2[l], w_pool[l], pool_scale[l],
                         sg_ln_g[l], sg_ln_b[l], w_sp[l], b_sp[l], w_out[l], g_post[l])
    return x
```

```python
import functools

import numpy as np
import jax
import jax.numpy as jnp
from jax import lax
from jax.experimental import pallas as pl
from jax.experimental.pallas import tpu as pltpu

F32 = jnp.float32
BF16 = jnp.bfloat16

HEAD_DIM = 64
N_HEADS = 4
GROUP_W = N_HEADS * HEAD_DIM
CMP_LEN = 32
CMP_STRIDE = 16
CMP_HIDDEN = 256
SEL_BLOCK = 64
SEL_TOPK = 16
WIN = 512
FORCE = 1e4
DILATED_PAIRS = ((128, 1), (512, 4), (2048, 16))
POOL_SIZES = (2, 4, 8, 16)
SG_CHUNK = 128
RMS_EPS = 1e-6
LN_EPS = 1e-5
NEG = -1e30

LANES = 128
N_GATES = 3 * N_HEADS

PB_AQ, PB_SLC, PB_WIN, PB_BQ, PB_BK, PB_BV = 0, 2, 3, 4, 6, 8
PB_COLS = 10 * LANES
PF_CMP, PF_AG, PF_AZ, PF_BZ, PF_CIN, PF_CZ, PF_DU, PF_DV, PF_DZ = 0, 1, 2, 4, 6, 8, 10, 12, 14
PF_COLS = 16 * LANES
P_COLS = PB_COLS + PF_COLS

TM_PROJ = 256
TQ_NSA = 128
TK_SEL = 512
T_DIL = 256
TS_POOL = 512
POOL_HALO = 16


def _sigmoid(x):
    return 1.0 / (1.0 + jnp.exp(-x))


def _silu(x):
    return x * _sigmoid(x)


def _dot_nt(a, b):
    return lax.dot_general(a, b, (((1,), (1,)), ((), ())), preferred_element_type=F32)


def _dot(a, b):
    return jnp.dot(a, b, preferred_element_type=F32)


def _in_proj_kernel(x_ref, g_ref, w_ref, pb_ref, pf_ref):
    x = x_ref[...]
    inv = lax.rsqrt(jnp.mean(x * x, axis=-1, keepdims=True) + RMS_EPS)
    h = (x * inv * g_ref[...]).astype(BF16)
    cw = 2 * LANES
    for c in range(PB_COLS // cw):
        pb_ref[:, c * cw:(c + 1) * cw] = _dot(h, w_ref[:, c * cw:(c + 1) * cw]).astype(BF16)
    for c in range(PF_COLS // cw):
        pf_ref[:, c * cw:(c + 1) * cw] = _dot(h, w_ref[:, PB_COLS + c * cw:PB_COLS + (c + 1) * cw])


def _in_proj(x2, g_pre, w_perm, layer):
    m, d = x2.shape
    return pl.pallas_call(
        _in_proj_kernel,
        grid=(m // TM_PROJ,),
        in_specs=[
            pl.BlockSpec((TM_PROJ, d), lambda i: (i, 0)),
            pl.BlockSpec((None, 1, d), lambda i: (layer, 0, 0)),
            pl.BlockSpec((None, d, P_COLS), lambda i: (layer, 0, 0)),
        ],
        out_specs=[
            pl.BlockSpec((TM_PROJ, PB_COLS), lambda i: (i, 0)),
            pl.BlockSpec((TM_PROJ, PF_COLS), lambda i: (i, 0)),
        ],
        out_shape=[
            jax.ShapeDtypeStruct((m, PB_COLS), BF16),
            jax.ShapeDtypeStruct((m, PF_COLS), F32),
        ],
        compiler_params=pltpu.CompilerParams(
            dimension_semantics=("arbitrary",), vmem_limit_bytes=48 * 1024 * 1024),
        name="in_proj",
    )(x2, g_pre, w_perm)


def _gelu_tanh(x):
    return 0.5 * x * (1.0 + jnp.tanh(np.sqrt(2.0 / np.pi).astype(np.float32) * (x + 0.044715 * (x * x * x))))


def _compress_kernel(g_ref, pe_ref, w1_ref, w2_ref, o_ref):
    nb, _, ng, half = g_ref.shape
    for j in range(2):
        g = g_ref[:, j].reshape(nb * ng, half)
        lo = _dot((g + pe_ref[j, 0:1, :]).astype(BF16), w1_ref[j, 0:half, :])
        hi = _dot((g + pe_ref[j, 1:2, :]).astype(BF16), w1_ref[j, half:2 * half, :])
        hid = _gelu_tanh(lo + pltpu.roll(hi, nb * ng - 1, 0))
        comp = _dot(hid.astype(BF16), w2_ref[j])
        row = lax.broadcasted_iota(jnp.int32, comp.shape, 0) % ng
        comp = jnp.where(row < ng - 1, comp, 0.0)
        o_ref[:, :, j * HEAD_DIM:(j + 1) * HEAD_DIM] = comp.reshape(nb, ng, HEAD_DIM).astype(BF16)


def _compress(gk, pe2, w1, w2, layer):
    nb, _, ng, half = gk.shape
    return pl.pallas_call(
        _compress_kernel,
        grid=(1,),
        in_specs=[
            pl.BlockSpec(gk.shape, lambda i: (0, 0, 0, 0)),
            pl.BlockSpec((None, 2, 2, half), lambda i: (layer, 0, 0, 0)),
            pl.BlockSpec((None, 2, 2 * half, CMP_HIDDEN), lambda i: (layer, 0, 0, 0)),
            pl.BlockSpec((None, 2, CMP_HIDDEN, HEAD_DIM), lambda i: (layer, 0, 0, 0)),
        ],
        out_specs=pl.BlockSpec((nb, ng, LANES), lambda i: (0, 0, 0)),
        out_shape=jax.ShapeDtypeStruct((nb, ng, LANES), BF16),
        compiler_params=pltpu.CompilerParams(
            dimension_semantics=("arbitrary",), vmem_limit_bytes=48 * 1024 * 1024),
        name="compress",
    )(gk, pe2, w1, w2)


def _nsa_kernel(q_ref, slc_ref, win_ref, kc_ref, gate_ref, z_ref, ovl_ref, expand_ref, o_ref):
    tq = q_ref.shape[0]
    seq = slc_ref.shape[0]
    nh = N_HEADS
    q0 = pl.program_id(1) * tq
    lane = lax.broadcasted_iota(jnp.int32, (tq, LANES), 1)
    low = lane < HEAD_DIM
    t_row = q0 + lax.broadcasted_iota(jnp.int32, (tq, LANES), 0)

    q = q_ref[...].astype(F32) * (HEAD_DIM ** -0.5)
    heads = []
    for h in range(nh):
        blk = q[:, LANES * (h // 2):LANES * (h // 2 + 1)]
        if h % 2:
            blk = pltpu.roll(blk, HEAD_DIM, 1)
        heads.append(jnp.where(low, blk, 0.0).astype(BF16))
    qs = jnp.concatenate(heads, axis=0)

    kc = kc_ref[...]
    n_cmp = (seq - CMP_LEN) // CMP_STRIDE + 1
    valid_c = (lane * CMP_STRIDE + (CMP_LEN - 1) <= t_row) & (lane < n_cmp)
    s_c = _dot_nt(qs, kc).reshape(nh, tq, LANES)
    s_c = jnp.where(valid_c[None], s_c, NEG)
    e_c = jnp.exp(s_c - jnp.max(s_c, axis=-1, keepdims=True))
    p_c = jnp.where(valid_c[None], e_c / jnp.sum(e_c, axis=-1, keepdims=True), 0.0)
    o_cmp = _dot(p_c.reshape(nh * tq, LANES).astype(BF16), kc)

    psum = p_c[0] + p_c[1] + p_c[2] + p_c[3]
    p_hi = psum.astype(BF16)
    r1 = psum - p_hi.astype(F32)
    p_mid = r1.astype(BF16)
    p_lo = (r1 - p_mid.astype(F32)).astype(BF16)
    ovl = ovl_ref[...]
    imp_t = _dot_nt(ovl, p_hi) + _dot_nt(ovl, p_mid) + _dot_nt(ovl, p_lo)
    n_slc = seq // SEL_BLOCK
    v = imp_t[0:n_slc]
    jj = lax.broadcasted_iota(jnp.int32, (n_slc, tq), 0)
    cur = (q0 + lax.broadcasted_iota(jnp.int32, (n_slc, tq), 1)) // SEL_BLOCK
    forced = (jj == 0) | (jj == cur) | (jj == cur - 1)
    v = jnp.where(forced, FORCE, jnp.where(jj <= cur, v, -FORCE))
    rank = jnp.zeros((n_slc, tq), F32)
    for j2 in range(n_slc):
        vj = v[j2:j2 + 1, :]
        ahead = (vj > v) | ((vj == v) & (j2 < jj))
        rank = rank + jnp.where(ahead, 1.0, 0.0)
    sel = jnp.where(rank < min(SEL_TOPK, n_slc), 1.0, 0.0)
    sel = jnp.concatenate([sel, jnp.zeros((LANES - n_slc, tq), F32)], axis=0)
    sel_q = sel.T.astype(BF16)

    tk = expand_ref.shape[2]
    t_k = q0 + lax.broadcasted_iota(jnp.int32, (tq, tk), 0)

    def sel_step(c, carry):
        m, l, acc = carry
        k0 = pl.multiple_of(c * tk, tk)
        kv = slc_ref[pl.ds(k0, tk), :]
        chosen = _dot(sel_q, expand_ref[c])
        kpos = k0 + lax.broadcasted_iota(jnp.int32, (tq, tk), 1)
        bias = jnp.where((chosen > 0.5) & (kpos <= t_k), 0.0, NEG)
        s = _dot_nt(qs, kv).reshape(nh, tq, tk) + bias[None]
        m_new = jnp.maximum(m, jnp.max(s, axis=-1, keepdims=True))
        alpha = jnp.exp(m - m_new)
        p = jnp.exp(s - m_new)
        l = alpha * l + jnp.sum(p, axis=-1, keepdims=True)
        pv = _dot(p.reshape(nh * tq, tk).astype(BF16), kv).reshape(nh, tq, LANES)
        return m_new, l, alpha * acc + pv

    n_chunks = (q0 + tq + tk - 1) // tk
    m0 = jnp.full((nh, tq, 1), NEG, F32)
    l0 = jnp.zeros((nh, tq, 1), F32)
    a0 = jnp.zeros((nh, tq, LANES), F32)
    _, l_s, acc_s = lax.fori_loop(0, n_chunks, sel_step, (m0, l0, a0))
    o_slc = acc_s / l_s

    span = WIN + tq
    w0 = pl.multiple_of(jnp.maximum(q0 - WIN, 0), tq)
    kvw = win_ref[pl.ds(w0, span), :]
    dist = (q0 + lax.broadcasted_iota(jnp.int32, (tq, span), 0)) - (w0 + lax.broadcasted_iota(jnp.int32, (tq, span), 1))
    bias_w = jnp.where((dist >= 0) & (dist <= WIN - 1), 0.0, NEG)
    s_w = _dot_nt(qs, kvw).reshape(nh, tq, span) + bias_w[None]
    e_w = jnp.exp(s_w - jnp.max(s_w, axis=-1, keepdims=True))
    l_w = jnp.sum(e_w, axis=-1, keepdims=True)
    o_win = _dot(e_w.reshape(nh * tq, span).astype(BF16), kvw).reshape(nh, tq, LANES) / l_w

    gate = _sigmoid(gate_ref[...])
    o_cmp = o_cmp.reshape(nh, tq, LANES)
    comb = []
    for h in range(nh):
        g_c = gate[:, 0 * nh + h:0 * nh + h + 1]
        g_s = gate[:, 1 * nh + h:1 * nh + h + 1]
        g_w = gate[:, 2 * nh + h:2 * nh + h + 1]
        comb.append(g_c * o_cmp[h] + g_s * o_slc[h] + g_w * o_win[h])
    for c in range(nh // 2):
        blk = jnp.where(low, pltpu.roll(comb[2 * c], HEAD_DIM, 1), comb[2 * c + 1])
        zc = z_ref[:, c * LANES:(c + 1) * LANES]
        o_ref[:, c * LANES:(c + 1) * LANES] = (blk * _silu(zc)).astype(o_ref.dtype)


def _nsa(pb, pf, kcvc, ovl_t, expand, nb, seq):
    tq = TQ_NSA
    nq = seq // tq
    n_chunks = seq // TK_SEL
    return pl.pallas_call(
        _nsa_kernel,
        grid=(nb, nq),
        in_specs=[
            pl.BlockSpec((tq, 2 * LANES), lambda b, i: (b * nq + i, PB_AQ // 2)),
            pl.BlockSpec((seq, LANES), lambda b, i: (b, PB_SLC)),
            pl.BlockSpec((seq, LANES), lambda b, i: (b, PB_WIN)),
            pl.BlockSpec((None, LANES, LANES), lambda b, i: (b, 0, 0)),
            pl.BlockSpec((tq, LANES), lambda b, i: (b * nq + i, PF_AG)),
            pl.BlockSpec((tq, 2 * LANES), lambda b, i: (b * nq + i, PF_AZ // 2)),
            pl.BlockSpec((LANES, LANES), lambda b, i: (0, 0)),
            pl.BlockSpec((n_chunks, LANES, TK_SEL), lambda b, i: (0, 0, 0)),
        ],
        out_specs=pl.BlockSpec((tq, GROUP_W), lambda b, i: (b * nq + i, 0)),
        out_shape=jax.ShapeDtypeStruct((nb * seq, GROUP_W), BF16),
        compiler_params=pltpu.CompilerParams(
            dimension_semantics=("arbitrary", "arbitrary"), vmem_limit_bytes=48 * 1024 * 1024),
        name="nsa",
    )(pb, pb, pb, kcvc, pf, pf, ovl_t, expand)


def _dilated_kernel(q_ref, k_ref, v_ref, z_ref, bias_ref, o_ref):
    t = q_ref.shape[0]
    qi = pl.program_id(1)
    lane = lax.broadcasted_iota(jnp.int32, (t, LANES), 1)
    low = lane < HEAD_DIM
    q = q_ref[...].astype(F32) * (HEAD_DIM ** -0.5)
    for c in range(N_HEADS // 2):
        blk = q[:, c * LANES:(c + 1) * LANES]
        qs = jnp.concatenate([jnp.where(low, blk, 0.0), jnp.where(low, 0.0, blk)], axis=0).astype(BF16)

        def step(j, carry):
            m, l, acc = carry
            k0 = pl.multiple_of(j * t, t)
            kk = k_ref[pl.ds(k0, t), c * LANES:(c + 1) * LANES]
            vv = v_ref[pl.ds(k0, t), c * LANES:(c + 1) * LANES]
            s = _dot_nt(qs, kk).reshape(2, t, t) + bias_ref[qi - j][None]
            m_new = jnp.maximum(m, jnp.max(s, axis=-1, keepdims=True))
            alpha = jnp.exp(m - m_new)
            p = jnp.exp(s - m_new)
            l = alpha * l + jnp.sum(p, axis=-1, keepdims=True)
            pv = _dot(p.reshape(2 * t, t).astype(BF16), vv).reshape(2, t, LANES)
            return m_new, l, alpha * acc + pv

        m0 = jnp.full((2, t, 1), NEG, F32)
        l0 = jnp.zeros((2, t, 1), F32)
        a0 = jnp.zeros((2, t, LANES), F32)
        _, l_f, acc = lax.fori_loop(0, qi + 1, step, (m0, l0, a0))
        o = acc / l_f
        blk_o = jnp.where(low, o[0], o[1])
        zc = z_ref[:, c * LANES:(c + 1) * LANES]
        o_ref[:, c * LANES:(c + 1) * LANES] = (blk_o * _silu(zc)).astype(o_ref.dtype)


def _dilated(pb, pf, bias_tab, nb, seq):
    t = T_DIL
    nq = seq // t
    return pl.pallas_call(
        _dilated_kernel,
        grid=(nb, nq),
        in_specs=[
            pl.BlockSpec((t, GROUP_W), lambda b, i: (b * nq + i, PB_BQ // 2)),
            pl.BlockSpec((seq, GROUP_W), lambda b, i: (b, PB_BK // 2)),
            pl.BlockSpec((seq, GROUP_W), lambda b, i: (b, PB_BV // 2)),
            pl.BlockSpec((t, GROUP_W), lambda b, i: (b * nq + i, PF_BZ // 2)),
            pl.BlockSpec((nq, t, t), lambda b, i: (0, 0, 0)),
        ],
        out_specs=pl.BlockSpec((t, GROUP_W), lambda b, i: (b * nq + i, 0)),
        out_shape=jax.ShapeDtypeStruct((nb * seq, GROUP_W), BF16),
        compiler_params=pltpu.CompilerParams(
            dimension_semantics=("arbitrary", "arbitrary"), vmem_limit_bytes=48 * 1024 * 1024),
        name="dilated",
    )(pb, pb, pb, pf, bias_tab)


def _pool_sgu_kernel(c_ref, cprev_ref, cz_ref, du_ref, dv_ref, dz_ref,
                     wpool_ref, pscale_ref, lng_ref, lnb_ref, wsp_ref, bsp_ref, yc_ref, yd_ref):
    ts = c_ref.shape[0]
    i = pl.program_id(1)
    lane = lax.broadcasted_iota(jnp.int32, (ts, GROUP_W), 1)

    cur = c_ref[...]
    prev = jnp.where(i > 0, cprev_ref[...], 0.0)
    ext = jnp.concatenate([prev, cur], axis=0)
    t_pos = i * ts + lax.broadcasted_iota(jnp.int32, (ts, 1), 0)
    pooled = None
    acc = ext
    width = 1
    for g, w in enumerate(POOL_SIZES):
        while width < w:
            acc = acc + pltpu.roll(acc, width, 0)
            width *= 2
        cnt = jnp.minimum(t_pos + 1, w).astype(F32)
        mean_w = acc[POOL_HALO:POOL_HALO + ts] * (1.0 / cnt)
        pooled = mean_w if pooled is None else jnp.where(lane >= g * HEAD_DIM, mean_w, pooled)
    pooled = pooled - cur
    mixed = _dot(pooled.astype(BF16), wpool_ref[...]) * pscale_ref[...]
    yc_ref[...] = (mixed * _silu(cz_ref[...])).astype(yc_ref.dtype)

    v = dv_ref[...]
    mu = jnp.mean(v, axis=-1, keepdims=True)
    var = jnp.mean(jnp.square(v - mu), axis=-1, keepdims=True)
    vn = ((v - mu) * lax.rsqrt(var + LN_EPS) * lng_ref[...] + lnb_ref[...]).astype(BF16)
    r = lax.broadcasted_iota(jnp.int32, (SG_CHUNK, SG_CHUNK), 0)
    cidx = lax.broadcasted_iota(jnp.int32, (SG_CHUNK, SG_CHUNK), 1)
    w_tril = [jnp.where(cidx <= r, wsp_ref[g], 0.0).astype(BF16) for g in range(N_HEADS)]
    lane_c = lax.broadcasted_iota(jnp.int32, (SG_CHUNK, GROUP_W), 1)
    for ci in range(ts // SG_CHUNK):
        rows = slice(ci * SG_CHUNK, (ci + 1) * SG_CHUNK)
        vc = vn[rows]
        zmix = _dot(w_tril[0], vc)
        for g in range(1, N_HEADS):
            zmix = jnp.where(lane_c >= g * HEAD_DIM, _dot(w_tril[g], vc), zmix)
        zfull = zmix + bsp_ref[...]
        yd_ref[rows, :] = (du_ref[rows, :] * zfull * _silu(dz_ref[rows, :])).astype(yd_ref.dtype)


def _pool_sgu(pf, wpool_bd, pool_scale, ln_g, ln_b, w_sp, bsp_exp, layer, nb, seq):
    ts = TS_POOL
    ns = seq // ts
    halo_blocks = ts // POOL_HALO

    def tile(col):
        return pl.BlockSpec((ts, GROUP_W), lambda b, i: (b * ns + i, col // 2))

    def per_layer(shape):
        nd = len(shape)
        return pl.BlockSpec((None,) + shape, lambda b, i: (layer,) + (0,) * nd)

    return pl.pallas_call(
        _pool_sgu_kernel,
        grid=(nb, ns),
        in_specs=[
            tile(PF_CIN),
            pl.BlockSpec((POOL_HALO, GROUP_W),
                         lambda b, i: (jnp.maximum((b * ns + i) * halo_blocks - 1, 0), PF_CIN // 2)),
            tile(PF_CZ), tile(PF_DU), tile(PF_DV), tile(PF_DZ),
            per_layer((GROUP_W, GROUP_W)),
            per_layer((1, GROUP_W)),
            per_layer((1, GROUP_W)),
            per_layer((1, GROUP_W)),
            per_layer((N_HEADS, SG_CHUNK, SG_CHUNK)),
            per_layer((SG_CHUNK, GROUP_W)),
        ],
        out_specs=[
            pl.BlockSpec((ts, GROUP_W), lambda b, i: (b * ns + i, 0)),
            pl.BlockSpec((ts, GROUP_W), lambda b, i: (b * ns + i, 0)),
        ],
        out_shape=[
            jax.ShapeDtypeStruct((nb * seq, GROUP_W), BF16),
            jax.ShapeDtypeStruct((nb * seq, GROUP_W), BF16),
        ],
        compiler_params=pltpu.CompilerParams(
            dimension_semantics=("arbitrary", "arbitrary"), vmem_limit_bytes=48 * 1024 * 1024),
        name="pool_sgu",
    )(pf, pf, pf, pf, pf, pf, wpool_bd, pool_scale, ln_g, ln_b, w_sp, bsp_exp)


def _out_proj_kernel(ya_ref, yb_ref, yc_ref, yd_ref, w_ref, x_ref, g_ref, o_ref):
    out = _dot(ya_ref[...], w_ref[0 * GROUP_W:1 * GROUP_W, :])
    out = out + _dot(yb_ref[...], w_ref[1 * GROUP_W:2 * GROUP_W, :])
    out = out + _dot(yc_ref[...], w_ref[2 * GROUP_W:3 * GROUP_W, :])
    out = out + _dot(yd_ref[...], w_ref[3 * GROUP_W:4 * GROUP_W, :])
    inv = lax.rsqrt(jnp.mean(out * out, axis=-1, keepdims=True) + RMS_EPS)
    o_ref[...] = x_ref[...] + out * inv * g_ref[...]


def _out_proj(ya, yb, yc, yd, w_out, x2, g_post, layer):
    m, d = x2.shape
    tm = TM_PROJ
    ytile = pl.BlockSpec((tm, GROUP_W), lambda i: (i, 0))
    return pl.pallas_call(
        _out_proj_kernel,
        grid=(m // tm,),
        in_specs=[
            ytile, ytile, ytile, ytile,
            pl.BlockSpec((None, 4 * GROUP_W, d), lambda i: (layer, 0, 0)),
            pl.BlockSpec((tm, d), lambda i: (i, 0)),
            pl.BlockSpec((None, 1, d), lambda i: (layer, 0, 0)),
        ],
        out_specs=pl.BlockSpec((tm, d), lambda i: (i, 0)),
        out_shape=jax.ShapeDtypeStruct((m, d), F32),
        compiler_params=pltpu.CompilerParams(
            dimension_semantics=("arbitrary",), vmem_limit_bytes=48 * 1024 * 1024),
        name="out_proj",
    )(ya, yb, yc, yd, w_out, x2, g_post)


def _overlap_t(seq):
    n_cmp = (seq - CMP_LEN) // CMP_STRIDE + 1
    n_slc = seq // SEL_BLOCK
    cs = np.arange(n_cmp) * CMP_STRIDE
    ss = np.arange(n_slc) * SEL_BLOCK
    ov = (cs[None, :] < ss[:, None] + SEL_BLOCK) & (cs[None, :] + CMP_LEN > ss[:, None])
    out = np.zeros((LANES, LANES), np.float32)
    out[:n_slc, :n_cmp] = ov
    return out


def _block_expand(seq):
    key = np.arange(seq)
    ex = (key[None, :] // SEL_BLOCK == np.arange(LANES)[:, None]).astype(np.float32)
    return ex.reshape(LANES, seq // TK_SEL, TK_SEL).transpose(1, 0, 2)


def _dilated_log_multiplicity(seq):
    t = T_DIL
    d0 = np.arange(t)[:, None] - np.arange(t)[None, :]
    tabs = []
    for delta in range(seq // t):
        d = d0 + delta * t
        mult = np.zeros_like(d)
        for window, dil in DILATED_PAIRS:
            mult += (d >= 0) & (d % dil == 0) & (d // dil <= window // dil) & (d // dil <= seq // dil - 1)
        tabs.append(np.where(mult > 0, np.log(np.maximum(mult, 1)), NEG))
    return np.stack(tabs).astype(np.float32)


def _regroup_w_in(w_in):
    gw = GROUP_W
    o = 0
    a_q = w_in[..., o:o + gw]; o += gw
    a_kv = w_in[..., o:o + 6 * HEAD_DIM]; o += 6 * HEAD_DIM
    a_g = w_in[..., o:o + N_GATES]; o += N_GATES
    a_z = w_in[..., o:o + gw]; o += gw
    b_qkv = w_in[..., o:o + 3 * gw]; o += 3 * gw
    rest = w_in[..., o:]
    pad = jnp.zeros(w_in.shape[:-1] + (LANES - N_GATES,), w_in.dtype)
    cols = [a_q, a_kv[..., 2 * HEAD_DIM:], b_qkv,
            a_kv[..., :2 * HEAD_DIM], a_g, pad, a_z, rest]
    out = jnp.concatenate(cols, axis=-1)
    assert out.shape[-1] == P_COLS, out.shape
    return out.astype(BF16)


def kernel(x, g_pre, w_in, pe_cmp, w_cmp1, w_cmp2, w_pool, pool_scale, sg_ln_g, sg_ln_b, w_sp, b_sp, w_out, g_post):
    nb, seq, d = x.shape
    depth = w_in.shape[0]
    assert seq % TK_SEL == 0 and seq % T_DIL == 0 and seq % TS_POOL == 0 and (nb * seq) % TM_PROJ == 0
    assert seq // SEL_BLOCK <= LANES and (seq - CMP_LEN) // CMP_STRIDE + 1 < LANES
    ng = seq // CMP_STRIDE
    half = CMP_STRIDE * HEAD_DIM

    w_perm = _regroup_w_in(w_in)
    w_out_b = w_out.astype(BF16)
    w1_b = w_cmp1.astype(BF16)
    w2_b = w_cmp2.astype(BF16)
    pe2 = pe_cmp.reshape(depth, 2, 2, half)
    eye = jnp.eye(N_HEADS, dtype=w_pool.dtype)
    wpool_bd = jnp.einsum('lgcd,gh->lgchd', w_pool, eye).reshape(depth, GROUP_W, GROUP_W).astype(BF16)
    bsp_exp = jnp.repeat(jnp.swapaxes(b_sp, 1, 2), HEAD_DIM, axis=2)
    g_pre3 = g_pre.reshape(depth, 1, d)
    g_post3 = g_post.reshape(depth, 1, d)
    pscale3 = pool_scale.reshape(depth, 1, GROUP_W)
    lng3 = sg_ln_g.reshape(depth, 1, GROUP_W)
    lnb3 = sg_ln_b.reshape(depth, 1, GROUP_W)
    ovl_t = jnp.asarray(_overlap_t(seq), BF16)
    expand = jnp.asarray(_block_expand(seq), BF16)
    dil_bias = jnp.asarray(_dilated_log_multiplicity(seq))

    x2 = x.reshape(nb * seq, d)
    for layer in range(depth):
        pb, pf = _in_proj(x2, g_pre3, w_perm, layer)
        gk = pf[:, PF_CMP * LANES:(PF_CMP + 1) * LANES].reshape(nb, ng, CMP_STRIDE, 2, HEAD_DIM)
        gk = gk.transpose(0, 3, 1, 2, 4).reshape(nb, 2, ng, half)
        kcvc = _compress(gk, pe2, w1_b, w2_b, layer)
        ya = _nsa(pb, pf, kcvc, ovl_t, expand, nb, seq)
        yb = _dilated(pb, pf, dil_bias, nb, seq)
        yc, yd = _pool_sgu(pf, wpool_bd, pscale3, lng3, lnb3, w_sp, bsp_exp, layer, nb, seq)
        x2 = _out_proj(ya, yb, yc, yd, w_out_b, x2, g_post3, layer)
    return x2.reshape(nb, seq, d)
```

```python
import functools

import numpy as np
import jax
import jax.numpy as jnp
from jax import lax
from jax.experimental import pallas as pl
from jax.experimental.pallas import tpu as pltpu

F32 = jnp.float32
BF16 = jnp.bfloat16

HEAD_DIM = 64
N_HEADS = 4
GROUP_W = N_HEADS * HEAD_DIM
CMP_LEN = 32
CMP_STRIDE = 16
CMP_HIDDEN = 256
SEL_BLOCK = 64
SEL_TOPK = 16
WIN = 512
FORCE = 1e4
DILATED_PAIRS = ((128, 1), (512, 4), (2048, 16))
POOL_SIZES = (2, 4, 8, 16)
SG_CHUNK = 128
RMS_EPS = 1e-6
LN_EPS = 1e-5
NEG = -1e30

LANES = 128
N_GATES = 3 * N_HEADS

PB_AQ, PB_SLC, PB_WIN, PB_BQ, PB_BK, PB_BV = 0, 2, 3, 4, 6, 8
PB_COLS = 10 * LANES
PF_CMP, PF_AG, PF_AZ, PF_BZ, PF_CIN, PF_CZ, PF_DU, PF_DV, PF_DZ = 0, 1, 2, 4, 6, 8, 10, 12, 14
PF_COLS = 16 * LANES
P_COLS = PB_COLS + PF_COLS

TM_PROJ = 256
TQ_NSA = 128
TK_SEL = 256
T_DIL = 256
TS_POOL = 512
POOL_HALO = 16


def _sigmoid(x):
    return 1.0 / (1.0 + jnp.exp(-x))


def _silu(x):
    return x * _sigmoid(x)


def _dot_nt(a, b):
    return lax.dot_general(a, b, (((1,), (1,)), ((), ())), preferred_element_type=F32)


def _dot(a, b):
    return jnp.dot(a, b, preferred_element_type=F32)


def _in_proj_kernel(x_ref, g_ref, w_ref, pb_ref, pf_ref):
    x = x_ref[...]
    inv = lax.rsqrt(jnp.mean(x * x, axis=-1, keepdims=True) + RMS_EPS)
    h = (x * inv * g_ref[...]).astype(BF16)
    cw = 2 * LANES
    for c in range(PB_COLS // cw):
        pb_ref[:, c * cw:(c + 1) * cw] = _dot(h, w_ref[:, c * cw:(c + 1) * cw]).astype(BF16)
    for c in range(PF_COLS // cw):
        pf_ref[:, c * cw:(c + 1) * cw] = _dot(h, w_ref[:, PB_COLS + c * cw:PB_COLS + (c + 1) * cw])


def _in_proj(x2, g_pre, w_perm, layer):
    m, d = x2.shape
    return pl.pallas_call(
        _in_proj_kernel,
        grid=(m // TM_PROJ,),
        in_specs=[
            pl.BlockSpec((TM_PROJ, d), lambda i: (i, 0)),
            pl.BlockSpec((None, 1, d), lambda i: (layer, 0, 0)),
            pl.BlockSpec((None, d, P_COLS), lambda i: (layer, 0, 0)),
        ],
        out_specs=[
            pl.BlockSpec((TM_PROJ, PB_COLS), lambda i: (i, 0)),
            pl.BlockSpec((TM_PROJ, PF_COLS), lambda i: (i, 0)),
        ],
        out_shape=[
            jax.ShapeDtypeStruct((m, PB_COLS), BF16),
            jax.ShapeDtypeStruct((m, PF_COLS), F32),
        ],
        compiler_params=pltpu.CompilerParams(
            dimension_semantics=("arbitrary",), vmem_limit_bytes=48 * 1024 * 1024),
        name="in_proj",
    )(x2, g_pre, w_perm)


def _gelu_tanh(x):
    return 0.5 * x * (1.0 + jnp.tanh(np.sqrt(2.0 / np.pi).astype(np.float32) * (x + 0.044715 * (x * x * x))))


def _compress_kernel(g_ref, pe_ref, w1_ref, w2_ref, o_ref):
    nb, _, ng, half = g_ref.shape
    for j in range(2):
        g = g_ref[:, j].reshape(nb * ng, half)
        lo = _dot((g + pe_ref[j, 0:1, :]).astype(BF16), w1_ref[j, 0:half, :])
        hi = _dot((g + pe_ref[j, 1:2, :]).astype(BF16), w1_ref[j, half:2 * half, :])
        hid = _gelu_tanh(lo + pltpu.roll(hi, nb * ng - 1, 0))
        comp = _dot(hid.astype(BF16), w2_ref[j])
        row = lax.broadcasted_iota(jnp.int32, comp.shape, 0) % ng
        comp = jnp.where(row < ng - 1, comp, 0.0)
        o_ref[:, :, j * HEAD_DIM:(j + 1) * HEAD_DIM] = comp.reshape(nb, ng, HEAD_DIM).astype(BF16)


def _compress(gk, pe2, w1, w2, layer):
    nb, _, ng, half = gk.shape
    return pl.pallas_call(
        _compress_kernel,
        grid=(1,),
        in_specs=[
            pl.BlockSpec(gk.shape, lambda i: (0, 0, 0, 0)),
            pl.BlockSpec((None, 2, 2, half), lambda i: (layer, 0, 0, 0)),
            pl.BlockSpec((None, 2, 2 * half, CMP_HIDDEN), lambda i: (layer, 0, 0, 0)),
            pl.BlockSpec((None, 2, CMP_HIDDEN, HEAD_DIM), lambda i: (layer, 0, 0, 0)),
        ],
        out_specs=pl.BlockSpec((nb, ng, LANES), lambda i: (0, 0, 0)),
        out_shape=jax.ShapeDtypeStruct((nb, ng, LANES), BF16),
        compiler_params=pltpu.CompilerParams(
            dimension_semantics=("arbitrary",), vmem_limit_bytes=48 * 1024 * 1024),
        name="compress",
    )(gk, pe2, w1, w2)


def _nsa_kernel(q_ref, slc_ref, win_ref, kc_ref, gate_ref, z_ref, ovl_ref, tri_ref, band_ref, o_ref,
                ksel_ref, vtsel_ref, kwin_ref, vtwin_ref, kct_ref, acc_ref, s_ref):
    tq = q_ref.shape[0]
    seq = slc_ref.shape[0]
    tk = s_ref.shape[0]
    nh = N_HEADS
    n_slc = seq // SEL_BLOCK
    qi = pl.program_id(1)
    q0 = qi * tq

    @pl.when(qi == 0)
    def _():
        lane_k = lax.broadcasted_iota(jnp.int32, (tk, LANES), 1)
        for j in range(seq // tk):
            rows = slice(j * tk, (j + 1) * tk)
            blk = slc_ref[rows, :]
            key_blk = (j * tk + lax.broadcasted_iota(jnp.int32, (tk, LANES), 0)) // SEL_BLOCK
            onehot = jnp.where(lane_k - HEAD_DIM == key_blk, 1.0, 0.0).astype(BF16)
            ksel_ref[rows, :] = jnp.where(lane_k < HEAD_DIM, blk, onehot)
            vtsel_ref[j] = blk.astype(F32).T.astype(BF16)
        lane_w = lax.broadcasted_iota(jnp.int32, (WIN, LANES), 1)
        kwin_ref[0:WIN, :] = jnp.where(lane_w == HEAD_DIM, 1.0, 0.0).astype(BF16)
        for j in range(WIN // LANES):
            vtwin_ref[j] = jnp.zeros((LANES, LANES), BF16)
        lane_t = lax.broadcasted_iota(jnp.int32, (LANES, LANES), 1)
        for j in range(seq // LANES):
            blk = win_ref[j * LANES:(j + 1) * LANES, :]
            kwin_ref[WIN + j * LANES:WIN + (j + 1) * LANES, :] = jnp.where(lane_t < HEAD_DIM, blk, jnp.zeros_like(blk))
            vtwin_ref[WIN // LANES + j] = blk.astype(F32).T.astype(BF16)
        kct_ref[...] = kc_ref[...].astype(F32).T.astype(BF16)

    lane = lax.broadcasted_iota(jnp.int32, (tq, LANES), 1)
    low = lane < HEAD_DIM
    q = q_ref[...].astype(F32) * (HEAD_DIM ** -0.5)
    q_heads = []
    for h in range(nh):
        blk = q[:, LANES * (h // 2):LANES * (h // 2 + 1)]
        if h % 2:
            blk = pltpu.roll(blk, HEAD_DIM, 1)
        q_heads.append(jnp.where(low, blk, 0.0))
    qs_plain = jnp.concatenate(q_heads, axis=0).astype(BF16)

    def tile4(x):
        return jnp.concatenate([x] * nh, axis=1)

    span = WIN + tq
    n_t = span // LANES
    pad_neg = jnp.where(lane == HEAD_DIM, NEG, 0.0)
    qs_win = jnp.concatenate([qh + pad_neg for qh in q_heads], axis=0).astype(BF16)
    s_w = _dot_nt(kwin_ref[pl.ds(pl.multiple_of(q0, tq), span), :], qs_win)
    s_w = jnp.concatenate([s_w[0:LANES] + tile4(band_ref[0]), s_w[LANES:span - LANES],
                           s_w[span - LANES:] + tile4(band_ref[1])], axis=0)
    e_w = jnp.exp(s_w - jnp.max(s_w, axis=0, keepdims=True))
    l_w = jnp.sum(e_w, axis=0, keepdims=True)
    vt_w = jnp.concatenate([vtwin_ref[qi + d] for d in range(n_t)], axis=1)
    o_win = _dot(vt_w, e_w.astype(BF16)) / l_w

    kc = kc_ref[...]
    n_cmp = (seq - CMP_LEN) // CMP_STRIDE + 1
    n_idx = lax.broadcasted_iota(jnp.int32, (LANES, tq), 0)
    t_idx = q0 + lax.broadcasted_iota(jnp.int32, (LANES, tq), 1)
    valid_c = tile4((n_idx * CMP_STRIDE + (CMP_LEN - 1) <= t_idx) & (n_idx < n_cmp))
    s_c = jnp.where(valid_c, _dot_nt(kc, qs_plain), NEG)
    e_c = jnp.exp(s_c - jnp.max(s_c, axis=0, keepdims=True))
    p_c = jnp.where(valid_c, e_c / jnp.sum(e_c, axis=0, keepdims=True), 0.0)
    o_cmp = _dot(kct_ref[...], p_c.astype(BF16))

    psum = p_c[:, 0:tq]
    for h in range(1, nh):
        psum = psum + p_c[:, h * tq:(h + 1) * tq]
    p_hi = psum.astype(BF16)
    r1 = psum - p_hi.astype(F32)
    p_mid = r1.astype(BF16)
    p_lo = (r1 - p_mid.astype(F32)).astype(BF16)
    ovl = ovl_ref[...]
    imp = _dot(ovl, p_hi) + _dot(ovl, p_mid) + _dot(ovl, p_lo)
    v = imp[0:n_slc]
    jj = lax.broadcasted_iota(jnp.int32, (n_slc, tq), 0)
    cur = (q0 + lax.broadcasted_iota(jnp.int32, (n_slc, tq), 1)) // SEL_BLOCK
    forced = (jj == 0) | (jj == cur) | (jj == cur - 1)
    v = jnp.where(forced, FORCE, jnp.where(jj <= cur, v, -FORCE))
    rank = jnp.zeros((n_slc, tq), F32)
    for j2 in range(n_slc):
        vj = v[j2:j2 + 1, :]
        ahead = (vj > v) | ((vj == v) & (j2 < jj))
        rank = rank + jnp.where(ahead, 1.0, 0.0)
    sel_neg = jnp.where((rank < min(SEL_TOPK, n_slc)) & (jj <= cur), 0.0, NEG)
    sel_neg = jnp.concatenate([jnp.zeros((HEAD_DIM, tq), F32), sel_neg,
                               jnp.zeros((LANES - HEAD_DIM - n_slc, tq), F32)], axis=0).T
    qs_sel = jnp.concatenate([qh + sel_neg for qh in q_heads], axis=0).astype(BF16)

    def scores(c):
        return _dot_nt(ksel_ref[pl.ds(pl.multiple_of(c * tk, tk), tk), :], qs_sel)

    def absorb(c, s, m, l):
        m_new = jnp.maximum(m, jnp.max(s, axis=0, keepdims=True))
        alpha = jnp.exp(m - m_new)
        p = jnp.exp(s - m_new)
        l = alpha * l + jnp.sum(p, axis=0, keepdims=True)
        acc_ref[...] = alpha * acc_ref[...] + _dot(vtsel_ref[c], p.astype(BF16))
        return m_new, l

    n_chunks = (q0 + tq + tk - 1) // tk
    acc_ref[...] = jnp.zeros_like(acc_ref)
    s_ref[...] = scores(0)

    def sel_step(c, carry):
        s = s_ref[...]
        s_next = scores(c + 1)
        carry = absorb(c, s, *carry)
        s_ref[...] = s_next
        return carry

    stats = (jnp.full((1, nh * tq), NEG, F32), jnp.zeros((1, nh * tq), F32))
    stats = lax.fori_loop(0, n_chunks - 1, sel_step, stats)
    last = n_chunks - 1
    tri = tri_ref[(q0 - last * tk) // tq]
    _, l_s = absorb(last, s_ref[...] + tile4(tri), *stats)
    o_slc = acc_ref[...] / l_s

    gate_t = _sigmoid(gate_ref[...]).T
    combs = []
    for h in range(nh):
        cols = slice(h * tq, (h + 1) * tq)
        combs.append(gate_t[0 * nh + h:0 * nh + h + 1, :] * o_cmp[HEAD_DIM:, cols]
                     + gate_t[1 * nh + h:1 * nh + h + 1, :] * o_slc[HEAD_DIM:, cols]
                     + gate_t[2 * nh + h:2 * nh + h + 1, :] * o_win[HEAD_DIM:, cols])
    for c in range(nh // 2):
        blk = jnp.concatenate([combs[2 * c], combs[2 * c + 1]], axis=0).T
        zc = z_ref[:, c * LANES:(c + 1) * LANES]
        o_ref[:, c * LANES:(c + 1) * LANES] = (blk * _silu(zc)).astype(o_ref.dtype)


def _nsa(pb, pf, kcvc, ovl, tri, band, nb, seq):
    tq = TQ_NSA
    tk = TK_SEL
    nq = seq // tq
    return pl.pallas_call(
        _nsa_kernel,
        grid=(nb, nq),
        in_specs=[
            pl.BlockSpec((tq, 2 * LANES), lambda b, i: (b * nq + i, PB_AQ // 2)),
            pl.BlockSpec((seq, LANES), lambda b, i: (b, PB_SLC)),
            pl.BlockSpec((seq, LANES), lambda b, i: (b, PB_WIN)),
            pl.BlockSpec((None, LANES, LANES), lambda b, i: (b, 0, 0)),
            pl.BlockSpec((tq, LANES), lambda b, i: (b * nq + i, PF_AG)),
            pl.BlockSpec((tq, 2 * LANES), lambda b, i: (b * nq + i, PF_AZ // 2)),
            pl.BlockSpec((LANES, LANES), lambda b, i: (0, 0)),
            pl.BlockSpec(tri.shape, lambda b, i: (0, 0, 0)),
            pl.BlockSpec(band.shape, lambda b, i: (0, 0, 0)),
        ],
        out_specs=pl.BlockSpec((tq, GROUP_W), lambda b, i: (b * nq + i, 0)),
        out_shape=jax.ShapeDtypeStruct((nb * seq, GROUP_W), BF16),
        scratch_shapes=[
            pltpu.VMEM((seq, LANES), BF16),
            pltpu.VMEM((seq // tk, LANES, tk), BF16),
            pltpu.VMEM((WIN + seq, LANES), BF16),
            pltpu.VMEM(((WIN + seq) // LANES, LANES, LANES), BF16),
            pltpu.VMEM((LANES, LANES), BF16),
            pltpu.VMEM((LANES, N_HEADS * tq), F32),
            pltpu.VMEM((tk, N_HEADS * tq), F32),
        ],
        compiler_params=pltpu.CompilerParams(
            dimension_semantics=("arbitrary", "arbitrary"), vmem_limit_bytes=48 * 1024 * 1024),
        name="nsa",
    )(pb, pb, pb, kcvc, pf, pf, ovl, tri, band)


def _dilated_kernel(q_ref, k_ref, v_ref, z_ref, bias_ref, o_ref, vt_ref, acc_ref, s_ref):
    t = q_ref.shape[0]
    seq = k_ref.shape[0]
    n_pairs = N_HEADS // 2
    qi = pl.program_id(1)

    @pl.when(qi == 0)
    def _():
        for j in range(seq // t):
            vt_ref[j] = v_ref[j * t:(j + 1) * t, :].astype(F32).T.astype(BF16)

    lane = lax.broadcasted_iota(jnp.int32, (t, LANES), 1)
    low = lane < HEAD_DIM
    q = q_ref[...].astype(F32) * (HEAD_DIM ** -0.5)
    qs = []
    for c in range(n_pairs):
        blk = q[:, c * LANES:(c + 1) * LANES]
        qs.append(jnp.concatenate([jnp.where(low, blk, 0.0), jnp.where(low, 0.0, blk)], axis=0).astype(BF16))
    acc_ref[...] = jnp.zeros_like(acc_ref)

    def scores(j, c):
        k0 = pl.multiple_of(j * t, t)
        return _dot_nt(k_ref[pl.ds(k0, t), c * LANES:(c + 1) * LANES], qs[c])

    for c in range(n_pairs):
        s_ref[c] = scores(0, c)

    def step(j, carry):
        j_next = jnp.minimum(j + 1, qi)
        bias = bias_ref[qi - j]
        bias2 = jnp.concatenate([bias, bias], axis=1)
        out = []
        for c in range(n_pairs):
            m, l = carry[c]
            s = s_ref[c] + bias2
            s_next = scores(j_next, c)
            m_new = jnp.maximum(m, jnp.max(s, axis=0, keepdims=True))
            alpha = jnp.exp(m - m_new)
            p = jnp.exp(s - m_new)
            l = alpha * l + jnp.sum(p, axis=0, keepdims=True)
            pv = _dot(vt_ref[j, c * LANES:(c + 1) * LANES, :], p.astype(BF16))
            acc_ref[c] = alpha * acc_ref[c] + pv
            s_ref[c] = s_next
            out.append((m_new, l))
        return tuple(out)

    init = tuple((jnp.full((1, 2 * t), NEG, F32), jnp.zeros((1, 2 * t), F32)) for _ in range(n_pairs))
    stats = lax.fori_loop(0, qi + 1, step, init)
    for c in range(n_pairs):
        o_t = acc_ref[c] / stats[c][1]
        o_pair = jnp.concatenate([o_t[:HEAD_DIM, :t], o_t[HEAD_DIM:, t:]], axis=0).T
        zc = z_ref[:, c * LANES:(c + 1) * LANES]
        o_ref[:, c * LANES:(c + 1) * LANES] = (o_pair * _silu(zc)).astype(o_ref.dtype)


def _dilated(pb, pf, bias_tab, nb, seq):
    t = T_DIL
    nq = seq // t
    return pl.pallas_call(
        _dilated_kernel,
        grid=(nb, nq),
        in_specs=[
            pl.BlockSpec((t, GROUP_W), lambda b, i: (b * nq + i, PB_BQ // 2)),
            pl.BlockSpec((seq, GROUP_W), lambda b, i: (b, PB_BK // 2)),
            pl.BlockSpec((seq, GROUP_W), lambda b, i: (b, PB_BV // 2)),
            pl.BlockSpec((t, GROUP_W), lambda b, i: (b * nq + i, PF_BZ // 2)),
            pl.BlockSpec((nq, t, t), lambda b, i: (0, 0, 0)),
        ],
        out_specs=pl.BlockSpec((t, GROUP_W), lambda b, i: (b * nq + i, 0)),
        out_shape=jax.ShapeDtypeStruct((nb * seq, GROUP_W), BF16),
        scratch_shapes=[
            pltpu.VMEM((nq, GROUP_W, t), BF16),
            pltpu.VMEM((N_HEADS // 2, LANES, 2 * t), F32),
            pltpu.VMEM((N_HEADS // 2, t, 2 * t), F32),
        ],
        compiler_params=pltpu.CompilerParams(
            dimension_semantics=("arbitrary", "arbitrary"), vmem_limit_bytes=48 * 1024 * 1024),
        name="dilated",
    )(pb, pb, pb, pf, bias_tab)


def _pool_sgu_kernel(c_ref, cprev_ref, cz_ref, du_ref, dv_ref, dz_ref,
                     wpool_ref, pscale_ref, lng_ref, lnb_ref, wsp_ref, bsp_ref, yc_ref, yd_ref):
    ts = c_ref.shape[0]
    i = pl.program_id(1)
    lane = lax.broadcasted_iota(jnp.int32, (ts, GROUP_W), 1)

    cur = c_ref[...]
    prev = jnp.where(i > 0, cprev_ref[...], 0.0)
    ext = jnp.concatenate([prev, cur], axis=0)
    t_pos = i * ts + lax.broadcasted_iota(jnp.int32, (ts, 1), 0)
    pooled = None
    acc = ext
    width = 1
    for g, w in enumerate(POOL_SIZES):
        while width < w:
            acc = acc + pltpu.roll(acc, width, 0)
            width *= 2
        cnt = jnp.minimum(t_pos + 1, w).astype(F32)
        mean_w = acc[POOL_HALO:POOL_HALO + ts] * (1.0 / cnt)
        pooled = mean_w if pooled is None else jnp.where(lane >= g * HEAD_DIM, mean_w, pooled)
    pooled = pooled - cur
    mixed = _dot(pooled.astype(BF16), wpool_ref[...]) * pscale_ref[...]
    yc_ref[...] = (mixed * _silu(cz_ref[...])).astype(yc_ref.dtype)

    v = dv_ref[...]
    mu = jnp.mean(v, axis=-1, keepdims=True)
    var = jnp.mean(jnp.square(v - mu), axis=-1, keepdims=True)
    vn = ((v - mu) * lax.rsqrt(var + LN_EPS) * lng_ref[...] + lnb_ref[...]).astype(BF16)
    r = lax.broadcasted_iota(jnp.int32, (SG_CHUNK, SG_CHUNK), 0)
    cidx = lax.broadcasted_iota(jnp.int32, (SG_CHUNK, SG_CHUNK), 1)
    w_tril = [jnp.where(cidx <= r, wsp_ref[g], 0.0).astype(BF16) for g in range(N_HEADS)]
    lane_c = lax.broadcasted_iota(jnp.int32, (SG_CHUNK, GROUP_W), 1)
    for ci in range(ts // SG_CHUNK):
        rows = slice(ci * SG_CHUNK, (ci + 1) * SG_CHUNK)
        vc = vn[rows]
        zmix = _dot(w_tril[0], vc)
        for g in range(1, N_HEADS):
            zmix = jnp.where(lane_c >= g * HEAD_DIM, _dot(w_tril[g], vc), zmix)
        zfull = zmix + bsp_ref[...]
        yd_ref[rows, :] = (du_ref[rows, :] * zfull * _silu(dz_ref[rows, :])).astype(yd_ref.dtype)


def _pool_sgu(pf, wpool_bd, pool_scale, ln_g, ln_b, w_sp, bsp_exp, layer, nb, seq):
    ts = TS_POOL
    ns = seq // ts
    halo_blocks = ts // POOL_HALO

    def tile(col):
        return pl.BlockSpec((ts, GROUP_W), lambda b, i: (b * ns + i, col // 2))

    def per_layer(shape):
        nd = len(shape)
        return pl.BlockSpec((None,) + shape, lambda b, i: (layer,) + (0,) * nd)

    return pl.pallas_call(
        _pool_sgu_kernel,
        grid=(nb, ns),
        in_specs=[
            tile(PF_CIN),
            pl.BlockSpec((POOL_HALO, GROUP_W),
                         lambda b, i: (jnp.maximum((b * ns + i) * halo_blocks - 1, 0), PF_CIN // 2)),
            tile(PF_CZ), tile(PF_DU), tile(PF_DV), tile(PF_DZ),
            per_layer((GROUP_W, GROUP_W)),
            per_layer((1, GROUP_W)),
            per_layer((1, GROUP_W)),
            per_layer((1, GROUP_W)),
            per_layer((N_HEADS, SG_CHUNK, SG_CHUNK)),
            per_layer((SG_CHUNK, GROUP_W)),
        ],
        out_specs=[
            pl.BlockSpec((ts, GROUP_W), lambda b, i: (b * ns + i, 0)),
            pl.BlockSpec((ts, GROUP_W), lambda b, i: (b * ns + i, 0)),
        ],
        out_shape=[
            jax.ShapeDtypeStruct((nb * seq, GROUP_W), BF16),
            jax.ShapeDtypeStruct((nb * seq, GROUP_W), BF16),
        ],
        compiler_params=pltpu.CompilerParams(
            dimension_semantics=("arbitrary", "arbitrary"), vmem_limit_bytes=48 * 1024 * 1024),
        name="pool_sgu",
    )(pf, pf, pf, pf, pf, pf, wpool_bd, pool_scale, ln_g, ln_b, w_sp, bsp_exp)


def _out_proj_kernel(ya_ref, yb_ref, yc_ref, yd_ref, w_ref, x_ref, g_ref, o_ref):
    out = _dot(ya_ref[...], w_ref[0 * GROUP_W:1 * GROUP_W, :])
    out = out + _dot(yb_ref[...], w_ref[1 * GROUP_W:2 * GROUP_W, :])
    out = out + _dot(yc_ref[...], w_ref[2 * GROUP_W:3 * GROUP_W, :])
    out = out + _dot(yd_ref[...], w_ref[3 * GROUP_W:4 * GROUP_W, :])
    inv = lax.rsqrt(jnp.mean(out * out, axis=-1, keepdims=True) + RMS_EPS)
    o_ref[...] = x_ref[...] + out * inv * g_ref[...]


def _out_proj(ya, yb, yc, yd, w_out, x2, g_post, layer):
    m, d = x2.shape
    tm = TM_PROJ
    ytile = pl.BlockSpec((tm, GROUP_W), lambda i: (i, 0))
    return pl.pallas_call(
        _out_proj_kernel,
        grid=(m // tm,),
        in_specs=[
            ytile, ytile, ytile, ytile,
            pl.BlockSpec((None, 4 * GROUP_W, d), lambda i: (layer, 0, 0)),
            pl.BlockSpec((tm, d), lambda i: (i, 0)),
            pl.BlockSpec((None, 1, d), lambda i: (layer, 0, 0)),
        ],
        out_specs=pl.BlockSpec((tm, d), lambda i: (i, 0)),
        out_shape=jax.ShapeDtypeStruct((m, d), F32),
        compiler_params=pltpu.CompilerParams(
            dimension_semantics=("arbitrary",), vmem_limit_bytes=48 * 1024 * 1024),
        name="out_proj",
    )(ya, yb, yc, yd, w_out, x2, g_post)


def _overlap_t(seq):
    n_cmp = (seq - CMP_LEN) // CMP_STRIDE + 1
    n_slc = seq // SEL_BLOCK
    cs = np.arange(n_cmp) * CMP_STRIDE
    ss = np.arange(n_slc) * SEL_BLOCK
    ov = (cs[None, :] < ss[:, None] + SEL_BLOCK) & (cs[None, :] + CMP_LEN > ss[:, None])
    out = np.zeros((LANES, LANES), np.float32)
    out[:n_slc, :n_cmp] = ov
    return out


def _causal_tri():
    a = np.arange(TK_SEL)[:, None]
    b = np.arange(TQ_NSA)[None, :]
    return np.stack([np.where(a - o * TQ_NSA <= b, 0.0, NEG) for o in range(TK_SEL // TQ_NSA)]).astype(np.float32)


def _window_band():
    a = np.arange(LANES)[:, None]
    b = np.arange(LANES)[None, :]
    return np.stack([np.where(a > b, 0.0, NEG), np.where(a <= b, 0.0, NEG)]).astype(np.float32)


def _dilated_log_multiplicity(seq):
    t = T_DIL
    d0 = np.arange(t)[:, None] - np.arange(t)[None, :]
    tabs = []
    for delta in range(seq // t):
        d = d0 + delta * t
        mult = np.zeros_like(d)
        for window, dil in DILATED_PAIRS:
            mult += (d >= 0) & (d % dil == 0) & (d // dil <= window // dil) & (d // dil <= seq // dil - 1)
        tabs.append(np.where(mult > 0, np.log(np.maximum(mult, 1)), NEG).T)
    return np.stack(tabs).astype(np.float32)


def _regroup_w_in(w_in):
    gw = GROUP_W
    o = 0
    a_q = w_in[..., o:o + gw]; o += gw
    a_kv = w_in[..., o:o + 6 * HEAD_DIM]; o += 6 * HEAD_DIM
    a_g = w_in[..., o:o + N_GATES]; o += N_GATES
    a_z = w_in[..., o:o + gw]; o += gw
    b_qkv = w_in[..., o:o + 3 * gw]; o += 3 * gw
    rest = w_in[..., o:]
    pad = jnp.zeros(w_in.shape[:-1] + (LANES - N_GATES,), w_in.dtype)
    cols = [a_q, a_kv[..., 2 * HEAD_DIM:], b_qkv,
            a_kv[..., :2 * HEAD_DIM], a_g, pad, a_z, rest]
    out = jnp.concatenate(cols, axis=-1)
    assert out.shape[-1] == P_COLS, out.shape
    return out.astype(BF16)


def kernel(x, g_pre, w_in, pe_cmp, w_cmp1, w_cmp2, w_pool, pool_scale, sg_ln_g, sg_ln_b, w_sp, b_sp, w_out, g_post):
    nb, seq, d = x.shape
    depth = w_in.shape[0]
    assert seq % TK_SEL == 0 and seq % T_DIL == 0 and seq % TS_POOL == 0 and (nb * seq) % TM_PROJ == 0
    assert TQ_NSA == LANES and TK_SEL % TQ_NSA == 0 and WIN % TQ_NSA == 0 and seq >= WIN + TQ_NSA
    assert seq // SEL_BLOCK <= LANES - HEAD_DIM and (seq - CMP_LEN) // CMP_STRIDE + 1 < LANES
    ng = seq // CMP_STRIDE
    half = CMP_STRIDE * HEAD_DIM

    w_perm = _regroup_w_in(w_in)
    w_out_b = w_out.astype(BF16)
    w1_b = w_cmp1.astype(BF16)
    w2_b = w_cmp2.astype(BF16)
    pe2 = pe_cmp.reshape(depth, 2, 2, half)
    eye = jnp.eye(N_HEADS, dtype=w_pool.dtype)
    wpool_bd = jnp.einsum('lgcd,gh->lgchd', w_pool, eye).reshape(depth, GROUP_W, GROUP_W).astype(BF16)
    bsp_exp = jnp.repeat(jnp.swapaxes(b_sp, 1, 2), HEAD_DIM, axis=2)
    g_pre3 = g_pre.reshape(depth, 1, d)
    g_post3 = g_post.reshape(depth, 1, d)
    pscale3 = pool_scale.reshape(depth, 1, GROUP_W)
    lng3 = sg_ln_g.reshape(depth, 1, GROUP_W)
    lnb3 = sg_ln_b.reshape(depth, 1, GROUP_W)
    ovl_t = jnp.asarray(_overlap_t(seq), BF16)
    tri = jnp.asarray(_causal_tri())
    band = jnp.asarray(_window_band())
    dil_bias = jnp.asarray(_dilated_log_multiplicity(seq))

    x2 = x.reshape(nb * seq, d)
    for layer in range(depth):
        pb, pf = _in_proj(x2, g_pre3, w_perm, layer)
        gk = pf[:, PF_CMP * LANES:(PF_CMP + 1) * LANES].reshape(nb, ng, CMP_STRIDE, 2, HEAD_DIM)
        gk = gk.transpose(0, 3, 1, 2, 4).reshape(nb, 2, ng, half)
        kcvc = _compress(gk, pe2, w1_b, w2_b, layer)
        ya = _nsa(pb, pf, kcvc, ovl_t, tri, band, nb, seq)
        yb = _dilated(pb, pf, dil_bias, nb, seq)
        yc, yd = _pool_sgu(pf, wpool_bd, pscale3, lng3, lnb3, w_sp, bsp_exp, layer, nb, seq)
        x2 = _out_proj(ya, yb, yc, yd, w_out_b, x2, g_post3, layer)
    return x2.reshape(nb, seq, d)
```

```python
import functools

import numpy as np
import jax
import jax.numpy as jnp
from jax import lax
from jax.experimental import pallas as pl
from jax.experimental.pallas import tpu as pltpu

F32 = jnp.float32
BF16 = jnp.bfloat16

HEAD_DIM = 64
N_HEADS = 4
GROUP_W = N_HEADS * HEAD_DIM
CMP_LEN = 32
CMP_STRIDE = 16
CMP_HIDDEN = 256
SEL_BLOCK = 64
SEL_TOPK = 16
WIN = 512
FORCE = 1e4
DILATED_PAIRS = ((128, 1), (512, 4), (2048, 16))
POOL_SIZES = (2, 4, 8, 16)
SG_CHUNK = 128
RMS_EPS = 1e-6
LN_EPS = 1e-5
NEG = -1e30

LANES = 128
N_GATES = 3 * N_HEADS

PB_AQ, PB_SLC, PB_WIN, PB_BQ, PB_BK, PB_BV = 0, 2, 3, 4, 6, 8
PB_COLS = 10 * LANES
PF_CMP, PF_AG, PF_AZ, PF_BZ, PF_CIN, PF_CZ, PF_DU, PF_DV, PF_DZ = 0, 1, 2, 4, 6, 8, 10, 12, 14
PF_COLS = 16 * LANES
P_COLS = PB_COLS + PF_COLS

TM_PROJ = 256
TM_OUT = 512
TQ_NSA = 256
TK_SEL = 256
T_DIL = 256
TS_POOL = 512
POOL_HALO = 16
TW_WIN = 128

Q_SCALE = float(np.log2(np.e)) * HEAD_DIM ** -0.5


def _sigmoid(x):
    return 1.0 / (1.0 + jnp.exp(-x))


def _silu(x):
    return x * _sigmoid(x)


def _dot_nt(a, b):
    return lax.dot_general(a, b, (((1,), (1,)), ((), ())), preferred_element_type=F32)


def _dot(a, b):
    return jnp.dot(a, b, preferred_element_type=F32)


def _in_proj_kernel(x_ref, g_ref, w_ref, pb_ref, pf_ref):
    x = x_ref[...]
    inv = lax.rsqrt(jnp.mean(x * x, axis=-1, keepdims=True) + RMS_EPS)
    h = (x * inv * g_ref[...]).astype(BF16)
    cw = 2 * LANES
    for c in range(PB_COLS // cw):
        pb_ref[:, c * cw:(c + 1) * cw] = _dot(h, w_ref[:, c * cw:(c + 1) * cw]).astype(BF16)
    for c in range(PF_COLS // cw):
        pf_ref[:, c * cw:(c + 1) * cw] = _dot(h, w_ref[:, PB_COLS + c * cw:PB_COLS + (c + 1) * cw])


def _in_proj(x2, g_pre, w_perm, layer):
    m, d = x2.shape
    return pl.pallas_call(
        _in_proj_kernel,
        grid=(m // TM_PROJ,),
        in_specs=[
            pl.BlockSpec((TM_PROJ, d), lambda i: (i, 0)),
            pl.BlockSpec((None, 1, d), lambda i: (layer, 0, 0)),
            pl.BlockSpec((None, d, P_COLS), lambda i: (layer, 0, 0)),
        ],
        out_specs=[
            pl.BlockSpec((TM_PROJ, PB_COLS), lambda i: (i, 0)),
            pl.BlockSpec((TM_PROJ, PF_COLS), lambda i: (i, 0)),
        ],
        out_shape=[
            jax.ShapeDtypeStruct((m, PB_COLS), BF16),
            jax.ShapeDtypeStruct((m, PF_COLS), F32),
        ],
        compiler_params=pltpu.CompilerParams(
            dimension_semantics=("arbitrary",), vmem_limit_bytes=48 * 1024 * 1024),
        name="in_proj",
    )(x2, g_pre, w_perm)


def _gelu_tanh(x):
    return 0.5 * x * (1.0 + jnp.tanh(np.sqrt(2.0 / np.pi).astype(np.float32) * (x + 0.044715 * (x * x * x))))


def _compress_kernel(x_ref, pe_ref, w1_ref, w2_ref, o_ref):
    n_grp = x_ref.shape[0] // CMP_STRIDE
    ng = o_ref.shape[1]
    lo = hi = None
    for l0 in range(CMP_STRIDE):
        rows = x_ref[pl.ds(l0, n_grp, stride=CMP_STRIDE), :]
        d_lo = _dot((rows + pe_ref[l0:l0 + 1, :]).astype(BF16), w1_ref[l0])
        d_hi = _dot((rows + pe_ref[CMP_STRIDE + l0:CMP_STRIDE + l0 + 1, :]).astype(BF16), w1_ref[CMP_STRIDE + l0])
        lo = d_lo if lo is None else lo + d_lo
        hi = d_hi if hi is None else hi + d_hi
    hid = _gelu_tanh(lo + pltpu.roll(hi, n_grp - 1, 0)).astype(BF16)
    row = lax.broadcasted_iota(jnp.int32, (n_grp, HEAD_DIM), 0) % ng
    for j in range(2):
        comp = _dot(hid[:, j * CMP_HIDDEN:(j + 1) * CMP_HIDDEN], w2_ref[j])
        comp = jnp.where(row < ng - 1, comp, 0.0)
        o_ref[:, :, j * HEAD_DIM:(j + 1) * HEAD_DIM] = comp.reshape(o_ref.shape[0], ng, HEAD_DIM).astype(BF16)


def _compress(pf, pe_rows, w1_bd, w2, layer, nb, seq):
    ng = seq // CMP_STRIDE
    return pl.pallas_call(
        _compress_kernel,
        grid=(1,),
        in_specs=[
            pl.BlockSpec((nb * seq, LANES), lambda i: (0, PF_CMP)),
            pl.BlockSpec((None, CMP_LEN, LANES), lambda i: (layer, 0, 0)),
            pl.BlockSpec((None, CMP_LEN, LANES, 2 * CMP_HIDDEN), lambda i: (layer, 0, 0, 0)),
            pl.BlockSpec((None, 2, CMP_HIDDEN, HEAD_DIM), lambda i: (layer, 0, 0, 0)),
        ],
        out_specs=pl.BlockSpec((nb, ng, LANES), lambda i: (0, 0, 0)),
        out_shape=jax.ShapeDtypeStruct((nb, ng, LANES), BF16),
        compiler_params=pltpu.CompilerParams(
            dimension_semantics=("arbitrary",), vmem_limit_bytes=48 * 1024 * 1024),
        name="compress",
    )(pf, pe_rows, w1_bd, w2)


def _nsa_kernel(q_ref, slc_ref, win_ref, kc_ref, gate_ref, z_ref, ovl_ref, tri_ref, band_ref, o_ref,
                ksel_ref, vtsel_ref, kwin_ref, vtwin_ref, kct_ref, acc_ref, s_ref):
    tq = q_ref.shape[0]
    seq = slc_ref.shape[0]
    tk = s_ref.shape[0]
    tw = band_ref.shape[1]
    nh = N_HEADS
    n_slc = seq // SEL_BLOCK
    qi = pl.program_id(1)
    q0 = qi * tq

    @pl.when(qi == 0)
    def _():
        lane_k = lax.broadcasted_iota(jnp.int32, (tk, LANES), 1)
        for j in range(seq // tk):
            rows = slice(j * tk, (j + 1) * tk)
            blk = slc_ref[rows, :]
            key_blk = (j * tk + lax.broadcasted_iota(jnp.int32, (tk, LANES), 0)) // SEL_BLOCK
            onehot = jnp.where(lane_k - HEAD_DIM == key_blk, 1.0, 0.0).astype(BF16)
            ksel_ref[rows, :] = jnp.where(lane_k < HEAD_DIM, blk, onehot)
            vtsel_ref[j] = blk.astype(F32).T.astype(BF16)
        lane_w = lax.broadcasted_iota(jnp.int32, (WIN, LANES), 1)
        kwin_ref[0:WIN, :] = jnp.where(lane_w == HEAD_DIM, 1.0, 0.0).astype(BF16)
        for j in range(WIN // tw):
            vtwin_ref[j] = jnp.zeros((LANES, tw), BF16)
        lane_t = lax.broadcasted_iota(jnp.int32, (tw, LANES), 1)
        for j in range(seq // tw):
            blk = win_ref[j * tw:(j + 1) * tw, :]
            kwin_ref[WIN + j * tw:WIN + (j + 1) * tw, :] = jnp.where(lane_t < HEAD_DIM, blk, jnp.zeros_like(blk))
            vtwin_ref[WIN // tw + j] = blk.astype(F32).T.astype(BF16)
        kct_ref[...] = kc_ref[...].astype(F32).T.astype(BF16)

    lane = lax.broadcasted_iota(jnp.int32, (tq, LANES), 1)
    low = lane < HEAD_DIM
    q = q_ref[...].astype(F32)
    q_heads = []
    for h in range(nh):
        blk = q[:, LANES * (h // 2):LANES * (h // 2 + 1)]
        if h % 2:
            blk = pltpu.roll(blk, HEAD_DIM, 1)
        q_heads.append(jnp.where(low, blk, 0.0))
    qs_plain = jnp.concatenate(q_heads, axis=0).astype(BF16)

    def tile4(x):
        return jnp.concatenate([x] * nh, axis=1)

    span = WIN + tw
    pad_neg = jnp.where(lane == HEAD_DIM, NEG, 0.0)
    edge = [jnp.concatenate([band_ref[e]] * nh, axis=1) for e in range(2)]
    o_win = []
    for u in range(tq // tw):
        qs_win = jnp.concatenate([(qh + pad_neg)[u * tw:(u + 1) * tw] for qh in q_heads], axis=0).astype(BF16)
        start = pl.multiple_of(q0 + u * tw, tw)
        s_w = _dot_nt(kwin_ref[pl.ds(start, span), :], qs_win)
        s_w = jnp.concatenate([s_w[0:tw] + edge[0], s_w[tw:span - tw], s_w[span - tw:] + edge[1]], axis=0)
        e_w = jnp.exp2(s_w - jnp.max(s_w, axis=0, keepdims=True))
        l_w = jnp.sum(e_w, axis=0, keepdims=True)
        t0 = qi * (tq // tw) + u
        vt_w = jnp.concatenate([vtwin_ref[t0 + d] for d in range(span // tw)], axis=1)
        o_win.append(_dot(vt_w, e_w.astype(BF16)) / l_w)

    kc = kc_ref[...]
    n_cmp = (seq - CMP_LEN) // CMP_STRIDE + 1
    n_idx = lax.broadcasted_iota(jnp.int32, (LANES, tq), 0)
    t_idx = q0 + lax.broadcasted_iota(jnp.int32, (LANES, tq), 1)
    valid_c = tile4((n_idx * CMP_STRIDE + (CMP_LEN - 1) <= t_idx) & (n_idx < n_cmp))
    s_c = jnp.where(valid_c, _dot_nt(kc, qs_plain), NEG)
    e_c = jnp.exp2(s_c - jnp.max(s_c, axis=0, keepdims=True))
    p_c = jnp.where(valid_c, e_c / jnp.sum(e_c, axis=0, keepdims=True), 0.0)
    o_cmp = _dot(kct_ref[...], p_c.astype(BF16))

    psum = p_c[:, 0:tq]
    for h in range(1, nh):
        psum = psum + p_c[:, h * tq:(h + 1) * tq]
    p_hi = psum.astype(BF16)
    r1 = psum - p_hi.astype(F32)
    p_mid = r1.astype(BF16)
    p_lo = (r1 - p_mid.astype(F32)).astype(BF16)
    ovl = ovl_ref[...]
    imp = _dot(ovl, p_hi) + _dot(ovl, p_mid) + _dot(ovl, p_lo)
    v = imp[0:n_slc]
    jj = lax.broadcasted_iota(jnp.int32, (n_slc, tq), 0)
    cur = (q0 + lax.broadcasted_iota(jnp.int32, (n_slc, tq), 1)) // SEL_BLOCK
    forced = (jj == 0) | (jj == cur) | (jj == cur - 1)
    v = jnp.where(forced, FORCE, jnp.where(jj <= cur, v, -FORCE))
    sub = 8
    ranks = []
    row8 = lax.broadcasted_iota(jnp.int32, (sub, tq), 0)
    for r0 in range(0, n_slc, sub):
        vr = v[r0:r0 + sub]
        cnt = jnp.zeros((sub, tq), F32)
        for j2 in range(n_slc):
            vj = v[j2:j2 + 1, :]
            if j2 < r0:
                ahead = vj >= vr
            elif j2 >= r0 + sub:
                ahead = vj > vr
            else:
                ahead = (vj > vr) | ((vj == vr) & (row8 > j2 - r0))
            cnt = cnt + jnp.where(ahead, 1.0, 0.0)
        ranks.append(cnt)
    rank = jnp.concatenate(ranks, axis=0)
    sel_neg = jnp.where((rank < min(SEL_TOPK, n_slc)) & (jj <= cur), 0.0, NEG)
    sel_neg = jnp.concatenate([jnp.zeros((HEAD_DIM, tq), F32), sel_neg,
                               jnp.zeros((LANES - HEAD_DIM - n_slc, tq), F32)], axis=0).T
    qs_sel = jnp.concatenate([qh + sel_neg for qh in q_heads], axis=0).astype(BF16)

    def scores(c):
        return _dot_nt(ksel_ref[pl.ds(pl.multiple_of(c * tk, tk), tk), :], qs_sel)

    def absorb(c, s, m, l):
        m_new = jnp.maximum(m, jnp.max(s, axis=0, keepdims=True))
        alpha = jnp.exp2(m - m_new)
        p = jnp.exp2(s - m_new)
        l = alpha * l + jnp.sum(p, axis=0, keepdims=True)
        acc_ref[...] = alpha * acc_ref[...] + _dot(vtsel_ref[c], p.astype(BF16))
        return m_new, l

    n_chunks = (q0 + tq + tk - 1) // tk
    acc_ref[...] = jnp.zeros_like(acc_ref)
    s_ref[...] = scores(0)

    def sel_step(c, carry):
        s = s_ref[...]
        s_next = scores(c + 1)
        carry = absorb(c, s, *carry)
        s_ref[...] = s_next
        return carry

    stats = (jnp.full((1, nh * tq), NEG, F32), jnp.zeros((1, nh * tq), F32))
    stats = lax.fori_loop(0, n_chunks - 1, sel_step, stats)
    last = n_chunks - 1
    tri = tri_ref[(q0 - last * tk) // tq]
    _, l_s = absorb(last, s_ref[...] + tile4(tri), *stats)
    o_slc = acc_ref[...] / l_s

    gate_t = _sigmoid(gate_ref[...]).T
    combs = []
    for h in range(nh):
        cols = slice(h * tq, (h + 1) * tq)
        o_win_h = jnp.concatenate([ow[HEAD_DIM:, h * tw:(h + 1) * tw] for ow in o_win], axis=1)
        combs.append(gate_t[0 * nh + h:0 * nh + h + 1, :] * o_cmp[HEAD_DIM:, cols]
                     + gate_t[1 * nh + h:1 * nh + h + 1, :] * o_slc[HEAD_DIM:, cols]
                     + gate_t[2 * nh + h:2 * nh + h + 1, :] * o_win_h)
    for c in range(nh // 2):
        blk = jnp.concatenate([combs[2 * c], combs[2 * c + 1]], axis=0).T
        zc = z_ref[:, c * LANES:(c + 1) * LANES]
        o_ref[:, c * LANES:(c + 1) * LANES] = (blk * _silu(zc)).astype(o_ref.dtype)


def _nsa(pb, pf, kcvc, ovl, tri, band, nb, seq):
    tq = TQ_NSA
    tk = TK_SEL
    nq = seq // tq
    return pl.pallas_call(
        _nsa_kernel,
        grid=(nb, nq),
        in_specs=[
            pl.BlockSpec((tq, 2 * LANES), lambda b, i: (b * nq + i, PB_AQ // 2)),
            pl.BlockSpec((seq, LANES), lambda b, i: (b, PB_SLC)),
            pl.BlockSpec((seq, LANES), lambda b, i: (b, PB_WIN)),
            pl.BlockSpec((None, LANES, LANES), lambda b, i: (b, 0, 0)),
            pl.BlockSpec((tq, LANES), lambda b, i: (b * nq + i, PF_AG)),
            pl.BlockSpec((tq, 2 * LANES), lambda b, i: (b * nq + i, PF_AZ // 2)),
            pl.BlockSpec((LANES, LANES), lambda b, i: (0, 0)),
            pl.BlockSpec(tri.shape, lambda b, i: (0, 0, 0)),
            pl.BlockSpec(band.shape, lambda b, i: (0, 0, 0)),
        ],
        out_specs=pl.BlockSpec((tq, GROUP_W), lambda b, i: (b * nq + i, 0)),
        out_shape=jax.ShapeDtypeStruct((nb * seq, GROUP_W), BF16),
        scratch_shapes=[
            pltpu.VMEM((seq, LANES), BF16),
            pltpu.VMEM((seq // tk, LANES, tk), BF16),
            pltpu.VMEM((WIN + seq, LANES), BF16),
            pltpu.VMEM(((WIN + seq) // TW_WIN, LANES, TW_WIN), BF16),
            pltpu.VMEM((LANES, LANES), BF16),
            pltpu.VMEM((LANES, N_HEADS * tq), F32),
            pltpu.VMEM((tk, N_HEADS * tq), F32),
        ],
        compiler_params=pltpu.CompilerParams(
            dimension_semantics=("arbitrary", "arbitrary"), vmem_limit_bytes=48 * 1024 * 1024),
        name="nsa",
    )(pb, pb, pb, kcvc, pf, pf, ovl, tri, band)


def _dilated_kernel(q_ref, k_ref, v_ref, z_ref, bias_ref, o_ref, vt_ref, acc_ref, s_ref):
    t = q_ref.shape[0]
    seq = k_ref.shape[0]
    n_pairs = N_HEADS // 2
    qi = pl.program_id(1)

    @pl.when(qi == 0)
    def _():
        for j in range(seq // t):
            vt_ref[j] = v_ref[j * t:(j + 1) * t, :].astype(F32).T.astype(BF16)

    lane = lax.broadcasted_iota(jnp.int32, (t, LANES), 1)
    low = lane < HEAD_DIM
    q = q_ref[...].astype(F32)
    qs = []
    for c in range(n_pairs):
        blk = q[:, c * LANES:(c + 1) * LANES]
        qs.append(jnp.concatenate([jnp.where(low, blk, 0.0), jnp.where(low, 0.0, blk)], axis=0).astype(BF16))
    acc_ref[...] = jnp.zeros_like(acc_ref)

    def scores(j, c):
        k0 = pl.multiple_of(j * t, t)
        return _dot_nt(k_ref[pl.ds(k0, t), c * LANES:(c + 1) * LANES], qs[c])

    for c in range(n_pairs):
        s_ref[c] = scores(0, c)

    def step(j, carry):
        j_next = jnp.minimum(j + 1, qi)
        bias = bias_ref[qi - j]
        bias2 = jnp.concatenate([bias, bias], axis=1)
        out = []
        for c in range(n_pairs):
            m, l = carry[c]
            s = s_ref[c] + bias2
            s_next = scores(j_next, c)
            m_new = jnp.maximum(m, jnp.max(s, axis=0, keepdims=True))
            alpha = jnp.exp2(m - m_new)
            p = jnp.exp2(s - m_new)
            l = alpha * l + jnp.sum(p, axis=0, keepdims=True)
            pv = _dot(vt_ref[j, c * LANES:(c + 1) * LANES, :], p.astype(BF16))
            acc_ref[c] = alpha * acc_ref[c] + pv
            s_ref[c] = s_next
            out.append((m_new, l))
        return tuple(out)

    init = tuple((jnp.full((1, 2 * t), NEG, F32), jnp.zeros((1, 2 * t), F32)) for _ in range(n_pairs))
    stats = lax.fori_loop(0, qi + 1, step, init)
    for c in range(n_pairs):
        o_t = acc_ref[c] / stats[c][1]
        o_pair = jnp.concatenate([o_t[:HEAD_DIM, :t], o_t[HEAD_DIM:, t:]], axis=0).T
        zc = z_ref[:, c * LANES:(c + 1) * LANES]
        o_ref[:, c * LANES:(c + 1) * LANES] = (o_pair * _silu(zc)).astype(o_ref.dtype)


def _dilated(pb, pf, bias_tab, nb, seq):
    t = T_DIL
    nq = seq // t
    return pl.pallas_call(
        _dilated_kernel,
        grid=(nb, nq),
        in_specs=[
            pl.BlockSpec((t, GROUP_W), lambda b, i: (b * nq + i, PB_BQ // 2)),
            pl.BlockSpec((seq, GROUP_W), lambda b, i: (b, PB_BK // 2)),
            pl.BlockSpec((seq, GROUP_W), lambda b, i: (b, PB_BV // 2)),
            pl.BlockSpec((t, GROUP_W), lambda b, i: (b * nq + i, PF_BZ // 2)),
            pl.BlockSpec((nq, t, t), lambda b, i: (0, 0, 0)),
        ],
        out_specs=pl.BlockSpec((t, GROUP_W), lambda b, i: (b * nq + i, 0)),
        out_shape=jax.ShapeDtypeStruct((nb * seq, GROUP_W), BF16),
        scratch_shapes=[
            pltpu.VMEM((nq, GROUP_W, t), BF16),
            pltpu.VMEM((N_HEADS // 2, LANES, 2 * t), F32),
            pltpu.VMEM((N_HEADS // 2, t, 2 * t), F32),
        ],
        compiler_params=pltpu.CompilerParams(
            dimension_semantics=("arbitrary", "arbitrary"), vmem_limit_bytes=48 * 1024 * 1024),
        name="dilated",
    )(pb, pb, pb, pf, bias_tab)


def _pool_sgu_kernel(c_ref, cprev_ref, cz_ref, du_ref, dv_ref, dz_ref,
                     wpool_ref, pscale_ref, lng_ref, lnb_ref, wsp_ref, bsp_ref, yc_ref, yd_ref):
    ts = c_ref.shape[0]
    i = pl.program_id(1)
    lane = lax.broadcasted_iota(jnp.int32, (ts, GROUP_W), 1)

    cur = c_ref[...]
    prev = jnp.where(i > 0, cprev_ref[...], 0.0)
    ext = jnp.concatenate([prev, cur], axis=0)
    t_pos = i * ts + lax.broadcasted_iota(jnp.int32, (ts, 1), 0)
    pooled = None
    acc = ext
    width = 1
    for g, w in enumerate(POOL_SIZES):
        while width < w:
            acc = acc + pltpu.roll(acc, width, 0)
            width *= 2
        cnt = jnp.minimum(t_pos + 1, w).astype(F32)
        mean_w = acc[POOL_HALO:POOL_HALO + ts] * (1.0 / cnt)
        pooled = mean_w if pooled is None else jnp.where(lane >= g * HEAD_DIM, mean_w, pooled)
    pooled = pooled - cur
    mixed = _dot(pooled.astype(BF16), wpool_ref[...]) * pscale_ref[...]
    yc_ref[...] = (mixed * _silu(cz_ref[...])).astype(yc_ref.dtype)

    v = dv_ref[...]
    mu = jnp.mean(v, axis=-1, keepdims=True)
    var = jnp.mean(jnp.square(v - mu), axis=-1, keepdims=True)
    vn = ((v - mu) * lax.rsqrt(var + LN_EPS) * lng_ref[...] + lnb_ref[...]).astype(BF16)
    r = lax.broadcasted_iota(jnp.int32, (SG_CHUNK, SG_CHUNK), 0)
    cidx = lax.broadcasted_iota(jnp.int32, (SG_CHUNK, SG_CHUNK), 1)
    w_tril = [jnp.where(cidx <= r, wsp_ref[g], 0.0).astype(BF16) for g in range(N_HEADS)]
    lane_c = lax.broadcasted_iota(jnp.int32, (SG_CHUNK, GROUP_W), 1)
    for ci in range(ts // SG_CHUNK):
        rows = slice(ci * SG_CHUNK, (ci + 1) * SG_CHUNK)
        vc = vn[rows]
        zmix = _dot(w_tril[0], vc)
        for g in range(1, N_HEADS):
            zmix = jnp.where(lane_c >= g * HEAD_DIM, _dot(w_tril[g], vc), zmix)
        zfull = zmix + bsp_ref[...]
        yd_ref[rows, :] = (du_ref[rows, :] * zfull * _silu(dz_ref[rows, :])).astype(yd_ref.dtype)


def _pool_sgu(pf, wpool_bd, pool_scale, ln_g, ln_b, w_sp, bsp_exp, layer, nb, seq):
    ts = TS_POOL
    ns = seq // ts
    halo_blocks = ts // POOL_HALO

    def tile(col):
        return pl.BlockSpec((ts, GROUP_W), lambda b, i: (b * ns + i, col // 2))

    def per_layer(shape):
        nd = len(shape)
        return pl.BlockSpec((None,) + shape, lambda b, i: (layer,) + (0,) * nd)

    return pl.pallas_call(
        _pool_sgu_kernel,
        grid=(nb, ns),
        in_specs=[
            tile(PF_CIN),
            pl.BlockSpec((POOL_HALO, GROUP_W),
                         lambda b, i: (jnp.maximum((b * ns + i) * halo_blocks - 1, 0), PF_CIN // 2)),
            tile(PF_CZ), tile(PF_DU), tile(PF_DV), tile(PF_DZ),
            per_layer((GROUP_W, GROUP_W)),
            per_layer((1, GROUP_W)),
            per_layer((1, GROUP_W)),
            per_layer((1, GROUP_W)),
            per_layer((N_HEADS, SG_CHUNK, SG_CHUNK)),
            per_layer((SG_CHUNK, GROUP_W)),
        ],
        out_specs=[
            pl.BlockSpec((ts, GROUP_W), lambda b, i: (b * ns + i, 0)),
            pl.BlockSpec((ts, GROUP_W), lambda b, i: (b * ns + i, 0)),
        ],
        out_shape=[
            jax.ShapeDtypeStruct((nb * seq, GROUP_W), BF16),
            jax.ShapeDtypeStruct((nb * seq, GROUP_W), BF16),
        ],
        compiler_params=pltpu.CompilerParams(
            dimension_semantics=("arbitrary", "arbitrary"), vmem_limit_bytes=48 * 1024 * 1024),
        name="pool_sgu",
    )(pf, pf, pf, pf, pf, pf, wpool_bd, pool_scale, ln_g, ln_b, w_sp, bsp_exp)


def _out_proj_kernel(ya_ref, yb_ref, yc_ref, yd_ref, w_ref, x_ref, g_ref, o_ref):
    y = jnp.concatenate([ya_ref[...], yb_ref[...], yc_ref[...], yd_ref[...]], axis=1)
    out = _dot(y, w_ref[...])
    inv = lax.rsqrt(jnp.mean(out * out, axis=-1, keepdims=True) + RMS_EPS)
    o_ref[...] = x_ref[...] + out * inv * g_ref[...]


def _out_proj(ya, yb, yc, yd, w_out, x2, g_post, layer):
    m, d = x2.shape
    tm = TM_OUT
    ytile = pl.BlockSpec((tm, GROUP_W), lambda i: (i, 0))
    return pl.pallas_call(
        _out_proj_kernel,
        grid=(m // tm,),
        in_specs=[
            ytile, ytile, ytile, ytile,
            pl.BlockSpec((None, 4 * GROUP_W, d), lambda i: (layer, 0, 0)),
            pl.BlockSpec((tm, d), lambda i: (i, 0)),
            pl.BlockSpec((None, 1, d), lambda i: (layer, 0, 0)),
        ],
        out_specs=pl.BlockSpec((tm, d), lambda i: (i, 0)),
        out_shape=jax.ShapeDtypeStruct((m, d), F32),
        compiler_params=pltpu.CompilerParams(
            dimension_semantics=("arbitrary",), vmem_limit_bytes=48 * 1024 * 1024),
        name="out_proj",
    )(ya, yb, yc, yd, w_out, x2, g_post)


def _overlap_t(seq):
    n_cmp = (seq - CMP_LEN) // CMP_STRIDE + 1
    n_slc = seq // SEL_BLOCK
    cs = np.arange(n_cmp) * CMP_STRIDE
    ss = np.arange(n_slc) * SEL_BLOCK
    ov = (cs[None, :] < ss[:, None] + SEL_BLOCK) & (cs[None, :] + CMP_LEN > ss[:, None])
    out = np.zeros((LANES, LANES), np.float32)
    out[:n_slc, :n_cmp] = ov
    return out


def _causal_tri():
    a = np.arange(TK_SEL)[:, None]
    b = np.arange(TQ_NSA)[None, :]
    return np.stack([np.where(a - o * TQ_NSA <= b, 0.0, NEG) for o in range(TK_SEL // TQ_NSA)]).astype(np.float32)


def _window_band():
    a = np.arange(TW_WIN)[:, None]
    b = np.arange(TW_WIN)[None, :]
    return np.stack([np.where(a > b, 0.0, NEG), np.where(a <= b, 0.0, NEG)]).astype(np.float32)


def _dilated_log_multiplicity(seq):
    t = T_DIL
    d0 = np.arange(t)[:, None] - np.arange(t)[None, :]
    tabs = []
    for delta in range(seq // t):
        d = d0 + delta * t
        mult = np.zeros_like(d)
        for window, dil in DILATED_PAIRS:
            mult += (d >= 0) & (d % dil == 0) & (d // dil <= window // dil) & (d // dil <= seq // dil - 1)
        tabs.append(np.where(mult > 0, np.log2(np.maximum(mult, 1)), NEG).T)
    return np.stack(tabs).astype(np.float32)


def _regroup_w_in(w_in):
    gw = GROUP_W
    o = 0
    a_q = w_in[..., o:o + gw]; o += gw
    a_kv = w_in[..., o:o + 6 * HEAD_DIM]; o += 6 * HEAD_DIM
    a_g = w_in[..., o:o + N_GATES]; o += N_GATES
    a_z = w_in[..., o:o + gw]; o += gw
    b_qkv = w_in[..., o:o + 3 * gw]; o += 3 * gw
    rest = w_in[..., o:]
    pad = jnp.zeros(w_in.shape[:-1] + (LANES - N_GATES,), w_in.dtype)
    cols = [a_q * Q_SCALE, a_kv[..., 2 * HEAD_DIM:],
            b_qkv[..., :gw] * Q_SCALE, b_qkv[..., gw:],
            a_kv[..., :2 * HEAD_DIM], a_g, pad, a_z, rest]
    out = jnp.concatenate(cols, axis=-1)
    assert out.shape[-1] == P_COLS, out.shape
    return out.astype(BF16)


def kernel(x, g_pre, w_in, pe_cmp, w_cmp1, w_cmp2, w_pool, pool_scale, sg_ln_g, sg_ln_b, w_sp, b_sp, w_out, g_post):
    nb, seq, d = x.shape
    depth = w_in.shape[0]
    assert seq % TK_SEL == 0 and seq % T_DIL == 0 and seq % TS_POOL == 0 and (nb * seq) % TM_PROJ == 0
    assert TQ_NSA % TW_WIN == 0 and TK_SEL % TQ_NSA == 0 and WIN % TW_WIN == 0 and TW_WIN % LANES == 0
    assert seq // SEL_BLOCK <= LANES - HEAD_DIM and (seq - CMP_LEN) // CMP_STRIDE + 1 < LANES
    assert seq % CMP_STRIDE == 0 and CMP_LEN == 2 * CMP_STRIDE

    w_perm = _regroup_w_in(w_in)
    w_out_b = w_out.astype(BF16)
    w1_l = w_cmp1.astype(BF16).reshape(depth, 2, CMP_LEN, HEAD_DIM, CMP_HIDDEN)
    eye2 = jnp.eye(2, dtype=BF16)
    w1_bd = jnp.einsum('ljmdh,jk->lmjdkh', w1_l, eye2).reshape(depth, CMP_LEN, 2 * HEAD_DIM, 2 * CMP_HIDDEN)
    w2_b = w_cmp2.astype(BF16)
    pe_rows = jnp.concatenate([pe_cmp[:, 0], pe_cmp[:, 1]], axis=-1)
    eye = jnp.eye(N_HEADS, dtype=w_pool.dtype)
    wpool_bd = jnp.einsum('lgcd,gh->lgchd', w_pool, eye).reshape(depth, GROUP_W, GROUP_W).astype(BF16)
    bsp_exp = jnp.repeat(jnp.swapaxes(b_sp, 1, 2), HEAD_DIM, axis=2)
    g_pre3 = g_pre.reshape(depth, 1, d)
    g_post3 = g_post.reshape(depth, 1, d)
    pscale3 = pool_scale.reshape(depth, 1, GROUP_W)
    lng3 = sg_ln_g.reshape(depth, 1, GROUP_W)
    lnb3 = sg_ln_b.reshape(depth, 1, GROUP_W)
    ovl_t = jnp.asarray(_overlap_t(seq), BF16)
    tri = jnp.asarray(_causal_tri())
    band = jnp.asarray(_window_band())
    dil_bias = jnp.asarray(_dilated_log_multiplicity(seq))

    x2 = x.reshape(nb * seq, d)
    for layer in range(depth):
        pb, pf = _in_proj(x2, g_pre3, w_perm, layer)
        kcvc = _compress(pf, pe_rows, w1_bd, w2_b, layer, nb, seq)
        ya = _nsa(pb, pf, kcvc, ovl_t, tri, band, nb, seq)
        yb = _dilated(pb, pf, dil_bias, nb, seq)
        yc, yd = _pool_sgu(pf, wpool_bd, pscale3, lng3, lnb3, w_sp, bsp_exp, layer, nb, seq)
        x2 = _out_proj(ya, yb, yc, yd, w_out_b, x2, g_post3, layer)
    return x2.reshape(nb, seq, d)
```

```python
import functools

import numpy as np
import jax
import jax.numpy as jnp
from jax import lax
from jax.experimental import pallas as pl
from jax.experimental.pallas import tpu as pltpu

F32 = jnp.float32
BF16 = jnp.bfloat16

HEAD_DIM = 64
N_HEADS = 4
GROUP_W = N_HEADS * HEAD_DIM
CMP_LEN = 32
CMP_STRIDE = 16
CMP_HIDDEN = 256
SEL_BLOCK = 64
SEL_TOPK = 16
WIN = 512
FORCE = 1e4
DILATED_PAIRS = ((128, 1), (512, 4), (2048, 16))
POOL_SIZES = (2, 4, 8, 16)
SG_CHUNK = 128
RMS_EPS = 1e-6
LN_EPS = 1e-5
NEG = -1e30

LANES = 128
N_GATES = 3 * N_HEADS

PB_AQ, PB_SLC, PB_WIN, PB_BQ, PB_BK, PB_BV = 0, 2, 3, 4, 6, 8
PB_COLS = 10 * LANES
PF_CMP, PF_AG, PF_AZ, PF_BZ, PF_CIN, PF_CZ, PF_DU, PF_DV, PF_DZ = 0, 1, 2, 4, 6, 8, 10, 12, 14
PF_COLS = 16 * LANES

TM_PROJ = 256
TM_OUT = 512
TQ_NSA = 256
TK_SEL = 256
T_DIL = 256
TS_POOL = 512
POOL_HALO = 16
TW_WIN = 128
SEL_GROUPS = 1

Q_SCALE = float(np.log2(np.e)) * HEAD_DIM ** -0.5


def _sigmoid(x):
    return 1.0 / (1.0 + jnp.exp(-x))


def _silu(x):
    return x * _sigmoid(x)


def _dot_nt(a, b):
    return lax.dot_general(a, b, (((1,), (1,)), ((), ())), preferred_element_type=F32)


def _dot(a, b):
    return jnp.dot(a, b, preferred_element_type=F32)


def _in_proj_kernel(x_ref, g_ref, wa_ref, wg_ref, wb_ref, pb_ref, pf_ref):
    _project_in(x_ref[...], g_ref, wa_ref, wg_ref, wb_ref, pb_ref, pf_ref)


def _project_in(x, g_ref, wa_ref, wg_ref, wb_ref, pb_ref, pf_ref):
    inv = lax.rsqrt(jnp.mean(x * x, axis=-1, keepdims=True) + RMS_EPS)
    h = (x * inv * g_ref[...]).astype(BF16)
    cw = 2 * LANES

    def put(dst, c, w):
        dst[:, c * cw:(c + 1) * cw] = _dot(h, w).astype(dst.dtype)

    put(pb_ref, PB_AQ // 2, wa_ref[:, 0:cw])
    put(pb_ref, PB_SLC // 2, wa_ref[:, cw + LANES:2 * cw + LANES])
    for c in range(3):
        put(pb_ref, PB_BQ // 2 + c, wb_ref[:, (1 + c) * cw:(2 + c) * cw])
    put(pf_ref, PF_CMP // 2, jnp.concatenate([wa_ref[:, cw:cw + LANES], wg_ref[...]], axis=1))
    put(pf_ref, PF_AZ // 2, wb_ref[:, 0:cw])
    for c in range(6):
        put(pf_ref, PF_BZ // 2 + c, wb_ref[:, (4 + c) * cw:(5 + c) * cw])


def _in_proj(x2, g_pre, w_a, w_g, w_b, layer):
    m, d = x2.shape
    return pl.pallas_call(
        _in_proj_kernel,
        grid=(m // TM_PROJ,),
        in_specs=[
            pl.BlockSpec((TM_PROJ, d), lambda i: (i, 0)),
            pl.BlockSpec((None, 1, d), lambda i: (layer, 0, 0)),
            pl.BlockSpec((None, d, w_a.shape[2]), lambda i: (layer, 0, 0)),
            pl.BlockSpec((None, d, w_g.shape[2]), lambda i: (layer, 0, 0)),
            pl.BlockSpec((None, d, w_b.shape[2]), lambda i: (layer, 0, 0)),
        ],
        out_specs=[
            pl.BlockSpec((TM_PROJ, PB_COLS), lambda i: (i, 0)),
            pl.BlockSpec((TM_PROJ, PF_COLS), lambda i: (i, 0)),
        ],
        out_shape=[
            jax.ShapeDtypeStruct((m, PB_COLS), BF16),
            jax.ShapeDtypeStruct((m, PF_COLS), F32),
        ],
        compiler_params=pltpu.CompilerParams(
            dimension_semantics=("arbitrary",), vmem_limit_bytes=48 * 1024 * 1024),
        name="in_proj",
    )(x2, g_pre, w_a, w_g, w_b)


def _gelu_tanh(x):
    return 0.5 * x * (1.0 + jnp.tanh(np.sqrt(2.0 / np.pi).astype(np.float32) * (x + 0.044715 * (x * x * x))))


def _compress_kernel(x_ref, pe_ref, w1_ref, w2_ref, o_ref):
    n_grp = x_ref.shape[0] // CMP_STRIDE
    ng = o_ref.shape[1]
    lo = [None, None]
    hi = [None, None]
    for l0 in range(CMP_STRIDE):
        rows = x_ref[pl.ds(l0, n_grp, stride=CMP_STRIDE), :]
        r_lo = (rows + pe_ref[l0:l0 + 1, :]).astype(BF16)
        r_hi = (rows + pe_ref[CMP_STRIDE + l0:CMP_STRIDE + l0 + 1, :]).astype(BF16)
        for j in range(2):
            cols = slice(j * HEAD_DIM, (j + 1) * HEAD_DIM)
            d_lo = _dot(r_lo[:, cols], w1_ref[j, l0])
            d_hi = _dot(r_hi[:, cols], w1_ref[j, CMP_STRIDE + l0])
            lo[j] = d_lo if lo[j] is None else lo[j] + d_lo
            hi[j] = d_hi if hi[j] is None else hi[j] + d_hi
    row = lax.broadcasted_iota(jnp.int32, (n_grp, HEAD_DIM), 0) % ng
    for j in range(2):
        hid = _gelu_tanh(lo[j] + pltpu.roll(hi[j], n_grp - 1, 0)).astype(BF16)
        comp = jnp.where(row < ng - 1, _dot(hid, w2_ref[j]), 0.0)
        o_ref[:, :, j * HEAD_DIM:(j + 1) * HEAD_DIM] = comp.reshape(o_ref.shape[0], ng, HEAD_DIM).astype(BF16)


def _compress(pf, pe_rows, w1, w2, layer, nb, seq):
    ng = seq // CMP_STRIDE
    return pl.pallas_call(
        _compress_kernel,
        grid=(1,),
        in_specs=[
            pl.BlockSpec((nb * seq, LANES), lambda i: (0, PF_CMP)),
            pl.BlockSpec((None, CMP_LEN, LANES), lambda i: (layer, 0, 0)),
            pl.BlockSpec((None, 2, CMP_LEN, HEAD_DIM, CMP_HIDDEN), lambda i: (layer, 0, 0, 0, 0)),
            pl.BlockSpec((None, 2, CMP_HIDDEN, HEAD_DIM), lambda i: (layer, 0, 0, 0)),
        ],
        out_specs=pl.BlockSpec((nb, ng, LANES), lambda i: (0, 0, 0)),
        out_shape=jax.ShapeDtypeStruct((nb, ng, LANES), BF16),
        compiler_params=pltpu.CompilerParams(
            dimension_semantics=("arbitrary",), vmem_limit_bytes=48 * 1024 * 1024),
        name="compress",
    )(pf, pe_rows, w1, w2)


def _nsa_kernel(q_ref, slc_ref, win_ref, kc_ref, gate_ref, z_ref, ovl_ref, tri_ref, band_ref, o_ref,
                ksel_ref, vtsel_ref, kwin_ref, vtwin_ref, kct_ref, acc_ref, s_ref):
    tq = q_ref.shape[0]
    seq = slc_ref.shape[0]
    tk = s_ref.shape[1]
    tw = band_ref.shape[1]
    nh = N_HEADS
    n_slc = seq // SEL_BLOCK
    qi = pl.program_id(1)
    q0 = qi * tq

    @pl.when(qi == 0)
    def _():
        lane_k = lax.broadcasted_iota(jnp.int32, (tk, LANES), 1)
        for j in range(seq // tk):
            rows = slice(j * tk, (j + 1) * tk)
            blk = slc_ref[rows, :]
            key_blk = (j * tk + lax.broadcasted_iota(jnp.int32, (tk, LANES), 0)) // SEL_BLOCK
            onehot = jnp.where(lane_k - HEAD_DIM == key_blk, 1.0, 0.0).astype(BF16)
            ksel_ref[rows, :] = jnp.where(lane_k < HEAD_DIM, blk, onehot)
            vtsel_ref[j] = blk.astype(F32).T.astype(BF16)
        lane_w = lax.broadcasted_iota(jnp.int32, (WIN, LANES), 1)
        kwin_ref[0:WIN, :] = jnp.where(lane_w == HEAD_DIM, 1.0, 0.0).astype(BF16)
        for j in range(WIN // tw):
            vtwin_ref[j] = jnp.zeros((LANES, tw), BF16)
        lane_t = lax.broadcasted_iota(jnp.int32, (tw, LANES), 1)
        for j in range(seq // tw):
            blk = win_ref[j * tw:(j + 1) * tw, :]
            kwin_ref[WIN + j * tw:WIN + (j + 1) * tw, :] = jnp.where(lane_t < HEAD_DIM, blk, jnp.zeros_like(blk))
            vtwin_ref[WIN // tw + j] = blk.astype(F32).T.astype(BF16)
        kct_ref[...] = kc_ref[...].astype(F32).T.astype(BF16)

    lane = lax.broadcasted_iota(jnp.int32, (tq, LANES), 1)
    low = lane < HEAD_DIM
    q = q_ref[...].astype(F32)
    q_heads = []
    for h in range(nh):
        blk = q[:, LANES * (h // 2):LANES * (h // 2 + 1)]
        if h % 2:
            blk = pltpu.roll(blk, HEAD_DIM, 1)
        q_heads.append(jnp.where(low, blk, 0.0))
    qs_plain = jnp.concatenate(q_heads, axis=0).astype(BF16)

    def tile4(x):
        return jnp.concatenate([x] * nh, axis=1)


    span = WIN + tw
    pad_neg = jnp.where(lane == HEAD_DIM, NEG, 0.0)
    n_sub = tq // tw
    s_win = []
    for u in range(n_sub):
        qs_win = jnp.concatenate([(qh + pad_neg)[u * tw:(u + 1) * tw] for qh in q_heads], axis=0).astype(BF16)
        start = pl.multiple_of(q0 + u * tw, tw)
        s_win.append(_dot_nt(kwin_ref[pl.ds(start, span), :], qs_win))
    kc = kc_ref[...]
    s_c = _dot_nt(kc, qs_plain)

    def window_softmax(u):
        edge = [jnp.concatenate([band_ref[e]] * nh, axis=1) for e in range(2)]
        s_w = s_win[u]
        s_w = jnp.concatenate([s_w[0:tw] + edge[0], s_w[tw:span - tw], s_w[span - tw:] + edge[1]], axis=0)
        e_w = jnp.exp2(s_w - jnp.max(s_w, axis=0, keepdims=True))
        l_w = jnp.sum(e_w, axis=0, keepdims=True)
        t0 = qi * n_sub + u
        vt_w = jnp.concatenate([vtwin_ref[t0 + d] for d in range(span // tw)], axis=1)
        return _dot(vt_w, e_w.astype(BF16)) / l_w

    n_cmp = (seq - CMP_LEN) // CMP_STRIDE + 1
    n_idx = lax.broadcasted_iota(jnp.int32, (LANES, tq), 0)
    t_idx = q0 + lax.broadcasted_iota(jnp.int32, (LANES, tq), 1)
    valid_c = tile4((n_idx * CMP_STRIDE + (CMP_LEN - 1) <= t_idx) & (n_idx < n_cmp))
    s_c = jnp.where(valid_c, s_c, NEG)
    e_c = jnp.exp2(s_c - jnp.max(s_c, axis=0, keepdims=True))
    p_c = jnp.where(valid_c, e_c / jnp.sum(e_c, axis=0, keepdims=True), 0.0)
    o_cmp = _dot(kct_ref[...], p_c.astype(BF16))

    psum = p_c[:, 0:tq]
    for h in range(1, nh):
        psum = psum + p_c[:, h * tq:(h + 1) * tq]
    p_hi = psum.astype(BF16)
    r1 = psum - p_hi.astype(F32)
    p_mid = r1.astype(BF16)
    p_lo = (r1 - p_mid.astype(F32)).astype(BF16)
    ovl = ovl_ref[...]
    imp = _dot(ovl, p_hi) + _dot(ovl, p_mid) + _dot(ovl, p_lo)

    o_win = [window_softmax(u) for u in range(n_sub - 1)]

    v = imp[0:n_slc]
    jj = lax.broadcasted_iota(jnp.int32, (n_slc, tq), 0)
    cur = (q0 + lax.broadcasted_iota(jnp.int32, (n_slc, tq), 1)) // SEL_BLOCK
    forced = (jj == 0) | (jj == cur) | (jj == cur - 1)
    v = jnp.where(forced, FORCE, jnp.where(jj <= cur, v, -FORCE))
    sub = 8
    ranks = []
    row8 = lax.broadcasted_iota(jnp.int32, (sub, tq), 0)
    for r0 in range(0, n_slc, sub):
        vr = v[r0:r0 + sub]
        cnt = jnp.zeros((sub, tq), F32)
        for j2 in range(n_slc):
            vj = v[j2:j2 + 1, :]
            if j2 < r0:
                ahead = vj >= vr
            elif j2 >= r0 + sub:
                ahead = vj > vr
            else:
                ahead = (vj > vr) | ((vj == vr) & (row8 > j2 - r0))
            cnt = cnt + jnp.where(ahead, 1.0, 0.0)
        ranks.append(cnt)
    rank = jnp.concatenate(ranks, axis=0)
    sel_neg = jnp.where((rank < min(SEL_TOPK, n_slc)) & (jj <= cur), 0.0, NEG)
    sel_neg = jnp.concatenate([jnp.zeros((HEAD_DIM, tq), F32), sel_neg,
                               jnp.zeros((LANES - HEAD_DIM - n_slc, tq), F32)], axis=0).T
    qs_sel = jnp.concatenate([qh + sel_neg for qh in q_heads], axis=0).astype(BF16)

    o_win.append(window_softmax(n_sub - 1))

    n_grp = s_ref.shape[0]
    gw = nh * tq // n_grp

    def scores(c, g):
        keys = ksel_ref[pl.ds(pl.multiple_of(c * tk, tk), tk), :]
        return _dot_nt(keys, qs_sel[g * gw:(g + 1) * gw])

    def absorb(c, g, s, m, l):
        m_new = jnp.maximum(m, jnp.max(s, axis=0, keepdims=True))
        alpha = jnp.exp2(m - m_new)
        p = jnp.exp2(s - m_new)
        l = alpha * l + jnp.sum(p, axis=0, keepdims=True)
        acc_ref[g] = alpha * acc_ref[g] + _dot(vtsel_ref[c], p.astype(BF16))
        return m_new, l

    n_chunks = (q0 + tq + tk - 1) // tk
    acc_ref[...] = jnp.zeros_like(acc_ref)
    for g in range(n_grp):
        s_ref[g] = scores(0, g)

    def sel_step(c, carry):
        s = [s_ref[g] for g in range(n_grp)]
        s_next = [scores(c + 1, g) for g in range(n_grp)]
        carry = tuple(absorb(c, g, s[g], *carry[g]) for g in range(n_grp))
        for g in range(n_grp):
            s_ref[g] = s_next[g]
        return carry

    stats = tuple((jnp.full((1, gw), NEG, F32), jnp.zeros((1, gw), F32)) for _ in range(n_grp))
    stats = lax.fori_loop(0, n_chunks - 1, sel_step, stats)
    last = n_chunks - 1
    tri = tri_ref[(q0 - last * tk) // tq]
    tri_g = jnp.concatenate([tri] * (gw // tq), axis=1)
    o_slc = []
    for g in range(n_grp):
        _, l_g = absorb(last, g, s_ref[g] + tri_g, *stats[g])
        o_slc.append(acc_ref[g] / l_g)
    o_slc = jnp.concatenate(o_slc, axis=1)

    gate_t = _sigmoid(gate_ref[...]).T
    combs = []
    for h in range(nh):
        cols = slice(h * tq, (h + 1) * tq)
        o_win_h = jnp.concatenate([ow[HEAD_DIM:, h * tw:(h + 1) * tw] for ow in o_win], axis=1)
        combs.append(gate_t[0 * nh + h:0 * nh + h + 1, :] * o_cmp[HEAD_DIM:, cols]
                     + gate_t[1 * nh + h:1 * nh + h + 1, :] * o_slc[HEAD_DIM:, cols]
                     + gate_t[2 * nh + h:2 * nh + h + 1, :] * o_win_h)
    for c in range(nh // 2):
        blk = jnp.concatenate([combs[2 * c], combs[2 * c + 1]], axis=0).T
        zc = z_ref[:, c * LANES:(c + 1) * LANES]
        o_ref[:, c * LANES:(c + 1) * LANES] = (blk * _silu(zc)).astype(o_ref.dtype)


def _nsa(pb, pf, kcvc, ovl, tri, band, nb, seq):
    tq = TQ_NSA
    tk = TK_SEL
    nq = seq // tq
    return pl.pallas_call(
        _nsa_kernel,
        grid=(nb, nq),
        in_specs=[
            pl.BlockSpec((tq, 2 * LANES), lambda b, i: (b * nq + i, PB_AQ // 2)),
            pl.BlockSpec((seq, LANES), lambda b, i: (b, PB_SLC)),
            pl.BlockSpec((seq, LANES), lambda b, i: (b, PB_WIN)),
            pl.BlockSpec((None, LANES, LANES), lambda b, i: (b, 0, 0)),
            pl.BlockSpec((tq, LANES), lambda b, i: (b * nq + i, PF_AG)),
            pl.BlockSpec((tq, 2 * LANES), lambda b, i: (b * nq + i, PF_AZ // 2)),
            pl.BlockSpec((LANES, LANES), lambda b, i: (0, 0)),
            pl.BlockSpec(tri.shape, lambda b, i: (0, 0, 0)),
            pl.BlockSpec(band.shape, lambda b, i: (0, 0, 0)),
        ],
        out_specs=pl.BlockSpec((tq, GROUP_W), lambda b, i: (b * nq + i, 0)),
        out_shape=jax.ShapeDtypeStruct((nb * seq, GROUP_W), BF16),
        scratch_shapes=[
            pltpu.VMEM((seq, LANES), BF16),
            pltpu.VMEM((seq // tk, LANES, tk), BF16),
            pltpu.VMEM((WIN + seq, LANES), BF16),
            pltpu.VMEM(((WIN + seq) // TW_WIN, LANES, TW_WIN), BF16),
            pltpu.VMEM((LANES, LANES), BF16),
            pltpu.VMEM((SEL_GROUPS, LANES, N_HEADS * tq // SEL_GROUPS), F32),
            pltpu.VMEM((SEL_GROUPS, tk, N_HEADS * tq // SEL_GROUPS), F32),
        ],
        compiler_params=pltpu.CompilerParams(
            dimension_semantics=("arbitrary", "arbitrary"), vmem_limit_bytes=48 * 1024 * 1024),
        name="nsa",
    )(pb, pb, pb, kcvc, pf, pf, ovl, tri, band)


def _dilated_kernel(q_ref, k_ref, v_ref, z_ref, bias_ref, o_ref, vt_ref, acc_ref, s_ref):
    t = q_ref.shape[0]
    seq = k_ref.shape[0]
    n_pairs = N_HEADS // 2
    qi = pl.program_id(1)

    @pl.when(qi == 0)
    def _():
        for j in range(seq // t):
            vt_ref[j] = v_ref[j * t:(j + 1) * t, :].astype(F32).T.astype(BF16)

    lane = lax.broadcasted_iota(jnp.int32, (t, LANES), 1)
    low = lane < HEAD_DIM
    q = q_ref[...].astype(F32)
    qs = []
    for c in range(n_pairs):
        blk = q[:, c * LANES:(c + 1) * LANES]
        qs.append(jnp.concatenate([jnp.where(low, blk, 0.0), jnp.where(low, 0.0, blk)], axis=0).astype(BF16))
    acc_ref[...] = jnp.zeros_like(acc_ref)

    def scores(j, c):
        k0 = pl.multiple_of(j * t, t)
        return _dot_nt(k_ref[pl.ds(k0, t), c * LANES:(c + 1) * LANES], qs[c])

    for c in range(n_pairs):
        s_ref[c] = scores(0, c)

    def step(j, carry):
        j_next = jnp.minimum(j + 1, qi)
        bias = bias_ref[qi - j]
        bias2 = jnp.concatenate([bias, bias], axis=1)
        out = []
        for c in range(n_pairs):
            m, l = carry[c]
            s = s_ref[c] + bias2
            s_next = scores(j_next, c)
            m_new = jnp.maximum(m, jnp.max(s, axis=0, keepdims=True))
            alpha = jnp.exp2(m - m_new)
            p = jnp.exp2(s - m_new)
            l = alpha * l + jnp.sum(p, axis=0, keepdims=True)
            pv = _dot(vt_ref[j, c * LANES:(c + 1) * LANES, :], p.astype(BF16))
            acc_ref[c] = alpha * acc_ref[c] + pv
            s_ref[c] = s_next
            out.append((m_new, l))
        return tuple(out)

    init = tuple((jnp.full((1, 2 * t), NEG, F32), jnp.zeros((1, 2 * t), F32)) for _ in range(n_pairs))
    stats = lax.fori_loop(0, qi + 1, step, init)
    for c in range(n_pairs):
        o_t = acc_ref[c] / stats[c][1]
        o_pair = jnp.concatenate([o_t[:HEAD_DIM, :t], o_t[HEAD_DIM:, t:]], axis=0).T
        zc = z_ref[:, c * LANES:(c + 1) * LANES]
        o_ref[:, c * LANES:(c + 1) * LANES] = (o_pair * _silu(zc)).astype(o_ref.dtype)


def _dilated(pb, pf, bias_tab, nb, seq):
    t = T_DIL
    nq = seq // t
    return pl.pallas_call(
        _dilated_kernel,
        grid=(nb, nq),
        in_specs=[
            pl.BlockSpec((t, GROUP_W), lambda b, i: (b * nq + i, PB_BQ // 2)),
            pl.BlockSpec((seq, GROUP_W), lambda b, i: (b, PB_BK // 2)),
            pl.BlockSpec((seq, GROUP_W), lambda b, i: (b, PB_BV // 2)),
            pl.BlockSpec((t, GROUP_W), lambda b, i: (b * nq + i, PF_BZ // 2)),
            pl.BlockSpec((nq, t, t), lambda b, i: (0, 0, 0)),
        ],
        out_specs=pl.BlockSpec((t, GROUP_W), lambda b, i: (b * nq + i, 0)),
        out_shape=jax.ShapeDtypeStruct((nb * seq, GROUP_W), BF16),
        scratch_shapes=[
            pltpu.VMEM((nq, GROUP_W, t), BF16),
            pltpu.VMEM((N_HEADS // 2, LANES, 2 * t), F32),
            pltpu.VMEM((N_HEADS // 2, t, 2 * t), F32),
        ],
        compiler_params=pltpu.CompilerParams(
            dimension_semantics=("arbitrary", "arbitrary"), vmem_limit_bytes=48 * 1024 * 1024),
        name="dilated",
    )(pb, pb, pb, pf, bias_tab)


def _pool_sgu_kernel(c_ref, cprev_ref, cz_ref, du_ref, dv_ref, dz_ref,
                     wpool_ref, pscale_ref, lng_ref, lnb_ref, wsp_ref, bsp_ref, yc_ref, yd_ref):
    ts = c_ref.shape[0]
    i = pl.program_id(1)
    lane = lax.broadcasted_iota(jnp.int32, (ts, GROUP_W), 1)

    cur = c_ref[...]
    prev = jnp.where(i > 0, cprev_ref[...], 0.0)
    ext = jnp.concatenate([prev, cur], axis=0)
    t_pos = i * ts + lax.broadcasted_iota(jnp.int32, (ts, 1), 0)
    pooled = None
    acc = ext
    width = 1
    for g, w in enumerate(POOL_SIZES):
        while width < w:
            acc = acc + pltpu.roll(acc, width, 0)
            width *= 2
        cnt = jnp.minimum(t_pos + 1, w).astype(F32)
        mean_w = acc[POOL_HALO:POOL_HALO + ts] * (1.0 / cnt)
        pooled = mean_w if pooled is None else jnp.where(lane >= g * HEAD_DIM, mean_w, pooled)
    pooled = pooled - cur
    mixed = _dot(pooled.astype(BF16), wpool_ref[...]) * pscale_ref[...]
    yc_ref[...] = (mixed * _silu(cz_ref[...])).astype(yc_ref.dtype)

    v = dv_ref[...]
    mu = jnp.mean(v, axis=-1, keepdims=True)
    var = jnp.mean(jnp.square(v - mu), axis=-1, keepdims=True)
    vn = ((v - mu) * lax.rsqrt(var + LN_EPS) * lng_ref[...] + lnb_ref[...]).astype(BF16)
    r = lax.broadcasted_iota(jnp.int32, (SG_CHUNK, SG_CHUNK), 0)
    cidx = lax.broadcasted_iota(jnp.int32, (SG_CHUNK, SG_CHUNK), 1)
    w_tril = [jnp.where(cidx <= r, wsp_ref[g], 0.0).astype(BF16) for g in range(N_HEADS)]
    lane_c = lax.broadcasted_iota(jnp.int32, (SG_CHUNK, GROUP_W), 1)
    for ci in range(ts // SG_CHUNK):
        rows = slice(ci * SG_CHUNK, (ci + 1) * SG_CHUNK)
        vc = vn[rows]
        zmix = _dot(w_tril[0], vc)
        for g in range(1, N_HEADS):
            zmix = jnp.where(lane_c >= g * HEAD_DIM, _dot(w_tril[g], vc), zmix)
        zfull = zmix + bsp_ref[...]
        yd_ref[rows, :] = (du_ref[rows, :] * zfull * _silu(dz_ref[rows, :])).astype(yd_ref.dtype)


def _pool_sgu(pf, wpool_bd, pool_scale, ln_g, ln_b, w_sp, bsp_exp, layer, nb, seq):
    ts = TS_POOL
    ns = seq // ts
    halo_blocks = ts // POOL_HALO

    def tile(col):
        return pl.BlockSpec((ts, GROUP_W), lambda b, i: (b * ns + i, col // 2))

    def per_layer(shape):
        nd = len(shape)
        return pl.BlockSpec((None,) + shape, lambda b, i: (layer,) + (0,) * nd)

    return pl.pallas_call(
        _pool_sgu_kernel,
        grid=(nb, ns),
        in_specs=[
            tile(PF_CIN),
            pl.BlockSpec((POOL_HALO, GROUP_W),
                         lambda b, i: (jnp.maximum((b * ns + i) * halo_blocks - 1, 0), PF_CIN // 2)),
            tile(PF_CZ), tile(PF_DU), tile(PF_DV), tile(PF_DZ),
            per_layer((GROUP_W, GROUP_W)),
            per_layer((1, GROUP_W)),
            per_layer((1, GROUP_W)),
            per_layer((1, GROUP_W)),
            per_layer((N_HEADS, SG_CHUNK, SG_CHUNK)),
            per_layer((SG_CHUNK, GROUP_W)),
        ],
        out_specs=[
            pl.BlockSpec((ts, GROUP_W), lambda b, i: (b * ns + i, 0)),
            pl.BlockSpec((ts, GROUP_W), lambda b, i: (b * ns + i, 0)),
        ],
        out_shape=[
            jax.ShapeDtypeStruct((nb * seq, GROUP_W), BF16),
            jax.ShapeDtypeStruct((nb * seq, GROUP_W), BF16),
        ],
        compiler_params=pltpu.CompilerParams(
            dimension_semantics=("arbitrary", "arbitrary"), vmem_limit_bytes=48 * 1024 * 1024),
        name="pool_sgu",
    )(pf, pf, pf, pf, pf, pf, wpool_bd, pool_scale, ln_g, ln_b, w_sp, bsp_exp)


def _project_out(ya_ref, yb_ref, yc_ref, yd_ref, w_ref, x_ref, g_ref):
    y = jnp.concatenate([ya_ref[...], yb_ref[...], yc_ref[...], yd_ref[...]], axis=1)
    out = _dot(y, w_ref[...])
    inv = lax.rsqrt(jnp.mean(out * out, axis=-1, keepdims=True) + RMS_EPS)
    return x_ref[...] + out * inv * g_ref[...]


def _out_proj_kernel(ya_ref, yb_ref, yc_ref, yd_ref, w_ref, x_ref, g_ref, o_ref):
    o_ref[...] = _project_out(ya_ref, yb_ref, yc_ref, yd_ref, w_ref, x_ref, g_ref)


def _out_in_proj_kernel(ya_ref, yb_ref, yc_ref, yd_ref, wo_ref, x_ref, gpost_ref,
                        gpre_ref, wa_ref, wg_ref, wb_ref, o_ref, pb_ref, pf_ref):
    half = o_ref.shape[0] // 2
    x_new = []
    for r in range(2):
        rows = pl.ds(r * half, half)
        x_new.append(_project_out(ya_ref.at[rows], yb_ref.at[rows], yc_ref.at[rows], yd_ref.at[rows],
                                  wo_ref, x_ref.at[rows], gpost_ref))
        o_ref[rows, :] = x_new[r]
    for r in range(2):
        rows = pl.ds(r * half, half)
        _project_in(x_new[r], gpre_ref, wa_ref, wg_ref, wb_ref, pb_ref.at[rows], pf_ref.at[rows])


def _out_in_proj(ya, yb, yc, yd, w_out, x2, g_post, g_pre, w_a, w_g, w_b, layer):
    m, d = x2.shape
    tm = TM_OUT
    ytile = pl.BlockSpec((tm, GROUP_W), lambda i: (i, 0))
    nxt = layer + 1
    return pl.pallas_call(
        _out_in_proj_kernel,
        grid=(m // tm,),
        in_specs=[
            ytile, ytile, ytile, ytile,
            pl.BlockSpec((None, 4 * GROUP_W, d), lambda i: (layer, 0, 0)),
            pl.BlockSpec((tm, d), lambda i: (i, 0)),
            pl.BlockSpec((None, 1, d), lambda i: (layer, 0, 0)),
            pl.BlockSpec((None, 1, d), lambda i: (nxt, 0, 0)),
            pl.BlockSpec((None, d, w_a.shape[2]), lambda i: (nxt, 0, 0)),
            pl.BlockSpec((None, d, w_g.shape[2]), lambda i: (nxt, 0, 0)),
            pl.BlockSpec((None, d, w_b.shape[2]), lambda i: (nxt, 0, 0)),
        ],
        out_specs=[
            pl.BlockSpec((tm, d), lambda i: (i, 0)),
            pl.BlockSpec((tm, PB_COLS), lambda i: (i, 0)),
            pl.BlockSpec((tm, PF_COLS), lambda i: (i, 0)),
        ],
        out_shape=[
            jax.ShapeDtypeStruct((m, d), F32),
            jax.ShapeDtypeStruct((m, PB_COLS), BF16),
            jax.ShapeDtypeStruct((m, PF_COLS), F32),
        ],
        compiler_params=pltpu.CompilerParams(
            dimension_semantics=("arbitrary",), vmem_limit_bytes=56 * 1024 * 1024),
        name="out_in_proj",
    )(ya, yb, yc, yd, w_out, x2, g_post, g_pre, w_a, w_g, w_b)


def _out_proj(ya, yb, yc, yd, w_out, x2, g_post, layer):
    m, d = x2.shape
    tm = TM_OUT
    ytile = pl.BlockSpec((tm, GROUP_W), lambda i: (i, 0))
    return pl.pallas_call(
        _out_proj_kernel,
        grid=(m // tm,),
        in_specs=[
            ytile, ytile, ytile, ytile,
            pl.BlockSpec((None, 4 * GROUP_W, d), lambda i: (layer, 0, 0)),
            pl.BlockSpec((tm, d), lambda i: (i, 0)),
            pl.BlockSpec((None, 1, d), lambda i: (layer, 0, 0)),
        ],
        out_specs=pl.BlockSpec((tm, d), lambda i: (i, 0)),
        out_shape=jax.ShapeDtypeStruct((m, d), F32),
        compiler_params=pltpu.CompilerParams(
            dimension_semantics=("arbitrary",), vmem_limit_bytes=48 * 1024 * 1024),
        name="out_proj",
    )(ya, yb, yc, yd, w_out, x2, g_post)


def _overlap_t(seq):
    n_cmp = (seq - CMP_LEN) // CMP_STRIDE + 1
    n_slc = seq // SEL_BLOCK
    cs = np.arange(n_cmp) * CMP_STRIDE
    ss = np.arange(n_slc) * SEL_BLOCK
    ov = (cs[None, :] < ss[:, None] + SEL_BLOCK) & (cs[None, :] + CMP_LEN > ss[:, None])
    out = np.zeros((LANES, LANES), np.float32)
    out[:n_slc, :n_cmp] = ov
    return out


def _causal_tri():
    a = np.arange(TK_SEL)[:, None]
    b = np.arange(TQ_NSA)[None, :]
    return np.stack([np.where(a - o * TQ_NSA <= b, 0.0, NEG) for o in range(TK_SEL // TQ_NSA)]).astype(np.float32)


def _window_band():
    a = np.arange(TW_WIN)[:, None]
    b = np.arange(TW_WIN)[None, :]
    return np.stack([np.where(a > b, 0.0, NEG), np.where(a <= b, 0.0, NEG)]).astype(np.float32)


def _dilated_log_multiplicity(seq):
    t = T_DIL
    d0 = np.arange(t)[:, None] - np.arange(t)[None, :]
    tabs = []
    for delta in range(seq // t):
        d = d0 + delta * t
        mult = np.zeros_like(d)
        for window, dil in DILATED_PAIRS:
            mult += (d >= 0) & (d % dil == 0) & (d // dil <= window // dil) & (d // dil <= seq // dil - 1)
        tabs.append(np.where(mult > 0, np.log2(np.maximum(mult, 1)), NEG).T)
    return np.stack(tabs).astype(np.float32)


def _split_w_in(w_in):
    gw = GROUP_W
    n_a = gw + 6 * HEAD_DIM
    n_b = 10 * gw
    assert w_in.shape[-1] == n_a + N_GATES + n_b, w_in.shape
    scale_a = np.ones((n_a,), np.float32)
    scale_a[:gw] = Q_SCALE
    scale_b = np.ones((n_b,), np.float32)
    scale_b[gw:2 * gw] = Q_SCALE
    w_a = (w_in[..., :n_a] * scale_a).astype(BF16)
    w_g = jnp.pad(w_in[..., n_a:n_a + N_GATES], ((0, 0), (0, 0), (0, LANES - N_GATES))).astype(BF16)
    w_b = (w_in[..., n_a + N_GATES:] * scale_b).astype(BF16)
    return w_a, w_g, w_b


def kernel(x, g_pre, w_in, pe_cmp, w_cmp1, w_cmp2, w_pool, pool_scale, sg_ln_g, sg_ln_b, w_sp, b_sp, w_out, g_post):
    nb, seq, d = x.shape
    depth = w_in.shape[0]
    assert seq % TK_SEL == 0 and seq % T_DIL == 0 and seq % TS_POOL == 0 and (nb * seq) % TM_PROJ == 0
    assert TQ_NSA % TW_WIN == 0 and TK_SEL % TQ_NSA == 0 and WIN % TW_WIN == 0 and TW_WIN % LANES == 0
    assert seq // SEL_BLOCK <= LANES - HEAD_DIM and (seq - CMP_LEN) // CMP_STRIDE + 1 < LANES
    assert seq % CMP_STRIDE == 0 and CMP_LEN == 2 * CMP_STRIDE

    w_a, w_g, w_b = _split_w_in(w_in)
    w_out_b = w_out.astype(BF16)
    w1_l = w_cmp1.astype(BF16).reshape(depth, 2, CMP_LEN, HEAD_DIM, CMP_HIDDEN)
    w2_b = w_cmp2.astype(BF16)
    pe_rows = jnp.concatenate([pe_cmp[:, 0], pe_cmp[:, 1]], axis=-1)
    eye = jnp.eye(N_HEADS, dtype=w_pool.dtype)
    wpool_bd = jnp.einsum('lgcd,gh->lgchd', w_pool, eye).reshape(depth, GROUP_W, GROUP_W).astype(BF16)
    bsp_exp = jnp.repeat(jnp.swapaxes(b_sp, 1, 2), HEAD_DIM, axis=2)
    g_pre3 = g_pre.reshape(depth, 1, d)
    g_post3 = g_post.reshape(depth, 1, d)
    pscale3 = pool_scale.reshape(depth, 1, GROUP_W)
    lng3 = sg_ln_g.reshape(depth, 1, GROUP_W)
    lnb3 = sg_ln_b.reshape(depth, 1, GROUP_W)
    ovl_t = jnp.asarray(_overlap_t(seq), BF16)
    tri = jnp.asarray(_causal_tri())
    band = jnp.asarray(_window_band())
    dil_bias = jnp.asarray(_dilated_log_multiplicity(seq))

    x2 = x.reshape(nb * seq, d)
    pb, pf = _in_proj(x2, g_pre3, w_a, w_g, w_b, 0)
    for layer in range(depth):
        kcvc = _compress(pf, pe_rows, w1_l, w2_b, layer, nb, seq)
        ya = _nsa(pb, pf, kcvc, ovl_t, tri, band, nb, seq)
        yb = _dilated(pb, pf, dil_bias, nb, seq)
        yc, yd = _pool_sgu(pf, wpool_bd, pscale3, lng3, lnb3, w_sp, bsp_exp, layer, nb, seq)
        if layer + 1 < depth:
            x2, pb, pf = _out_in_proj(ya, yb, yc, yd, w_out_b, x2, g_post3, g_pre3, w_a, w_g, w_b, layer)
        else:
            x2 = _out_proj(ya, yb, yc, yd, w_out_b, x2, g_post3, layer)
    return x2.reshape(nb, seq, d)
```

```python
import functools

import numpy as np
import jax
import jax.numpy as jnp
from jax import lax
from jax.experimental import pallas as pl
from jax.experimental.pallas import tpu as pltpu

F32 = jnp.float32
BF16 = jnp.bfloat16

HEAD_DIM = 64
N_HEADS = 4
GROUP_W = N_HEADS * HEAD_DIM
CMP_LEN = 32
CMP_STRIDE = 16
CMP_HIDDEN = 256
SEL_BLOCK = 64
SEL_TOPK = 16
WIN = 512
FORCE = 1e4
DILATED_PAIRS = ((128, 1), (512, 4), (2048, 16))
POOL_SIZES = (2, 4, 8, 16)
SG_CHUNK = 128
RMS_EPS = 1e-6
LN_EPS = 1e-5
NEG = -1e30

LANES = 128
N_GATES = 3 * N_HEADS

PB_AQ, PB_SLC, PB_WIN, PB_BQ, PB_BK, PB_BV = 0, 2, 3, 4, 6, 8
PB_COLS = 10 * LANES
PF_CMP, PF_AG, PF_AZ, PF_BZ, PF_CIN, PF_CZ, PF_DU, PF_DV, PF_DZ = 0, 1, 2, 4, 6, 8, 10, 12, 14
PF_COLS = 16 * LANES

TM_PROJ = 256
TM_OUT = 512
TQ_NSA = 256
TK_SEL = 256
T_DIL = 256
TS_POOL = 512
POOL_HALO = 16
TW_WIN = 128
SEL_GROUPS = 2
DEN_ROWS = 16

Q_SCALE = float(np.log2(np.e)) * HEAD_DIM ** -0.5


def _sigmoid(x):
    return 1.0 / (1.0 + jnp.exp(-x))


def _silu(x):
    return x * _sigmoid(x)


def _dot_nt(a, b):
    return lax.dot_general(a, b, (((1,), (1,)), ((), ())), preferred_element_type=F32)


def _dot(a, b):
    return jnp.dot(a, b, preferred_element_type=F32)


def _in_proj_kernel(x_ref, g_ref, wa_ref, wg_ref, wb_ref, pb_ref, pf_ref):
    _project_in(x_ref[...], g_ref, wa_ref, wg_ref, wb_ref, pb_ref, pf_ref)


def _project_in(x, g_ref, wa_ref, wg_ref, wb_ref, pb_ref, pf_ref):
    inv = lax.rsqrt(jnp.mean(x * x, axis=-1, keepdims=True) + RMS_EPS)
    h = (x * inv * g_ref[...]).astype(BF16)
    cw = 2 * LANES

    def put(dst, c, w):
        dst[:, c * cw:(c + 1) * cw] = _dot(h, w).astype(dst.dtype)

    put(pb_ref, PB_AQ // 2, wa_ref[:, 0:cw])
    put(pb_ref, PB_SLC // 2, wa_ref[:, cw + LANES:2 * cw + LANES])
    for c in range(3):
        put(pb_ref, PB_BQ // 2 + c, wb_ref[:, (1 + c) * cw:(2 + c) * cw])
    put(pf_ref, PF_CMP // 2, jnp.concatenate([wa_ref[:, cw:cw + LANES], wg_ref[...]], axis=1))
    put(pf_ref, PF_AZ // 2, wb_ref[:, 0:cw])
    for c in range(6):
        put(pf_ref, PF_BZ // 2 + c, wb_ref[:, (4 + c) * cw:(5 + c) * cw])


def _in_proj(x2, g_pre, w_a, w_g, w_b, layer):
    m, d = x2.shape
    return pl.pallas_call(
        _in_proj_kernel,
        grid=(m // TM_PROJ,),
        in_specs=[
            pl.BlockSpec((TM_PROJ, d), lambda i: (i, 0)),
            pl.BlockSpec((None, 1, d), lambda i: (layer, 0, 0)),
            pl.BlockSpec((None, d, w_a.shape[2]), lambda i: (layer, 0, 0)),
            pl.BlockSpec((None, d, w_g.shape[2]), lambda i: (layer, 0, 0)),
            pl.BlockSpec((None, d, w_b.shape[2]), lambda i: (layer, 0, 0)),
        ],
        out_specs=[
            pl.BlockSpec((TM_PROJ, PB_COLS), lambda i: (i, 0)),
            pl.BlockSpec((TM_PROJ, PF_COLS), lambda i: (i, 0)),
        ],
        out_shape=[
            jax.ShapeDtypeStruct((m, PB_COLS), BF16),
            jax.ShapeDtypeStruct((m, PF_COLS), F32),
        ],
        compiler_params=pltpu.CompilerParams(
            dimension_semantics=("arbitrary",), vmem_limit_bytes=48 * 1024 * 1024),
        name="in_proj",
    )(x2, g_pre, w_a, w_g, w_b)


def _gelu_tanh(x):
    return 0.5 * x * (1.0 + jnp.tanh(np.sqrt(2.0 / np.pi).astype(np.float32) * (x + 0.044715 * (x * x * x))))


def _compress_kernel(x_ref, pe_ref, w1_ref, w2_ref, o_ref):
    n_grp = x_ref.shape[0] // CMP_STRIDE
    ng = o_ref.shape[1]
    lo = [None, None]
    hi = [None, None]
    for l0 in range(CMP_STRIDE):
        rows = x_ref[pl.ds(l0, n_grp, stride=CMP_STRIDE), :]
        r_lo = (rows + pe_ref[l0:l0 + 1, :]).astype(BF16)
        r_hi = (rows + pe_ref[CMP_STRIDE + l0:CMP_STRIDE + l0 + 1, :]).astype(BF16)
        for j in range(2):
            cols = slice(j * HEAD_DIM, (j + 1) * HEAD_DIM)
            d_lo = _dot(r_lo[:, cols], w1_ref[j, l0])
            d_hi = _dot(r_hi[:, cols], w1_ref[j, CMP_STRIDE + l0])
            lo[j] = d_lo if lo[j] is None else lo[j] + d_lo
            hi[j] = d_hi if hi[j] is None else hi[j] + d_hi
    row = lax.broadcasted_iota(jnp.int32, (n_grp, HEAD_DIM), 0) % ng
    for j in range(2):
        hid = _gelu_tanh(lo[j] + pltpu.roll(hi[j], n_grp - 1, 0)).astype(BF16)
        comp = jnp.where(row < ng - 1, _dot(hid, w2_ref[j]), 0.0)
        o_ref[:, :, j * HEAD_DIM:(j + 1) * HEAD_DIM] = comp.reshape(o_ref.shape[0], ng, HEAD_DIM).astype(BF16)


def _compress(pf, pe_rows, w1, w2, layer, nb, seq):
    ng = seq // CMP_STRIDE
    return pl.pallas_call(
        _compress_kernel,
        grid=(1,),
        in_specs=[
            pl.BlockSpec((nb * seq, LANES), lambda i: (0, PF_CMP)),
            pl.BlockSpec((None, CMP_LEN, LANES), lambda i: (layer, 0, 0)),
            pl.BlockSpec((None, 2, CMP_LEN, HEAD_DIM, CMP_HIDDEN), lambda i: (layer, 0, 0, 0, 0)),
            pl.BlockSpec((None, 2, CMP_HIDDEN, HEAD_DIM), lambda i: (layer, 0, 0, 0)),
        ],
        out_specs=pl.BlockSpec((nb, ng, LANES), lambda i: (0, 0, 0)),
        out_shape=jax.ShapeDtypeStruct((nb, ng, LANES), BF16),
        compiler_params=pltpu.CompilerParams(
            dimension_semantics=("arbitrary",), vmem_limit_bytes=48 * 1024 * 1024),
        name="compress",
    )(pf, pe_rows, w1, w2)


def _nsa_kernel(q_ref, slc_ref, win_ref, kc_ref, gate_ref, z_ref, ovl_ref, tri_ref, band_ref, o_ref,
                ksel_ref, vtsel_ref, kwin_ref, vtwin_ref, kct_ref, acc_ref, s_ref):
    tq = q_ref.shape[0]
    seq = slc_ref.shape[0]
    tk = s_ref.shape[1]
    tw = band_ref.shape[1]
    nh = N_HEADS
    n_slc = seq // SEL_BLOCK
    qi = pl.program_id(1)
    q0 = qi * tq

    @pl.when(qi == 0)
    def _():
        lane_k = lax.broadcasted_iota(jnp.int32, (tk, LANES), 1)
        for j in range(seq // tk):
            rows = slice(j * tk, (j + 1) * tk)
            blk = slc_ref[rows, :]
            key_blk = (j * tk + lax.broadcasted_iota(jnp.int32, (tk, LANES), 0)) // SEL_BLOCK
            onehot = jnp.where(lane_k - HEAD_DIM == key_blk, 1.0, 0.0).astype(BF16)
            ksel_ref[rows, :] = jnp.where(lane_k < HEAD_DIM, blk, onehot)
            vtsel_ref[j] = blk.astype(F32).T[HEAD_DIM:].astype(BF16)
        lane_w = lax.broadcasted_iota(jnp.int32, (WIN, LANES), 1)
        kwin_ref[0:WIN, :] = jnp.where(lane_w == HEAD_DIM, 1.0, 0.0).astype(BF16)
        for j in range(WIN // tw):
            vtwin_ref[j] = jnp.zeros((HEAD_DIM, tw), BF16)
        lane_t = lax.broadcasted_iota(jnp.int32, (tw, LANES), 1)
        for j in range(seq // tw):
            blk = win_ref[j * tw:(j + 1) * tw, :]
            kwin_ref[WIN + j * tw:WIN + (j + 1) * tw, :] = jnp.where(lane_t < HEAD_DIM, blk, jnp.zeros_like(blk))
            vtwin_ref[WIN // tw + j] = blk.astype(F32).T[HEAD_DIM:].astype(BF16)
        kct_ref[...] = kc_ref[...].astype(F32).T[HEAD_DIM:].astype(BF16)

    q_t = q_ref[...].astype(F32).T

    def q_operand(extra):
        return jnp.concatenate([jnp.concatenate([q_t[h * HEAD_DIM:(h + 1) * HEAD_DIM], extra], axis=0)
                                for h in range(nh)], axis=1).astype(BF16)

    def tile4(x):
        return jnp.concatenate([x] * nh, axis=1)


    span = WIN + tw
    extra_row = lax.broadcasted_iota(jnp.int32, (HEAD_DIM, tq), 0)
    q_win = q_operand(jnp.where(extra_row == 0, NEG, 0.0))
    n_sub = tq // tw
    s_win = []
    for u in range(n_sub):
        q_sub = jnp.concatenate([q_win[:, h * tq + u * tw:h * tq + (u + 1) * tw] for h in range(nh)], axis=1)
        start = pl.multiple_of(q0 + u * tw, tw)
        s_win.append(_dot(kwin_ref[pl.ds(start, span), :], q_sub))
    kc = kc_ref[...]
    s_c = _dot(kc, q_operand(jnp.zeros((HEAD_DIM, tq), F32)))

    def window_softmax(u):
        edge = [jnp.concatenate([band_ref[e]] * nh, axis=1) for e in range(2)]
        s_w = s_win[u]
        s_w = jnp.concatenate([s_w[0:tw] + edge[0], s_w[tw:span - tw], s_w[span - tw:] + edge[1]], axis=0)
        e_w = jnp.exp2(s_w - jnp.max(s_w, axis=0, keepdims=True))
        l_w = jnp.sum(e_w, axis=0, keepdims=True)
        t0 = qi * n_sub + u
        vt_w = jnp.concatenate([vtwin_ref[t0 + d] for d in range(span // tw)], axis=1)
        return _dot(vt_w, e_w.astype(BF16)) / l_w

    n_cmp = (seq - CMP_LEN) // CMP_STRIDE + 1
    n_idx = lax.broadcasted_iota(jnp.int32, (LANES, tq), 0)
    t_idx = q0 + lax.broadcasted_iota(jnp.int32, (LANES, tq), 1)
    valid_c = tile4((n_idx * CMP_STRIDE + (CMP_LEN - 1) <= t_idx) & (n_idx < n_cmp))
    s_c = jnp.where(valid_c, s_c, NEG)
    e_c = jnp.exp2(s_c - jnp.max(s_c, axis=0, keepdims=True))
    p_c = jnp.where(valid_c, e_c / jnp.sum(e_c, axis=0, keepdims=True), 0.0)
    o_cmp = _dot(kct_ref[...], p_c.astype(BF16))

    psum = p_c[:, 0:tq]
    for h in range(1, nh):
        psum = psum + p_c[:, h * tq:(h + 1) * tq]
    p_hi = psum.astype(BF16)
    r1 = psum - p_hi.astype(F32)
    p_mid = r1.astype(BF16)
    p_lo = (r1 - p_mid.astype(F32)).astype(BF16)
    ovl = ovl_ref[...]
    imp = _dot(ovl, p_hi) + _dot(ovl, p_mid) + _dot(ovl, p_lo)

    o_win = [window_softmax(u) for u in range(n_sub - 1)]

    v = imp[0:n_slc]
    jj = lax.broadcasted_iota(jnp.int32, (n_slc, tq), 0)
    cur = (q0 + lax.broadcasted_iota(jnp.int32, (n_slc, tq), 1)) // SEL_BLOCK
    forced = (jj == 0) | (jj == cur) | (jj == cur - 1)
    v = jnp.where(forced, FORCE, jnp.where(jj <= cur, v, -FORCE))
    sub = 8
    ranks = []
    row8 = lax.broadcasted_iota(jnp.int32, (sub, tq), 0)
    for r0 in range(0, n_slc, sub):
        vr = v[r0:r0 + sub]
        cnt = jnp.zeros((sub, tq), F32)
        for j2 in range(n_slc):
            vj = v[j2:j2 + 1, :]
            if j2 < r0:
                ahead = vj >= vr
            elif j2 >= r0 + sub:
                ahead = vj > vr
            else:
                ahead = (vj > vr) | ((vj == vr) & (row8 > j2 - r0))
            cnt = cnt + jnp.where(ahead, 1.0, 0.0)
        ranks.append(cnt)
    rank = jnp.concatenate(ranks, axis=0)
    sel_neg = jnp.where((rank < min(SEL_TOPK, n_slc)) & (jj <= cur), 0.0, NEG)
    q_sel = q_operand(jnp.concatenate([sel_neg, jnp.zeros((HEAD_DIM - n_slc, tq), F32)], axis=0))

    o_win.append(window_softmax(n_sub - 1))

    n_grp = s_ref.shape[0]
    gw = nh * tq // n_grp

    def scores(c, g):
        keys = ksel_ref[pl.ds(pl.multiple_of(c * tk, tk), tk), :]
        s = _dot(keys, q_sel[:, g * gw:(g + 1) * gw])
        s_ref[g] = s
        return jnp.max(s, axis=0, keepdims=True)

    def absorb(c, g, s, m_tile, m, l):
        m_new = jnp.maximum(m, m_tile)
        alpha = jnp.exp2(m - m_new)
        p = jnp.exp2(s - m_new)
        l = alpha * l + jnp.sum(p, axis=0, keepdims=True)
        acc_ref[g] = alpha * acc_ref[g] + _dot(vtsel_ref[c], p.astype(BF16))
        return m_new, l

    n_chunks = (q0 + tq + tk - 1) // tk
    acc_ref[...] = jnp.zeros_like(acc_ref)
    first_max = [scores(0, g) for g in range(n_grp)]

    def sel_step(c, carry):
        out = []
        for g in range(n_grp):
            m, l, m_tile = carry[g]
            s = s_ref[g]
            m_tile_next = scores(c + 1, g)
            out.append(absorb(c, g, s, m_tile, m, l) + (m_tile_next,))
        return tuple(out)

    stats = tuple((jnp.full((1, gw), NEG, F32), jnp.zeros((1, gw), F32), first_max[g]) for g in range(n_grp))
    stats = lax.fori_loop(0, n_chunks - 1, sel_step, stats)
    last = n_chunks - 1
    tri = tri_ref[(q0 - last * tk) // tq]
    tri_g = jnp.concatenate([tri] * (gw // tq), axis=1)
    o_slc = []
    for g in range(n_grp):
        s_last = s_ref[g] + tri_g
        _, l_g = absorb(last, g, s_last, jnp.max(s_last, axis=0, keepdims=True), *stats[g][:2])
        o_slc.append(acc_ref[g] / l_g)
    o_slc = jnp.concatenate(o_slc, axis=1)

    gate_t = _sigmoid(gate_ref[...]).T
    combs = []
    for h in range(nh):
        cols = slice(h * tq, (h + 1) * tq)
        o_win_h = jnp.concatenate([ow[:, h * tw:(h + 1) * tw] for ow in o_win], axis=1)
        combs.append(gate_t[0 * nh + h:0 * nh + h + 1, :] * o_cmp[:, cols]
                     + gate_t[1 * nh + h:1 * nh + h + 1, :] * o_slc[:, cols]
                     + gate_t[2 * nh + h:2 * nh + h + 1, :] * o_win_h)
    for c in range(nh // 2):
        blk = jnp.concatenate([combs[2 * c], combs[2 * c + 1]], axis=0).T
        zc = z_ref[:, c * LANES:(c + 1) * LANES]
        o_ref[:, c * LANES:(c + 1) * LANES] = (blk * _silu(zc)).astype(o_ref.dtype)


def _nsa(pb, pf, kcvc, ovl, tri, band, nb, seq):
    tq = TQ_NSA
    tk = TK_SEL
    nq = seq // tq
    return pl.pallas_call(
        _nsa_kernel,
        grid=(nb, nq),
        in_specs=[
            pl.BlockSpec((tq, 2 * LANES), lambda b, i: (b * nq + i, PB_AQ // 2)),
            pl.BlockSpec((seq, LANES), lambda b, i: (b, PB_SLC)),
            pl.BlockSpec((seq, LANES), lambda b, i: (b, PB_WIN)),
            pl.BlockSpec((None, LANES, LANES), lambda b, i: (b, 0, 0)),
            pl.BlockSpec((tq, LANES), lambda b, i: (b * nq + i, PF_AG)),
            pl.BlockSpec((tq, 2 * LANES), lambda b, i: (b * nq + i, PF_AZ // 2)),
            pl.BlockSpec((LANES, LANES), lambda b, i: (0, 0)),
            pl.BlockSpec(tri.shape, lambda b, i: (0, 0, 0)),
            pl.BlockSpec(band.shape, lambda b, i: (0, 0, 0)),
        ],
        out_specs=pl.BlockSpec((tq, GROUP_W), lambda b, i: (b * nq + i, 0)),
        out_shape=jax.ShapeDtypeStruct((nb * seq, GROUP_W), BF16),
        scratch_shapes=[
            pltpu.VMEM((seq, LANES), BF16),
            pltpu.VMEM((seq // tk, HEAD_DIM, tk), BF16),
            pltpu.VMEM((WIN + seq, LANES), BF16),
            pltpu.VMEM(((WIN + seq) // TW_WIN, HEAD_DIM, TW_WIN), BF16),
            pltpu.VMEM((HEAD_DIM, LANES), BF16),
            pltpu.VMEM((SEL_GROUPS, HEAD_DIM, N_HEADS * tq // SEL_GROUPS), F32),
            pltpu.VMEM((SEL_GROUPS, tk, N_HEADS * tq // SEL_GROUPS), F32),
        ],
        compiler_params=pltpu.CompilerParams(
            dimension_semantics=("arbitrary", "arbitrary"), vmem_limit_bytes=48 * 1024 * 1024),
        name="nsa",
    )(pb, pb, pb, kcvc, pf, pf, ovl, tri, band)


def _dilated_kernel(q_ref, k_ref, v_ref, z_ref, bias_ref, o_ref, vt_ref, acc_ref, s_ref):
    t = q_ref.shape[0]
    seq = k_ref.shape[0]
    n_pairs = N_HEADS // 2
    rows_h = DEN_ROWS + HEAD_DIM
    qi = pl.program_id(1)

    @pl.when(qi == 0)
    def _():
        ones_rows = jnp.where(lax.broadcasted_iota(jnp.int32, (DEN_ROWS, t), 0) == 0, 1.0, 0.0).astype(BF16)
        for j in range(seq // t):
            v_t = v_ref[j * t:(j + 1) * t, :].astype(F32).T.astype(BF16)
            for h in range(N_HEADS):
                vt_ref[j, h * rows_h:h * rows_h + DEN_ROWS] = ones_rows
                vt_ref[j, h * rows_h + DEN_ROWS:(h + 1) * rows_h] = v_t[h * HEAD_DIM:(h + 1) * HEAD_DIM]

    q_t = q_ref[...].astype(F32).T
    upper = lax.broadcasted_iota(jnp.int32, (LANES, t), 0) >= HEAD_DIM
    qs = []
    for c in range(n_pairs):
        blk = q_t[c * LANES:(c + 1) * LANES]
        qs.append(jnp.concatenate([jnp.where(upper, 0.0, blk), jnp.where(upper, blk, 0.0)], axis=1).astype(BF16))
    acc_ref[...] = jnp.zeros_like(acc_ref)

    def scores(j, c):
        k0 = pl.multiple_of(j * t, t)
        bias = bias_ref[qi - j]
        s = _dot(k_ref[pl.ds(k0, t), c * LANES:(c + 1) * LANES], qs[c]) + jnp.concatenate([bias, bias], axis=1)
        s_ref[c] = s
        return jnp.max(s, axis=0, keepdims=True)

    tile_max = tuple(scores(0, c) for c in range(n_pairs))

    def step(j, carry):
        j_next = jnp.minimum(j + 1, qi)
        out = []
        for c in range(n_pairs):
            m, m_tile = carry[c]
            s = s_ref[c]
            m_tile_next = scores(j_next, c)
            m_new = jnp.maximum(m, m_tile)
            alpha = jnp.exp2(m - m_new)
            pb = jnp.exp2(s - m_new).astype(BF16)
            for e in range(2):
                h = 2 * c + e
                pv = _dot(vt_ref[j, h * rows_h:(h + 1) * rows_h, :], pb[:, e * t:(e + 1) * t])
                acc_ref[h] = alpha[:, e * t:(e + 1) * t] * acc_ref[h] + pv
            out.append((m_new, m_tile_next))
        return tuple(out)

    init = tuple((jnp.full((1, 2 * t), NEG, F32), tile_max[c]) for c in range(n_pairs))
    lax.fori_loop(0, qi + 1, step, init)
    for c in range(n_pairs):
        heads = [acc_ref[2 * c + e] for e in range(2)]
        o_pair = jnp.concatenate([a[DEN_ROWS:] / a[0:1] for a in heads], axis=0).T
        zc = z_ref[:, c * LANES:(c + 1) * LANES]
        o_ref[:, c * LANES:(c + 1) * LANES] = (o_pair * _silu(zc)).astype(o_ref.dtype)


def _dilated(pb, pf, bias_tab, nb, seq):
    t = T_DIL
    nq = seq // t
    return pl.pallas_call(
        _dilated_kernel,
        grid=(nb, nq),
        in_specs=[
            pl.BlockSpec((t, GROUP_W), lambda b, i: (b * nq + i, PB_BQ // 2)),
            pl.BlockSpec((seq, GROUP_W), lambda b, i: (b, PB_BK // 2)),
            pl.BlockSpec((seq, GROUP_W), lambda b, i: (b, PB_BV // 2)),
            pl.BlockSpec((t, GROUP_W), lambda b, i: (b * nq + i, PF_BZ // 2)),
            pl.BlockSpec((nq, t, t), lambda b, i: (0, 0, 0)),
        ],
        out_specs=pl.BlockSpec((t, GROUP_W), lambda b, i: (b * nq + i, 0)),
        out_shape=jax.ShapeDtypeStruct((nb * seq, GROUP_W), BF16),
        scratch_shapes=[
            pltpu.VMEM((nq, N_HEADS * (DEN_ROWS + HEAD_DIM), t), BF16),
            pltpu.VMEM((N_HEADS, DEN_ROWS + HEAD_DIM, t), F32),
            pltpu.VMEM((N_HEADS // 2, t, 2 * t), F32),
        ],
        compiler_params=pltpu.CompilerParams(
            dimension_semantics=("arbitrary", "arbitrary"), vmem_limit_bytes=48 * 1024 * 1024),
        name="dilated",
    )(pb, pb, pb, pf, bias_tab)


def _pool_sgu_kernel(c_ref, cprev_ref, cz_ref, du_ref, dv_ref, dz_ref,
                     wpool_ref, pscale_ref, lng_ref, lnb_ref, wsp_ref, bsp_ref, yc_ref, yd_ref):
    ts = c_ref.shape[0]
    i = pl.program_id(1)
    lane = lax.broadcasted_iota(jnp.int32, (ts, GROUP_W), 1)

    cur = c_ref[...]
    prev = jnp.where(i > 0, cprev_ref[...], 0.0)
    ext = jnp.concatenate([prev, cur], axis=0)
    t_pos = i * ts + lax.broadcasted_iota(jnp.int32, (ts, 1), 0)
    pooled = None
    acc = ext
    width = 1
    for g, w in enumerate(POOL_SIZES):
        while width < w:
            acc = acc + pltpu.roll(acc, width, 0)
            width *= 2
        cnt = jnp.minimum(t_pos + 1, w).astype(F32)
        mean_w = acc[POOL_HALO:POOL_HALO + ts] * (1.0 / cnt)
        pooled = mean_w if pooled is None else jnp.where(lane >= g * HEAD_DIM, mean_w, pooled)
    pooled = pooled - cur
    mixed = _dot(pooled.astype(BF16), wpool_ref[...]) * pscale_ref[...]
    yc_ref[...] = (mixed * _silu(cz_ref[...])).astype(yc_ref.dtype)

    v = dv_ref[...]
    mu = jnp.mean(v, axis=-1, keepdims=True)
    var = jnp.mean(jnp.square(v - mu), axis=-1, keepdims=True)
    vn = ((v - mu) * lax.rsqrt(var + LN_EPS) * lng_ref[...] + lnb_ref[...]).astype(BF16)
    r = lax.broadcasted_iota(jnp.int32, (SG_CHUNK, SG_CHUNK), 0)
    cidx = lax.broadcasted_iota(jnp.int32, (SG_CHUNK, SG_CHUNK), 1)
    w_tril = [jnp.where(cidx <= r, wsp_ref[g], 0.0).astype(BF16) for g in range(N_HEADS)]
    lane_c = lax.broadcasted_iota(jnp.int32, (SG_CHUNK, GROUP_W), 1)
    for ci in range(ts // SG_CHUNK):
        rows = slice(ci * SG_CHUNK, (ci + 1) * SG_CHUNK)
        vc = vn[rows]
        zmix = _dot(w_tril[0], vc)
        for g in range(1, N_HEADS):
            zmix = jnp.where(lane_c >= g * HEAD_DIM, _dot(w_tril[g], vc), zmix)
        zfull = zmix + bsp_ref[...]
        yd_ref[rows, :] = (du_ref[rows, :] * zfull * _silu(dz_ref[rows, :])).astype(yd_ref.dtype)


def _pool_sgu(pf, wpool_bd, pool_scale, ln_g, ln_b, w_sp, bsp_exp, layer, nb, seq):
    ts = TS_POOL
    ns = seq // ts
    halo_blocks = ts // POOL_HALO

    def tile(col):
        return pl.BlockSpec((ts, GROUP_W), lambda b, i: (b * ns + i, col // 2))

    def per_layer(shape):
        nd = len(shape)
        return pl.BlockSpec((None,) + shape, lambda b, i: (layer,) + (0,) * nd)

    return pl.pallas_call(
        _pool_sgu_kernel,
        grid=(nb, ns),
        in_specs=[
            tile(PF_CIN),
            pl.BlockSpec((POOL_HALO, GROUP_W),
                         lambda b, i: (jnp.maximum((b * ns + i) * halo_blocks - 1, 0), PF_CIN // 2)),
            tile(PF_CZ), tile(PF_DU), tile(PF_DV), tile(PF_DZ),
            per_layer((GROUP_W, GROUP_W)),
            per_layer((1, GROUP_W)),
            per_layer((1, GROUP_W)),
            per_layer((1, GROUP_W)),
            per_layer((N_HEADS, SG_CHUNK, SG_CHUNK)),
            per_layer((SG_CHUNK, GROUP_W)),
        ],
        out_specs=[
            pl.BlockSpec((ts, GROUP_W), lambda b, i: (b * ns + i, 0)),
            pl.BlockSpec((ts, GROUP_W), lambda b, i: (b * ns + i, 0)),
        ],
        out_shape=[
            jax.ShapeDtypeStruct((nb * seq, GROUP_W), BF16),
            jax.ShapeDtypeStruct((nb * seq, GROUP_W), BF16),
        ],
        compiler_params=pltpu.CompilerParams(
            dimension_semantics=("arbitrary", "arbitrary"), vmem_limit_bytes=48 * 1024 * 1024),
        name="pool_sgu",
    )(pf, pf, pf, pf, pf, pf, wpool_bd, pool_scale, ln_g, ln_b, w_sp, bsp_exp)


def _project_out(ya_ref, yb_ref, yc_ref, yd_ref, w_ref, x_ref, g_ref):
    y = jnp.concatenate([ya_ref[...], yb_ref[...], yc_ref[...], yd_ref[...]], axis=1)
    out = _dot(y, w_ref[...])
    inv = lax.rsqrt(jnp.mean(out * out, axis=-1, keepdims=True) + RMS_EPS)
    return x_ref[...] + out * inv * g_ref[...]


def _out_proj_kernel(ya_ref, yb_ref, yc_ref, yd_ref, w_ref, x_ref, g_ref, o_ref):
    o_ref[...] = _project_out(ya_ref, yb_ref, yc_ref, yd_ref, w_ref, x_ref, g_ref)


def _out_in_proj_kernel(ya_ref, yb_ref, yc_ref, yd_ref, wo_ref, x_ref, gpost_ref,
                        gpre_ref, wa_ref, wg_ref, wb_ref, o_ref, pb_ref, pf_ref):
    half = o_ref.shape[0] // 2
    x_new = []
    for r in range(2):
        rows = pl.ds(r * half, half)
        x_new.append(_project_out(ya_ref.at[rows], yb_ref.at[rows], yc_ref.at[rows], yd_ref.at[rows],
                                  wo_ref, x_ref.at[rows], gpost_ref))
        o_ref[rows, :] = x_new[r]
    for r in range(2):
        rows = pl.ds(r * half, half)
        _project_in(x_new[r], gpre_ref, wa_ref, wg_ref, wb_ref, pb_ref.at[rows], pf_ref.at[rows])


def _out_in_proj(ya, yb, yc, yd, w_out, x2, g_post, g_pre, w_a, w_g, w_b, layer):
    m, d = x2.shape
    tm = TM_OUT
    ytile = pl.BlockSpec((tm, GROUP_W), lambda i: (i, 0))
    nxt = layer + 1
    return pl.pallas_call(
        _out_in_proj_kernel,
        grid=(m // tm,),
        in_specs=[
            ytile, ytile, ytile, ytile,
            pl.BlockSpec((None, 4 * GROUP_W, d), lambda i: (layer, 0, 0)),
            pl.BlockSpec((tm, d), lambda i: (i, 0)),
            pl.BlockSpec((None, 1, d), lambda i: (layer, 0, 0)),
            pl.BlockSpec((None, 1, d), lambda i: (nxt, 0, 0)),
            pl.BlockSpec((None, d, w_a.shape[2]), lambda i: (nxt, 0, 0)),
            pl.BlockSpec((None, d, w_g.shape[2]), lambda i: (nxt, 0, 0)),
            pl.BlockSpec((None, d, w_b.shape[2]), lambda i: (nxt, 0, 0)),
        ],
        out_specs=[
            pl.BlockSpec((tm, d), lambda i: (i, 0)),
            pl.BlockSpec((tm, PB_COLS), lambda i: (i, 0)),
            pl.BlockSpec((tm, PF_COLS), lambda i: (i, 0)),
        ],
        out_shape=[
            jax.ShapeDtypeStruct((m, d), F32),
            jax.ShapeDtypeStruct((m, PB_COLS), BF16),
            jax.ShapeDtypeStruct((m, PF_COLS), F32),
        ],
        compiler_params=pltpu.CompilerParams(
            dimension_semantics=("arbitrary",), vmem_limit_bytes=56 * 1024 * 1024),
        name="out_in_proj",
    )(ya, yb, yc, yd, w_out, x2, g_post, g_pre, w_a, w_g, w_b)


def _out_proj(ya, yb, yc, yd, w_out, x2, g_post, layer):
    m, d = x2.shape
    tm = TM_OUT
    ytile = pl.BlockSpec((tm, GROUP_W), lambda i: (i, 0))
    return pl.pallas_call(
        _out_proj_kernel,
        grid=(m // tm,),
        in_specs=[
            ytile, ytile, ytile, ytile,
            pl.BlockSpec((None, 4 * GROUP_W, d), lambda i: (layer, 0, 0)),
            pl.BlockSpec((tm, d), lambda i: (i, 0)),
            pl.BlockSpec((None, 1, d), lambda i: (layer, 0, 0)),
        ],
        out_specs=pl.BlockSpec((tm, d), lambda i: (i, 0)),
        out_shape=jax.ShapeDtypeStruct((m, d), F32),
        compiler_params=pltpu.CompilerParams(
            dimension_semantics=("arbitrary",), vmem_limit_bytes=48 * 1024 * 1024),
        name="out_proj",
    )(ya, yb, yc, yd, w_out, x2, g_post)


def _overlap_t(seq):
    n_cmp = (seq - CMP_LEN) // CMP_STRIDE + 1
    n_slc = seq // SEL_BLOCK
    cs = np.arange(n_cmp) * CMP_STRIDE
    ss = np.arange(n_slc) * SEL_BLOCK
    ov = (cs[None, :] < ss[:, None] + SEL_BLOCK) & (cs[None, :] + CMP_LEN > ss[:, None])
    out = np.zeros((LANES, LANES), np.float32)
    out[:n_slc, :n_cmp] = ov
    return out


def _causal_tri():
    a = np.arange(TK_SEL)[:, None]
    b = np.arange(TQ_NSA)[None, :]
    return np.stack([np.where(a - o * TQ_NSA <= b, 0.0, NEG) for o in range(TK_SEL // TQ_NSA)]).astype(np.float32)


def _window_band():
    a = np.arange(TW_WIN)[:, None]
    b = np.arange(TW_WIN)[None, :]
    return np.stack([np.where(a > b, 0.0, NEG), np.where(a <= b, 0.0, NEG)]).astype(np.float32)


def _dilated_log_multiplicity(seq):
    t = T_DIL
    d0 = np.arange(t)[:, None] - np.arange(t)[None, :]
    tabs = []
    for delta in range(seq // t):
        d = d0 + delta * t
        mult = np.zeros_like(d)
        for window, dil in DILATED_PAIRS:
            mult += (d >= 0) & (d % dil == 0) & (d // dil <= window // dil) & (d // dil <= seq // dil - 1)
        tabs.append(np.where(mult > 0, np.log2(np.maximum(mult, 1)), NEG).T)
    return np.stack(tabs).astype(np.float32)


def _split_w_in(w_in):
    gw = GROUP_W
    n_a = gw + 6 * HEAD_DIM
    n_b = 10 * gw
    assert w_in.shape[-1] == n_a + N_GATES + n_b, w_in.shape
    scale_a = np.ones((n_a,), np.float32)
    scale_a[:gw] = Q_SCALE
    scale_b = np.ones((n_b,), np.float32)
    scale_b[gw:2 * gw] = Q_SCALE
    w_a = (w_in[..., :n_a] * scale_a).astype(BF16)
    w_g = jnp.pad(w_in[..., n_a:n_a + N_GATES], ((0, 0), (0, 0), (0, LANES - N_GATES))).astype(BF16)
    w_b = (w_in[..., n_a + N_GATES:] * scale_b).astype(BF16)
    return w_a, w_g, w_b


def kernel(x, g_pre, w_in, pe_cmp, w_cmp1, w_cmp2, w_pool, pool_scale, sg_ln_g, sg_ln_b, w_sp, b_sp, w_out, g_post):
    nb, seq, d = x.shape
    depth = w_in.shape[0]
    assert seq % TK_SEL == 0 and seq % T_DIL == 0 and seq % TS_POOL == 0 and (nb * seq) % TM_PROJ == 0
    assert TQ_NSA % TW_WIN == 0 and TK_SEL % TQ_NSA == 0 and WIN % TW_WIN == 0 and TW_WIN % LANES == 0
    assert seq // SEL_BLOCK <= LANES - HEAD_DIM and (seq - CMP_LEN) // CMP_STRIDE + 1 < LANES
    assert seq % CMP_STRIDE == 0 and CMP_LEN == 2 * CMP_STRIDE

    w_a, w_g, w_b = _split_w_in(w_in)
    w_out_b = w_out.astype(BF16)
    w1_l = w_cmp1.astype(BF16).reshape(depth, 2, CMP_LEN, HEAD_DIM, CMP_HIDDEN)
    w2_b = w_cmp2.astype(BF16)
    pe_rows = jnp.concatenate([pe_cmp[:, 0], pe_cmp[:, 1]], axis=-1)
    eye = jnp.eye(N_HEADS, dtype=w_pool.dtype)
    wpool_bd = jnp.einsum('lgcd,gh->lgchd', w_pool, eye).reshape(depth, GROUP_W, GROUP_W).astype(BF16)
    bsp_exp = jnp.repeat(jnp.swapaxes(b_sp, 1, 2), HEAD_DIM, axis=2)
    g_pre3 = g_pre.reshape(depth, 1, d)
    g_post3 = g_post.reshape(depth, 1, d)
    pscale3 = pool_scale.reshape(depth, 1, GROUP_W)
    lng3 = sg_ln_g.reshape(depth, 1, GROUP_W)
    lnb3 = sg_ln_b.reshape(depth, 1, GROUP_W)
    ovl_t = jnp.asarray(_overlap_t(seq), BF16)
    tri = jnp.asarray(_causal_tri())
    band = jnp.asarray(_window_band())
    dil_bias = jnp.asarray(_dilated_log_multiplicity(seq))

    x2 = x.reshape(nb * seq, d)
    pb, pf = _in_proj(x2, g_pre3, w_a, w_g, w_b, 0)
    for layer in range(depth):
        kcvc = _compress(pf, pe_rows, w1_l, w2_b, layer, nb, seq)
        ya = _nsa(pb, pf, kcvc, ovl_t, tri, band, nb, seq)
        yb = _dilated(pb, pf, dil_bias, nb, seq)
        yc, yd = _pool_sgu(pf, wpool_bd, pscale3, lng3, lnb3, w_sp, bsp_exp, layer, nb, seq)
        if layer + 1 < depth:
            x2, pb, pf = _out_in_proj(ya, yb, yc, yd, w_out_b, x2, g_post3, g_pre3, w_a, w_g, w_b, layer)
        else:
            x2 = _out_proj(ya, yb, yc, yd, w_out_b, x2, g_post3, layer)
    return x2.reshape(nb, seq, d)
```

```python
import functools

import numpy as np
import jax
import jax.numpy as jnp
from jax import lax
from jax.experimental import pallas as pl
from jax.experimental.pallas import tpu as pltpu

F32 = jnp.float32
BF16 = jnp.bfloat16

HEAD_DIM = 64
N_HEADS = 4
GROUP_W = N_HEADS * HEAD_DIM
CMP_LEN = 32
CMP_STRIDE = 16
CMP_HIDDEN = 256
SEL_BLOCK = 64
SEL_TOPK = 16
WIN = 512
FORCE = 1e4
DILATED_PAIRS = ((128, 1), (512, 4), (2048, 16))
POOL_SIZES = (2, 4, 8, 16)
SG_CHUNK = 128
RMS_EPS = 1e-6
LN_EPS = 1e-5
NEG = -1e30

LANES = 128
N_GATES = 3 * N_HEADS

PB_AQ, PB_SLC, PB_WIN, PB_BQ, PB_BK, PB_BV = 0, 2, 3, 4, 6, 8
PB_COLS = 10 * LANES
PF_CMP, PF_AG, PF_AZ, PF_BZ, PF_CIN, PF_CZ, PF_DU, PF_DV, PF_DZ = 0, 1, 2, 4, 6, 8, 10, 12, 14
PF_COLS = 16 * LANES

TM_PROJ = 256
TM_OUT = 512
TQ_NSA = 256
TK_SEL = 256
T_DIL = 256
TS_POOL = 512
POOL_HALO = 16
TW_WIN = 128
SEL_GROUPS = 2
DEN_ROWS = 16

Q_SCALE = float(np.log2(np.e)) * HEAD_DIM ** -0.5


def _sigmoid(x):
    return 0.5 * jnp.tanh(0.5 * x) + 0.5


def _silu(x):
    return x * _sigmoid(x)


def _dot_nt(a, b):
    return lax.dot_general(a, b, (((1,), (1,)), ((), ())), preferred_element_type=F32)


def _dot(a, b):
    return jnp.dot(a, b, preferred_element_type=F32)


def _in_proj_kernel(x_ref, g_ref, wa_ref, wg_ref, wb_ref, pb_ref, pf_ref):
    _project_in(x_ref[...], g_ref, wa_ref, wg_ref, wb_ref, pb_ref, pf_ref)


def _project_in(x, g_ref, wa_ref, wg_ref, wb_ref, pb_ref, pf_ref):
    inv = lax.rsqrt(jnp.mean(x * x, axis=-1, keepdims=True) + RMS_EPS)
    h = (x * inv * g_ref[...]).astype(BF16)
    cw = 2 * LANES

    def put(dst, c, w):
        dst[:, c * cw:(c + 1) * cw] = _dot(h, w).astype(dst.dtype)

    put(pb_ref, PB_AQ // 2, wa_ref[:, 0:cw])
    put(pb_ref, PB_SLC // 2, wa_ref[:, cw + LANES:2 * cw + LANES])
    for c in range(3):
        put(pb_ref, PB_BQ // 2 + c, wb_ref[:, (1 + c) * cw:(2 + c) * cw])
    put(pf_ref, PF_CMP // 2, jnp.concatenate([wa_ref[:, cw:cw + LANES], wg_ref[...]], axis=1))
    put(pf_ref, PF_AZ // 2, wb_ref[:, 0:cw])
    for c in range(6):
        put(pf_ref, PF_BZ // 2 + c, wb_ref[:, (4 + c) * cw:(5 + c) * cw])


def _in_proj(x2, g_pre, w_a, w_g, w_b, layer):
    m, d = x2.shape
    return pl.pallas_call(
        _in_proj_kernel,
        grid=(m // TM_PROJ,),
        in_specs=[
            pl.BlockSpec((TM_PROJ, d), lambda i: (i, 0)),
            pl.BlockSpec((None, 1, d), lambda i: (layer, 0, 0)),
            pl.BlockSpec((None, d, w_a.shape[2]), lambda i: (layer, 0, 0)),
            pl.BlockSpec((None, d, w_g.shape[2]), lambda i: (layer, 0, 0)),
            pl.BlockSpec((None, d, w_b.shape[2]), lambda i: (layer, 0, 0)),
        ],
        out_specs=[
            pl.BlockSpec((TM_PROJ, PB_COLS), lambda i: (i, 0)),
            pl.BlockSpec((TM_PROJ, PF_COLS), lambda i: (i, 0)),
        ],
        out_shape=[
            jax.ShapeDtypeStruct((m, PB_COLS), BF16),
            jax.ShapeDtypeStruct((m, PF_COLS), F32),
        ],
        compiler_params=pltpu.CompilerParams(
            dimension_semantics=("arbitrary",), vmem_limit_bytes=48 * 1024 * 1024),
        name="in_proj",
    )(x2, g_pre, w_a, w_g, w_b)


def _gelu_tanh(x):
    return 0.5 * x * (1.0 + jnp.tanh(np.sqrt(2.0 / np.pi).astype(np.float32) * (x + 0.044715 * (x * x * x))))


def _compress_kernel(x_ref, pe_ref, w1_ref, w2_ref, o_ref):
    n_grp = x_ref.shape[0] // CMP_STRIDE
    ng = o_ref.shape[1]
    lo = [None, None]
    hi = [None, None]
    for l0 in range(CMP_STRIDE):
        rows = x_ref[pl.ds(l0, n_grp, stride=CMP_STRIDE), :]
        r_lo = (rows + pe_ref[l0:l0 + 1, :]).astype(BF16)
        r_hi = (rows + pe_ref[CMP_STRIDE + l0:CMP_STRIDE + l0 + 1, :]).astype(BF16)
        for j in range(2):
            cols = slice(j * HEAD_DIM, (j + 1) * HEAD_DIM)
            d_lo = _dot(r_lo[:, cols], w1_ref[j, l0].astype(BF16))
            d_hi = _dot(r_hi[:, cols], w1_ref[j, CMP_STRIDE + l0].astype(BF16))
            lo[j] = d_lo if lo[j] is None else lo[j] + d_lo
            hi[j] = d_hi if hi[j] is None else hi[j] + d_hi
    row = lax.broadcasted_iota(jnp.int32, (n_grp, HEAD_DIM), 0) % ng
    for j in range(2):
        hid = _gelu_tanh(lo[j] + pltpu.roll(hi[j], n_grp - 1, 0)).astype(BF16)
        comp = jnp.where(row < ng - 1, _dot(hid, w2_ref[j]), 0.0)
        o_ref[:, :, j * HEAD_DIM:(j + 1) * HEAD_DIM] = comp.reshape(o_ref.shape[0], ng, HEAD_DIM).astype(BF16)


def _compress(pf, pe_rows, w1, w2, layer, nb, seq):
    ng = seq // CMP_STRIDE
    return pl.pallas_call(
        _compress_kernel,
        grid=(1,),
        in_specs=[
            pl.BlockSpec((nb * seq, LANES), lambda i: (0, PF_CMP)),
            pl.BlockSpec((None, CMP_LEN, LANES), lambda i: (layer, 0, 0)),
            pl.BlockSpec((None, 2, CMP_LEN, HEAD_DIM, CMP_HIDDEN), lambda i: (layer, 0, 0, 0, 0)),
            pl.BlockSpec((None, 2, CMP_HIDDEN, HEAD_DIM), lambda i: (layer, 0, 0, 0)),
        ],
        out_specs=pl.BlockSpec((nb, ng, LANES), lambda i: (0, 0, 0)),
        out_shape=jax.ShapeDtypeStruct((nb, ng, LANES), BF16),
        compiler_params=pltpu.CompilerParams(
            dimension_semantics=("arbitrary",), vmem_limit_bytes=48 * 1024 * 1024),
        name="compress",
    )(pf, pe_rows, w1, w2)


def _nsa_kernel(q_ref, slc_ref, win_ref, kc_ref, gate_ref, z_ref, ovl_ref, tri_ref, band_ref, o_ref,
                ksel_ref, vtsel_ref, kwin_ref, vtwin_ref, kct_ref, acc_ref, s_ref):
    tq = q_ref.shape[0]
    seq = slc_ref.shape[0]
    tk = s_ref.shape[1]
    tw = band_ref.shape[1]
    nh = N_HEADS
    n_slc = seq // SEL_BLOCK
    qi = pl.program_id(1)
    q0 = qi * tq

    @pl.when(qi == 0)
    def _():
        lane_k = lax.broadcasted_iota(jnp.int32, (tk, LANES), 1)
        for j in range(seq // tk):
            rows = slice(j * tk, (j + 1) * tk)
            blk = slc_ref[rows, :]
            key_blk = (j * tk + lax.broadcasted_iota(jnp.int32, (tk, LANES), 0)) // SEL_BLOCK
            onehot = jnp.where(lane_k - HEAD_DIM == key_blk, 1.0, 0.0).astype(BF16)
            ksel_ref[rows, :] = jnp.where(lane_k < HEAD_DIM, blk, onehot)
            vtsel_ref[j] = blk.astype(F32).T[HEAD_DIM:].astype(BF16)
        lane_w = lax.broadcasted_iota(jnp.int32, (WIN, LANES), 1)
        kwin_ref[0:WIN, :] = jnp.where(lane_w == HEAD_DIM, 1.0, 0.0).astype(BF16)
        for j in range(WIN // tw):
            vtwin_ref[j] = jnp.zeros((HEAD_DIM, tw), BF16)
        lane_t = lax.broadcasted_iota(jnp.int32, (tw, LANES), 1)
        for j in range(seq // tw):
            blk = win_ref[j * tw:(j + 1) * tw, :]
            kwin_ref[WIN + j * tw:WIN + (j + 1) * tw, :] = jnp.where(lane_t < HEAD_DIM, blk, jnp.zeros_like(blk))
            vtwin_ref[WIN // tw + j] = blk.astype(F32).T[HEAD_DIM:].astype(BF16)
        kct_ref[...] = kc_ref[...].astype(F32).T[HEAD_DIM:].astype(BF16)

    q_t = q_ref[...].astype(F32).T

    def q_operand(extra):
        return jnp.concatenate([jnp.concatenate([q_t[h * HEAD_DIM:(h + 1) * HEAD_DIM], extra], axis=0)
                                for h in range(nh)], axis=1).astype(BF16)

    def tile4(x):
        return jnp.concatenate([x] * nh, axis=1)


    span = WIN + tw
    extra_row = lax.broadcasted_iota(jnp.int32, (HEAD_DIM, tq), 0)
    q_win = q_operand(jnp.where(extra_row == 0, NEG, 0.0))
    n_sub = tq // tw

    def window_scores(u):
        q_sub = jnp.concatenate([q_win[:, h * tq + u * tw:h * tq + (u + 1) * tw] for h in range(nh)], axis=1)
        start = pl.multiple_of(q0 + u * tw, tw)
        return _dot(kwin_ref[pl.ds(start, span), :], q_sub)

    kc = kc_ref[...]
    s_c = _dot(kc, q_operand(jnp.zeros((HEAD_DIM, tq), F32)))
    s_win = [window_scores(0)]

    def window_softmax(u):
        edge = [jnp.concatenate([band_ref[e]] * nh, axis=1) for e in range(2)]
        s_w = s_win[u]
        s_w = jnp.concatenate([s_w[0:tw] + edge[0], s_w[tw:span - tw], s_w[span - tw:] + edge[1]], axis=0)
        e_w = jnp.exp2(s_w - jnp.max(s_w, axis=0, keepdims=True))
        t0 = qi * n_sub + u
        vt_w = jnp.concatenate([vtwin_ref[t0 + d] for d in range(span // tw)], axis=1)
        return _dot(vt_w, e_w.astype(BF16)) / jnp.sum(e_w, axis=0, keepdims=True)

    n_cmp = (seq - CMP_LEN) // CMP_STRIDE + 1
    n_idx = lax.broadcasted_iota(jnp.int32, (LANES, tq), 0)
    t_idx = q0 + lax.broadcasted_iota(jnp.int32, (LANES, tq), 1)
    valid_c = tile4((n_idx * CMP_STRIDE + (CMP_LEN - 1) <= t_idx) & (n_idx < n_cmp))
    s_c = jnp.where(valid_c, s_c, NEG)
    e_c = jnp.exp2(s_c - jnp.max(s_c, axis=0, keepdims=True))
    p_c = jnp.where(valid_c, e_c / jnp.sum(e_c, axis=0, keepdims=True), 0.0)
    o_cmp = _dot(kct_ref[...], p_c.astype(BF16))

    psum = p_c[:, 0:tq]
    for h in range(1, nh):
        psum = psum + p_c[:, h * tq:(h + 1) * tq]
    p_hi = psum.astype(BF16)
    r1 = psum - p_hi.astype(F32)
    p_mid = r1.astype(BF16)
    p_lo = (r1 - p_mid.astype(F32)).astype(BF16)
    ovl = ovl_ref[...]
    imp = _dot(ovl, p_hi) + _dot(ovl, p_mid) + _dot(ovl, p_lo)

    o_win = []
    for u in range(n_sub - 1):
        s_win.append(window_scores(u + 1))
        o_win.append(window_softmax(u))

    v = imp[0:n_slc]
    jj = lax.broadcasted_iota(jnp.int32, (n_slc, tq), 0)
    cur = (q0 + lax.broadcasted_iota(jnp.int32, (n_slc, tq), 1)) // SEL_BLOCK
    forced = (jj == 0) | (jj == cur) | (jj == cur - 1)
    v = jnp.where(forced, FORCE, jnp.where(jj <= cur, v, -FORCE))
    sub = 8
    ranks = []
    row8 = lax.broadcasted_iota(jnp.int32, (sub, tq), 0)
    for r0 in range(0, n_slc, sub):
        vr = v[r0:r0 + sub]
        cnt = jnp.zeros((sub, tq), F32)
        for j2 in range(n_slc):
            vj = v[j2:j2 + 1, :]
            if j2 < r0:
                ahead = vj >= vr
            elif j2 >= r0 + sub:
                ahead = vj > vr
            else:
                ahead = (vj > vr) | ((vj == vr) & (row8 > j2 - r0))
            cnt = cnt + jnp.where(ahead, 1.0, 0.0)
        ranks.append(cnt)
    rank = jnp.concatenate(ranks, axis=0)
    sel_neg = jnp.where((rank < min(SEL_TOPK, n_slc)) & (jj <= cur), 0.0, NEG)
    q_sel = q_operand(jnp.concatenate([sel_neg, jnp.zeros((HEAD_DIM - n_slc, tq), F32)], axis=0))

    n_grp = s_ref.shape[0]
    gw = nh * tq // n_grp

    def scores(c, g):
        keys = ksel_ref[pl.ds(pl.multiple_of(c * tk, tk), tk), :]
        s = _dot(keys, q_sel[:, g * gw:(g + 1) * gw])
        s_ref[g] = s
        return jnp.max(s, axis=0, keepdims=True)

    def absorb(c, g, s, m_tile, m, l):
        m_new = jnp.maximum(m, m_tile)
        alpha = jnp.exp2(m - m_new)
        p = jnp.exp2(s - m_new)
        l = alpha * l + jnp.sum(p, axis=0, keepdims=True)
        acc_ref[g] = alpha * acc_ref[g] + _dot(vtsel_ref[c], p.astype(BF16))
        return m_new, l

    n_chunks = (q0 + tq + tk - 1) // tk
    acc_ref[...] = jnp.zeros_like(acc_ref)
    first_max = [scores(0, g) for g in range(n_grp)]
    o_win.append(window_softmax(n_sub - 1))

    def sel_step(c, carry):
        out = []
        for g in range(n_grp):
            m, l, m_tile = carry[g]
            s = s_ref[g]
            m_tile_next = scores(c + 1, g)
            out.append(absorb(c, g, s, m_tile, m, l) + (m_tile_next,))
        return tuple(out)

    stats = tuple((jnp.full((1, gw), NEG, F32), jnp.zeros((1, gw), F32), first_max[g]) for g in range(n_grp))
    stats = lax.fori_loop(0, n_chunks - 1, sel_step, stats)
    last = n_chunks - 1
    tri = tri_ref[(q0 - last * tk) // tq]
    tri_g = jnp.concatenate([tri] * (gw // tq), axis=1)
    l_last = []
    for g in range(n_grp):
        s_last = s_ref[g] + tri_g
        l_last.append(absorb(last, g, s_last, jnp.max(s_last, axis=0, keepdims=True), *stats[g][:2])[1])

    gate_t = _sigmoid(gate_ref[...]).T
    z_gate = _silu(z_ref[...])
    partial = []
    for h in range(nh):
        cols = slice(h * tq, (h + 1) * tq)
        o_win_h = jnp.concatenate([ow[:, h * tw:(h + 1) * tw] for ow in o_win], axis=1)
        partial.append(gate_t[0 * nh + h:0 * nh + h + 1, :] * o_cmp[:, cols]
                       + gate_t[2 * nh + h:2 * nh + h + 1, :] * o_win_h)
    o_slc = jnp.concatenate([acc_ref[g] / l_last[g] for g in range(n_grp)], axis=1)
    combs = [partial[h] + gate_t[1 * nh + h:1 * nh + h + 1, :] * o_slc[:, h * tq:(h + 1) * tq] for h in range(nh)]
    for c in range(nh // 2):
        blk = jnp.concatenate([combs[2 * c], combs[2 * c + 1]], axis=0).T
        o_ref[:, c * LANES:(c + 1) * LANES] = (blk * z_gate[:, c * LANES:(c + 1) * LANES]).astype(o_ref.dtype)


def _nsa(pb, pf, kcvc, ovl, tri, band, nb, seq):
    tq = TQ_NSA
    tk = TK_SEL
    nq = seq // tq
    return pl.pallas_call(
        _nsa_kernel,
        grid=(nb, nq),
        in_specs=[
            pl.BlockSpec((tq, 2 * LANES), lambda b, i: (b * nq + i, PB_AQ // 2)),
            pl.BlockSpec((seq, LANES), lambda b, i: (b, PB_SLC)),
            pl.BlockSpec((seq, LANES), lambda b, i: (b, PB_WIN)),
            pl.BlockSpec((None, LANES, LANES), lambda b, i: (b, 0, 0)),
            pl.BlockSpec((tq, LANES), lambda b, i: (b * nq + i, PF_AG)),
            pl.BlockSpec((tq, 2 * LANES), lambda b, i: (b * nq + i, PF_AZ // 2)),
            pl.BlockSpec((LANES, LANES), lambda b, i: (0, 0)),
            pl.BlockSpec(tri.shape, lambda b, i: (0, 0, 0)),
            pl.BlockSpec(band.shape, lambda b, i: (0, 0, 0)),
        ],
        out_specs=pl.BlockSpec((tq, GROUP_W), lambda b, i: (b * nq + i, 0)),
        out_shape=jax.ShapeDtypeStruct((nb * seq, GROUP_W), BF16),
        scratch_shapes=[
            pltpu.VMEM((seq, LANES), BF16),
            pltpu.VMEM((seq // tk, HEAD_DIM, tk), BF16),
            pltpu.VMEM((WIN + seq, LANES), BF16),
            pltpu.VMEM(((WIN + seq) // TW_WIN, HEAD_DIM, TW_WIN), BF16),
            pltpu.VMEM((HEAD_DIM, LANES), BF16),
            pltpu.VMEM((SEL_GROUPS, HEAD_DIM, N_HEADS * tq // SEL_GROUPS), F32),
            pltpu.VMEM((SEL_GROUPS, tk, N_HEADS * tq // SEL_GROUPS), F32),
        ],
        compiler_params=pltpu.CompilerParams(
            dimension_semantics=("arbitrary", "arbitrary"), vmem_limit_bytes=48 * 1024 * 1024),
        name="nsa",
    )(pb, pb, pb, kcvc, pf, pf, ovl, tri, band)


def _dilated_kernel(q_ref, k_ref, v_ref, z_ref, bias_ref, o_ref, vt_ref, acc_ref, s_ref):
    t = q_ref.shape[0]
    seq = k_ref.shape[0]
    n_pairs = N_HEADS // 2
    rows_h = DEN_ROWS + HEAD_DIM
    qi = pl.program_id(1)

    @pl.when(qi == 0)
    def _():
        ones_rows = jnp.where(lax.broadcasted_iota(jnp.int32, (DEN_ROWS, t), 0) == 0, 1.0, 0.0).astype(BF16)
        for j in range(seq // t):
            v_t = v_ref[j * t:(j + 1) * t, :].astype(F32).T.astype(BF16)
            for h in range(N_HEADS):
                vt_ref[j, h * rows_h:h * rows_h + DEN_ROWS] = ones_rows
                vt_ref[j, h * rows_h + DEN_ROWS:(h + 1) * rows_h] = v_t[h * HEAD_DIM:(h + 1) * HEAD_DIM]

    q_t = q_ref[...].astype(F32).T
    upper = lax.broadcasted_iota(jnp.int32, (LANES, t), 0) >= HEAD_DIM
    qs = []
    for c in range(n_pairs):
        blk = q_t[c * LANES:(c + 1) * LANES]
        qs.append(jnp.concatenate([jnp.where(upper, 0.0, blk), jnp.where(upper, blk, 0.0)], axis=1).astype(BF16))
    acc_ref[...] = jnp.zeros_like(acc_ref)

    def scores(j, c):
        k0 = pl.multiple_of(j * t, t)
        bias = bias_ref[qi - j]
        s = _dot(k_ref[pl.ds(k0, t), c * LANES:(c + 1) * LANES], qs[c]) + jnp.concatenate([bias, bias], axis=1)
        s_ref[c] = s
        return jnp.max(s, axis=0, keepdims=True)

    tile_max = tuple(scores(0, c) for c in range(n_pairs))

    def step(j, carry):
        j_next = jnp.minimum(j + 1, qi)
        out = []
        for c in range(n_pairs):
            m, m_tile = carry[c]
            s = s_ref[c]
            m_tile_next = scores(j_next, c)
            m_new = jnp.maximum(m, m_tile)
            alpha = jnp.exp2(m - m_new)
            pb = jnp.exp2(s - m_new).astype(BF16)
            for e in range(2):
                h = 2 * c + e
                pv = _dot(vt_ref[j, h * rows_h:(h + 1) * rows_h, :], pb[:, e * t:(e + 1) * t])
                acc_ref[h] = alpha[:, e * t:(e + 1) * t] * acc_ref[h] + pv
            out.append((m_new, m_tile_next))
        return tuple(out)

    init = tuple((jnp.full((1, 2 * t), NEG, F32), tile_max[c]) for c in range(n_pairs))
    lax.fori_loop(0, qi + 1, step, init)
    for c in range(n_pairs):
        heads = [acc_ref[2 * c + e] for e in range(2)]
        o_pair = jnp.concatenate([a[DEN_ROWS:] / a[0:1] for a in heads], axis=0).T
        zc = z_ref[:, c * LANES:(c + 1) * LANES]
        o_ref[:, c * LANES:(c + 1) * LANES] = (o_pair * _silu(zc)).astype(o_ref.dtype)


def _dilated(pb, pf, bias_tab, nb, seq):
    t = T_DIL
    nq = seq // t
    return pl.pallas_call(
        _dilated_kernel,
        grid=(nb, nq),
        in_specs=[
            pl.BlockSpec((t, GROUP_W), lambda b, i: (b * nq + i, PB_BQ // 2)),
            pl.BlockSpec((seq, GROUP_W), lambda b, i: (b, PB_BK // 2)),
            pl.BlockSpec((seq, GROUP_W), lambda b, i: (b, PB_BV // 2)),
            pl.BlockSpec((t, GROUP_W), lambda b, i: (b * nq + i, PF_BZ // 2)),
            pl.BlockSpec((nq, t, t), lambda b, i: (0, 0, 0)),
        ],
        out_specs=pl.BlockSpec((t, GROUP_W), lambda b, i: (b * nq + i, 0)),
        out_shape=jax.ShapeDtypeStruct((nb * seq, GROUP_W), BF16),
        scratch_shapes=[
            pltpu.VMEM((nq, N_HEADS * (DEN_ROWS + HEAD_DIM), t), BF16),
            pltpu.VMEM((N_HEADS, DEN_ROWS + HEAD_DIM, t), F32),
            pltpu.VMEM((N_HEADS // 2, t, 2 * t), F32),
        ],
        compiler_params=pltpu.CompilerParams(
            dimension_semantics=("arbitrary", "arbitrary"), vmem_limit_bytes=48 * 1024 * 1024),
        name="dilated",
    )(pb, pb, pb, pf, bias_tab)


def _pool_sgu_kernel(c_ref, cprev_ref, cz_ref, du_ref, dv_ref, dz_ref,
                     wpool_ref, pscale_ref, lng_ref, lnb_ref, wsp_ref, bsp_ref, yc_ref, yd_ref):
    ts = c_ref.shape[0]
    i = pl.program_id(1)
    lane = lax.broadcasted_iota(jnp.int32, (ts, GROUP_W), 1)

    cur = c_ref[...]
    prev = jnp.where(i > 0, cprev_ref[...], 0.0)
    ext = jnp.concatenate([prev, cur], axis=0)
    t_head = i * ts + lax.broadcasted_iota(jnp.int32, (POOL_HALO, 1), 0)
    pooled = None
    acc = ext
    width = 1
    for g, w in enumerate(POOL_SIZES):
        while width < w:
            acc = acc + pltpu.roll(acc, width, 0)
            width *= 2
        inv_head = 1.0 / jnp.minimum(t_head + 1, w).astype(F32)
        mean_w = jnp.concatenate([acc[POOL_HALO:2 * POOL_HALO] * inv_head,
                                  acc[2 * POOL_HALO:POOL_HALO + ts] * (1.0 / w)], axis=0)
        pooled = mean_w if pooled is None else jnp.where(lane >= g * HEAD_DIM, mean_w, pooled)
    pooled = pooled - cur
    mixed = _dot(pooled.astype(BF16), wpool_ref[...]) * pscale_ref[...]
    yc_ref[...] = (mixed * _silu(cz_ref[...])).astype(yc_ref.dtype)

    v = dv_ref[...]
    mu = jnp.mean(v, axis=-1, keepdims=True)
    var = jnp.mean(jnp.square(v - mu), axis=-1, keepdims=True)
    vn = ((v - mu) * lax.rsqrt(var + LN_EPS) * lng_ref[...] + lnb_ref[...]).astype(BF16)
    r = lax.broadcasted_iota(jnp.int32, (SG_CHUNK, SG_CHUNK), 0)
    cidx = lax.broadcasted_iota(jnp.int32, (SG_CHUNK, SG_CHUNK), 1)
    w_tril = [jnp.where(cidx <= r, wsp_ref[g], 0.0).astype(BF16) for g in range(N_HEADS)]
    lane_c = lax.broadcasted_iota(jnp.int32, (SG_CHUNK, GROUP_W), 1)
    for ci in range(ts // SG_CHUNK):
        rows = slice(ci * SG_CHUNK, (ci + 1) * SG_CHUNK)
        vc = vn[rows]
        zmix = _dot(w_tril[0], vc)
        for g in range(1, N_HEADS):
            zmix = jnp.where(lane_c >= g * HEAD_DIM, _dot(w_tril[g], vc), zmix)
        zfull = zmix + bsp_ref[...]
        yd_ref[rows, :] = (du_ref[rows, :] * zfull * _silu(dz_ref[rows, :])).astype(yd_ref.dtype)


def _pool_sgu(pf, wpool_bd, pool_scale, ln_g, ln_b, w_sp, bsp_exp, layer, nb, seq):
    ts = TS_POOL
    ns = seq // ts
    halo_blocks = ts // POOL_HALO

    def tile(col):
        return pl.BlockSpec((ts, GROUP_W), lambda b, i: (b * ns + i, col // 2))

    def per_layer(shape):
        nd = len(shape)
        return pl.BlockSpec((None,) + shape, lambda b, i: (layer,) + (0,) * nd)

    return pl.pallas_call(
        _pool_sgu_kernel,
        grid=(nb, ns),
        in_specs=[
            tile(PF_CIN),
            pl.BlockSpec((POOL_HALO, GROUP_W),
                         lambda b, i: (jnp.maximum((b * ns + i) * halo_blocks - 1, 0), PF_CIN // 2)),
            tile(PF_CZ), tile(PF_DU), tile(PF_DV), tile(PF_DZ),
            per_layer((GROUP_W, GROUP_W)),
            per_layer((1, GROUP_W)),
            per_layer((1, GROUP_W)),
            per_layer((1, GROUP_W)),
            per_layer((N_HEADS, SG_CHUNK, SG_CHUNK)),
            per_layer((SG_CHUNK, GROUP_W)),
        ],
        out_specs=[
            pl.BlockSpec((ts, GROUP_W), lambda b, i: (b * ns + i, 0)),
            pl.BlockSpec((ts, GROUP_W), lambda b, i: (b * ns + i, 0)),
        ],
        out_shape=[
            jax.ShapeDtypeStruct((nb * seq, GROUP_W), BF16),
            jax.ShapeDtypeStruct((nb * seq, GROUP_W), BF16),
        ],
        compiler_params=pltpu.CompilerParams(
            dimension_semantics=("arbitrary", "arbitrary"), vmem_limit_bytes=48 * 1024 * 1024),
        name="pool_sgu",
    )(pf, pf, pf, pf, pf, pf, wpool_bd, pool_scale, ln_g, ln_b, w_sp, bsp_exp)


def _project_out(ya_ref, yb_ref, yc_ref, yd_ref, w_ref, x_ref, g_ref):
    y = jnp.concatenate([ya_ref[...], yb_ref[...], yc_ref[...], yd_ref[...]], axis=1)
    out = _dot(y, w_ref[...])
    inv = lax.rsqrt(jnp.mean(out * out, axis=-1, keepdims=True) + RMS_EPS)
    return x_ref[...] + out * inv * g_ref[...]


def _out_proj_kernel(ya_ref, yb_ref, yc_ref, yd_ref, w_ref, x_ref, g_ref, o_ref):
    o_ref[...] = _project_out(ya_ref, yb_ref, yc_ref, yd_ref, w_ref, x_ref, g_ref)


def _out_in_proj_kernel(ya_ref, yb_ref, yc_ref, yd_ref, wo_ref, x_ref, gpost_ref,
                        gpre_ref, wa_ref, wg_ref, wb_ref, o_ref, pb_ref, pf_ref):
    half = o_ref.shape[0] // 2
    x_new = []
    for r in range(2):
        rows = pl.ds(r * half, half)
        x_new.append(_project_out(ya_ref.at[rows], yb_ref.at[rows], yc_ref.at[rows], yd_ref.at[rows],
                                  wo_ref, x_ref.at[rows], gpost_ref))
        o_ref[rows, :] = x_new[r]
    for r in range(2):
        rows = pl.ds(r * half, half)
        _project_in(x_new[r], gpre_ref, wa_ref, wg_ref, wb_ref, pb_ref.at[rows], pf_ref.at[rows])


def _out_in_proj(ya, yb, yc, yd, w_out, x2, g_post, g_pre, w_a, w_g, w_b, layer):
    m, d = x2.shape
    tm = TM_OUT
    ytile = pl.BlockSpec((tm, GROUP_W), lambda i: (i, 0))
    nxt = layer + 1
    return pl.pallas_call(
        _out_in_proj_kernel,
        grid=(m // tm,),
        in_specs=[
            ytile, ytile, ytile, ytile,
            pl.BlockSpec((None, 4 * GROUP_W, d), lambda i: (layer, 0, 0)),
            pl.BlockSpec((tm, d), lambda i: (i, 0)),
            pl.BlockSpec((None, 1, d), lambda i: (layer, 0, 0)),
            pl.BlockSpec((None, 1, d), lambda i: (nxt, 0, 0)),
            pl.BlockSpec((None, d, w_a.shape[2]), lambda i: (nxt, 0, 0)),
            pl.BlockSpec((None, d, w_g.shape[2]), lambda i: (nxt, 0, 0)),
            pl.BlockSpec((None, d, w_b.shape[2]), lambda i: (nxt, 0, 0)),
        ],
        out_specs=[
            pl.BlockSpec((tm, d), lambda i: (i, 0)),
            pl.BlockSpec((tm, PB_COLS), lambda i: (i, 0)),
            pl.BlockSpec((tm, PF_COLS), lambda i: (i, 0)),
        ],
        out_shape=[
            jax.ShapeDtypeStruct((m, d), F32),
            jax.ShapeDtypeStruct((m, PB_COLS), BF16),
            jax.ShapeDtypeStruct((m, PF_COLS), F32),
        ],
        compiler_params=pltpu.CompilerParams(
            dimension_semantics=("arbitrary",), vmem_limit_bytes=56 * 1024 * 1024),
        name="out_in_proj",
    )(ya, yb, yc, yd, w_out, x2, g_post, g_pre, w_a, w_g, w_b)


def _out_proj(ya, yb, yc, yd, w_out, x2, g_post, layer):
    m, d = x2.shape
    tm = TM_OUT
    ytile = pl.BlockSpec((tm, GROUP_W), lambda i: (i, 0))
    return pl.pallas_call(
        _out_proj_kernel,
        grid=(m // tm,),
        in_specs=[
            ytile, ytile, ytile, ytile,
            pl.BlockSpec((None, 4 * GROUP_W, d), lambda i: (layer, 0, 0)),
            pl.BlockSpec((tm, d), lambda i: (i, 0)),
            pl.BlockSpec((None, 1, d), lambda i: (layer, 0, 0)),
        ],
        out_specs=pl.BlockSpec((tm, d), lambda i: (i, 0)),
        out_shape=jax.ShapeDtypeStruct((m, d), F32),
        compiler_params=pltpu.CompilerParams(
            dimension_semantics=("arbitrary",), vmem_limit_bytes=48 * 1024 * 1024),
        name="out_proj",
    )(ya, yb, yc, yd, w_out, x2, g_post)


def _overlap_t(seq):
    n_cmp = (seq - CMP_LEN) // CMP_STRIDE + 1
    n_slc = seq // SEL_BLOCK
    cs = np.arange(n_cmp) * CMP_STRIDE
    ss = np.arange(n_slc) * SEL_BLOCK
    ov = (cs[None, :] < ss[:, None] + SEL_BLOCK) & (cs[None, :] + CMP_LEN > ss[:, None])
    out = np.zeros((LANES, LANES), np.float32)
    out[:n_slc, :n_cmp] = ov
    return out


def _causal_tri():
    a = np.arange(TK_SEL)[:, None]
    b = np.arange(TQ_NSA)[None, :]
    return np.stack([np.where(a - o * TQ_NSA <= b, 0.0, NEG) for o in range(TK_SEL // TQ_NSA)]).astype(np.float32)


def _window_band():
    a = np.arange(TW_WIN)[:, None]
    b = np.arange(TW_WIN)[None, :]
    return np.stack([np.where(a > b, 0.0, NEG), np.where(a <= b, 0.0, NEG)]).astype(np.float32)


def _dilated_log_multiplicity(seq):
    t = T_DIL
    d0 = np.arange(t)[:, None] - np.arange(t)[None, :]
    tabs = []
    for delta in range(seq // t):
        d = d0 + delta * t
        mult = np.zeros_like(d)
        for window, dil in DILATED_PAIRS:
            mult += (d >= 0) & (d % dil == 0) & (d // dil <= window // dil) & (d // dil <= seq // dil - 1)
        tabs.append(np.where(mult > 0, np.log2(np.maximum(mult, 1)), NEG).T)
    return np.stack(tabs).astype(np.float32)


def _split_w_in(w_in):
    gw = GROUP_W
    n_a = gw + 6 * HEAD_DIM
    n_b = 10 * gw
    assert w_in.shape[-1] == n_a + N_GATES + n_b, w_in.shape
    scale_a = np.ones((n_a,), np.float32)
    scale_a[:gw] = Q_SCALE
    scale_b = np.ones((n_b,), np.float32)
    scale_b[gw:2 * gw] = Q_SCALE
    w_a = (w_in[..., :n_a] * scale_a).astype(BF16)
    w_g = jnp.pad(w_in[..., n_a:n_a + N_GATES], ((0, 0), (0, 0), (0, LANES - N_GATES))).astype(BF16)
    w_b = (w_in[..., n_a + N_GATES:] * scale_b).astype(BF16)
    return w_a, w_g, w_b


def kernel(x, g_pre, w_in, pe_cmp, w_cmp1, w_cmp2, w_pool, pool_scale, sg_ln_g, sg_ln_b, w_sp, b_sp, w_out, g_post):
    nb, seq, d = x.shape
    depth = w_in.shape[0]
    assert seq % TK_SEL == 0 and seq % T_DIL == 0 and seq % TS_POOL == 0 and (nb * seq) % TM_PROJ == 0
    assert TQ_NSA % TW_WIN == 0 and TK_SEL % TQ_NSA == 0 and WIN % TW_WIN == 0 and TW_WIN % LANES == 0
    assert seq // SEL_BLOCK <= LANES - HEAD_DIM and (seq - CMP_LEN) // CMP_STRIDE + 1 < LANES
    assert seq % CMP_STRIDE == 0 and CMP_LEN == 2 * CMP_STRIDE

    w_a, w_g, w_b = _split_w_in(w_in)
    w_out_b = w_out.astype(BF16)
    w1_l = w_cmp1.reshape(depth, 2, CMP_LEN, HEAD_DIM, CMP_HIDDEN)
    w2_b = w_cmp2.astype(BF16)
    pe_rows = jnp.concatenate([pe_cmp[:, 0], pe_cmp[:, 1]], axis=-1)
    eye = jnp.eye(N_HEADS, dtype=w_pool.dtype)
    wpool_bd = jnp.einsum('lgcd,gh->lgchd', w_pool, eye).reshape(depth, GROUP_W, GROUP_W).astype(BF16)
    bsp_exp = jnp.repeat(jnp.swapaxes(b_sp, 1, 2), HEAD_DIM, axis=2)
    g_pre3 = g_pre.reshape(depth, 1, d)
    g_post3 = g_post.reshape(depth, 1, d)
    pscale3 = pool_scale.reshape(depth, 1, GROUP_W)
    lng3 = sg_ln_g.reshape(depth, 1, GROUP_W)
    lnb3 = sg_ln_b.reshape(depth, 1, GROUP_W)
    ovl_t = jnp.asarray(_overlap_t(seq), BF16)
    tri = jnp.asarray(_causal_tri())
    band = jnp.asarray(_window_band())
    dil_bias = jnp.asarray(_dilated_log_multiplicity(seq))

    x2 = x.reshape(nb * seq, d)
    pb, pf = _in_proj(x2, g_pre3, w_a, w_g, w_b, 0)
    for layer in range(depth):
        kcvc = _compress(pf, pe_rows, w1_l, w2_b, layer, nb, seq)
        ya = _nsa(pb, pf, kcvc, ovl_t, tri, band, nb, seq)
        yb = _dilated(pb, pf, dil_bias, nb, seq)
        yc, yd = _pool_sgu(pf, wpool_bd, pscale3, lng3, lnb3, w_sp, bsp_exp, layer, nb, seq)
        if layer + 1 < depth:
            x2, pb, pf = _out_in_proj(ya, yb, yc, yd, w_out_b, x2, g_post3, g_pre3, w_a, w_g, w_b, layer)
        else:
            x2 = _out_proj(ya, yb, yc, yd, w_out_b, x2, g_post3, layer)
    return x2.reshape(nb, seq, d)
```

```python
import functools

import numpy as np
import jax
import jax.numpy as jnp
from jax import lax
from jax.experimental import pallas as pl
from jax.experimental.pallas import tpu as pltpu

F32 = jnp.float32
BF16 = jnp.bfloat16

HEAD_DIM = 64
N_HEADS = 4
GROUP_W = N_HEADS * HEAD_DIM
CMP_LEN = 32
CMP_STRIDE = 16
CMP_HIDDEN = 256
SEL_BLOCK = 64
SEL_TOPK = 16
WIN = 512
FORCE = 1e4
DILATED_PAIRS = ((128, 1), (512, 4), (2048, 16))
POOL_SIZES = (2, 4, 8, 16)
SG_CHUNK = 128
RMS_EPS = 1e-6
LN_EPS = 1e-5
NEG = -1e30

LANES = 128
N_GATES = 3 * N_HEADS

PB_AQ, PB_SLC, PB_WIN, PB_BQ, PB_BK, PB_BV = 0, 2, 3, 4, 6, 8
PB_COLS = 10 * LANES
PF_CMP, PF_AG, PF_AZ, PF_BZ, PF_CIN, PF_CZ, PF_DU, PF_DV, PF_DZ = 0, 1, 2, 4, 6, 8, 10, 12, 14
PF_COLS = 16 * LANES

TM_PROJ = 256
TM_OUT = 512
TQ_NSA = 256
TK_SEL = 256
T_DIL = 256
TS_POOL = 512
POOL_HALO = 16
TW_WIN = 128
SEL_GROUPS = 2
DEN_ROWS = 16

Q_SCALE = float(np.log2(np.e)) * HEAD_DIM ** -0.5


def _sigmoid(x):
    return 0.5 * jnp.tanh(0.5 * x) + 0.5


def _silu(x):
    return x * _sigmoid(x)


def _dot_nt(a, b):
    return lax.dot_general(a, b, (((1,), (1,)), ((), ())), preferred_element_type=F32)


def _dot(a, b):
    return jnp.dot(a, b, preferred_element_type=F32)


def _in_proj_kernel(x_ref, g_ref, wa_ref, wg_ref, wb_ref, pb_ref, pf_ref):
    _project_in(x_ref[...], g_ref, wa_ref, wg_ref, wb_ref, pb_ref, pf_ref)


def _project_in(x, g_ref, wa_ref, wg_ref, wb_ref, pb_ref, pf_ref):
    inv = lax.rsqrt(jnp.mean(x * x, axis=-1, keepdims=True) + RMS_EPS)
    h = (x * inv * g_ref[...]).astype(BF16)
    cw = 2 * LANES

    def put(dst, c, w):
        dst[:, c * cw:(c + 1) * cw] = _dot(h, w).astype(dst.dtype)

    put(pb_ref, PB_AQ // 2, wa_ref[:, 0:cw])
    put(pb_ref, PB_SLC // 2, wa_ref[:, cw + LANES:2 * cw + LANES])
    for c in range(3):
        put(pb_ref, PB_BQ // 2 + c, wb_ref[:, (1 + c) * cw:(2 + c) * cw])
    put(pf_ref, PF_CMP // 2, jnp.concatenate([wa_ref[:, cw:cw + LANES], wg_ref[...]], axis=1))
    put(pf_ref, PF_AZ // 2, wb_ref[:, 0:cw])
    for c in range(6):
        put(pf_ref, PF_BZ // 2 + c, wb_ref[:, (4 + c) * cw:(5 + c) * cw])


def _in_proj(x2, g_pre, w_a, w_g, w_b, layer):
    m, d = x2.shape
    return pl.pallas_call(
        _in_proj_kernel,
        grid=(m // TM_PROJ,),
        in_specs=[
            pl.BlockSpec((TM_PROJ, d), lambda i: (i, 0)),
            pl.BlockSpec((None, 1, d), lambda i: (layer, 0, 0)),
            pl.BlockSpec((None, d, w_a.shape[2]), lambda i: (layer, 0, 0)),
            pl.BlockSpec((None, d, w_g.shape[2]), lambda i: (layer, 0, 0)),
            pl.BlockSpec((None, d, w_b.shape[2]), lambda i: (layer, 0, 0)),
        ],
        out_specs=[
            pl.BlockSpec((TM_PROJ, PB_COLS), lambda i: (i, 0)),
            pl.BlockSpec((TM_PROJ, PF_COLS), lambda i: (i, 0)),
        ],
        out_shape=[
            jax.ShapeDtypeStruct((m, PB_COLS), BF16),
            jax.ShapeDtypeStruct((m, PF_COLS), F32),
        ],
        compiler_params=pltpu.CompilerParams(
            dimension_semantics=("arbitrary",), vmem_limit_bytes=48 * 1024 * 1024),
        name="in_proj",
    )(x2, g_pre, w_a, w_g, w_b)


def _gelu_tanh(x):
    return 0.5 * x * (1.0 + jnp.tanh(np.sqrt(2.0 / np.pi).astype(np.float32) * (x + 0.044715 * (x * x * x))))


def _compress_kernel(x_ref, pe_ref, w1_ref, w2_ref, o_ref):
    n_grp = x_ref.shape[0] // CMP_STRIDE
    ng = o_ref.shape[1]
    lo = [None, None]
    hi = [None, None]
    for l0 in range(CMP_STRIDE):
        rows = x_ref[pl.ds(l0, n_grp, stride=CMP_STRIDE), :]
        r_lo = (rows + pe_ref[l0:l0 + 1, :]).astype(BF16)
        r_hi = (rows + pe_ref[CMP_STRIDE + l0:CMP_STRIDE + l0 + 1, :]).astype(BF16)
        for j in range(2):
            cols = slice(j * HEAD_DIM, (j + 1) * HEAD_DIM)
            d_lo = _dot(r_lo[:, cols], w1_ref[j, l0].astype(BF16))
            d_hi = _dot(r_hi[:, cols], w1_ref[j, CMP_STRIDE + l0].astype(BF16))
            lo[j] = d_lo if lo[j] is None else lo[j] + d_lo
            hi[j] = d_hi if hi[j] is None else hi[j] + d_hi
    row = lax.broadcasted_iota(jnp.int32, (n_grp, HEAD_DIM), 0) % ng
    for j in range(2):
        hid = _gelu_tanh(lo[j] + pltpu.roll(hi[j], n_grp - 1, 0)).astype(BF16)
        comp = jnp.where(row < ng - 1, _dot(hid, w2_ref[j]), 0.0)
        o_ref[:, :, j * HEAD_DIM:(j + 1) * HEAD_DIM] = comp.reshape(o_ref.shape[0], ng, HEAD_DIM).astype(BF16)


def _compress(pf, pe_rows, w1, w2, layer, nb, seq):
    ng = seq // CMP_STRIDE
    return pl.pallas_call(
        _compress_kernel,
        grid=(1,),
        in_specs=[
            pl.BlockSpec((nb * seq, LANES), lambda i: (0, PF_CMP)),
            pl.BlockSpec((None, CMP_LEN, LANES), lambda i: (layer, 0, 0)),
            pl.BlockSpec((None, 2, CMP_LEN, HEAD_DIM, CMP_HIDDEN), lambda i: (layer, 0, 0, 0, 0)),
            pl.BlockSpec((None, 2, CMP_HIDDEN, HEAD_DIM), lambda i: (layer, 0, 0, 0)),
        ],
        out_specs=pl.BlockSpec((nb, ng, LANES), lambda i: (0, 0, 0)),
        out_shape=jax.ShapeDtypeStruct((nb, ng, LANES), BF16),
        compiler_params=pltpu.CompilerParams(
            dimension_semantics=("arbitrary",), vmem_limit_bytes=48 * 1024 * 1024),
        name="compress",
    )(pf, pe_rows, w1, w2)


def _nsa_kernel(q_ref, slc_ref, win_ref, kc_ref, gate_ref, z_ref, ovl_ref, tri_ref, band_ref, o_ref,
                ksel_ref, vtsel_ref, kwin_ref, vtwin_ref, kct_ref, acc_ref, s_ref):
    tq = q_ref.shape[0]
    seq = slc_ref.shape[0]
    tk = s_ref.shape[1]
    tw = band_ref.shape[1]
    nh = N_HEADS
    n_slc = seq // SEL_BLOCK
    qi = pl.program_id(1)
    q0 = qi * tq

    @pl.when(qi == 0)
    def _():
        lane_k = lax.broadcasted_iota(jnp.int32, (tk, LANES), 1)
        for j in range(seq // tk):
            rows = slice(j * tk, (j + 1) * tk)
            blk = slc_ref[rows, :]
            key_blk = (j * tk + lax.broadcasted_iota(jnp.int32, (tk, LANES), 0)) // SEL_BLOCK
            onehot = jnp.where(lane_k - HEAD_DIM == key_blk, 1.0, 0.0).astype(BF16)
            ksel_ref[rows, :] = jnp.where(lane_k < HEAD_DIM, blk, onehot)
            vtsel_ref[j] = blk.astype(F32).T[HEAD_DIM:].astype(BF16)
        lane_w = lax.broadcasted_iota(jnp.int32, (WIN, LANES), 1)
        kwin_ref[0:WIN, :] = jnp.where(lane_w == HEAD_DIM, 1.0, 0.0).astype(BF16)
        for j in range(WIN // tw):
            vtwin_ref[j] = jnp.zeros((HEAD_DIM, tw), BF16)
        lane_t = lax.broadcasted_iota(jnp.int32, (tw, LANES), 1)
        for j in range(seq // tw):
            blk = win_ref[j * tw:(j + 1) * tw, :]
            kwin_ref[WIN + j * tw:WIN + (j + 1) * tw, :] = jnp.where(lane_t < HEAD_DIM, blk, jnp.zeros_like(blk))
            vtwin_ref[WIN // tw + j] = blk.astype(F32).T[HEAD_DIM:].astype(BF16)
        kct_ref[...] = kc_ref[...].astype(F32).T[HEAD_DIM:].astype(BF16)

    q_t = q_ref[...].astype(F32).T

    def q_operand(extra):
        return jnp.concatenate([jnp.concatenate([q_t[h * HEAD_DIM:(h + 1) * HEAD_DIM], extra], axis=0)
                                for h in range(nh)], axis=1).astype(BF16)

    def tile4(x):
        return jnp.concatenate([x] * nh, axis=1)


    span = WIN + tw
    extra_row = lax.broadcasted_iota(jnp.int32, (HEAD_DIM, tq), 0)
    q_win = q_operand(jnp.where(extra_row == 0, NEG, 0.0))
    n_sub = tq // tw

    def window_scores(u):
        q_sub = jnp.concatenate([q_win[:, h * tq + u * tw:h * tq + (u + 1) * tw] for h in range(nh)], axis=1)
        start = pl.multiple_of(q0 + u * tw, tw)
        return _dot(kwin_ref[pl.ds(start, span), :], q_sub)

    kc = kc_ref[...]
    s_c = _dot(kc, q_operand(jnp.zeros((HEAD_DIM, tq), F32)))
    s_win = [window_scores(0)]

    def window_softmax(u):
        edge = [jnp.concatenate([band_ref[e]] * nh, axis=1) for e in range(2)]
        s_w = s_win[u]
        s_w = jnp.concatenate([s_w[0:tw] + edge[0], s_w[tw:span - tw], s_w[span - tw:] + edge[1]], axis=0)
        e_w = jnp.exp2(s_w - jnp.max(s_w, axis=0, keepdims=True))
        t0 = qi * n_sub + u
        vt_w = jnp.concatenate([vtwin_ref[t0 + d] for d in range(span // tw)], axis=1)
        return _dot(vt_w, e_w.astype(BF16)) / jnp.sum(e_w, axis=0, keepdims=True)

    n_cmp = (seq - CMP_LEN) // CMP_STRIDE + 1
    n_idx = lax.broadcasted_iota(jnp.int32, (LANES, tq), 0)
    t_idx = q0 + lax.broadcasted_iota(jnp.int32, (LANES, tq), 1)
    valid_c = tile4((n_idx * CMP_STRIDE + (CMP_LEN - 1) <= t_idx) & (n_idx < n_cmp))
    s_c = jnp.where(valid_c, s_c, NEG)
    e_c = jnp.exp2(s_c - jnp.max(s_c, axis=0, keepdims=True))
    p_c = jnp.where(valid_c, e_c / jnp.sum(e_c, axis=0, keepdims=True), 0.0)
    o_cmp = _dot(kct_ref[...], p_c.astype(BF16))

    psum = p_c[:, 0:tq]
    for h in range(1, nh):
        psum = psum + p_c[:, h * tq:(h + 1) * tq]
    p_hi = psum.astype(BF16)
    r1 = psum - p_hi.astype(F32)
    p_mid = r1.astype(BF16)
    p_lo = (r1 - p_mid.astype(F32)).astype(BF16)
    ovl = ovl_ref[...]
    imp = _dot(ovl, p_hi) + _dot(ovl, p_mid) + _dot(ovl, p_lo)

    o_win = []
    for u in range(n_sub - 1):
        s_win.append(window_scores(u + 1))
        o_win.append(window_softmax(u))

    v = imp[0:n_slc]
    jj = lax.broadcasted_iota(jnp.int32, (n_slc, tq), 0)
    cur = (q0 + lax.broadcasted_iota(jnp.int32, (n_slc, tq), 1)) // SEL_BLOCK
    forced = (jj == 0) | (jj == cur) | (jj == cur - 1)
    v = jnp.where(forced, FORCE, jnp.where(jj <= cur, v, -FORCE))
    sub = 8
    ranks = []
    row8 = lax.broadcasted_iota(jnp.int32, (sub, tq), 0)
    for r0 in range(0, n_slc, sub):
        vr = v[r0:r0 + sub]
        cnt = jnp.zeros((sub, tq), F32)
        for j2 in range(n_slc):
            vj = v[j2:j2 + 1, :]
            if j2 < r0:
                ahead = vj >= vr
            elif j2 >= r0 + sub:
                ahead = vj > vr
            else:
                ahead = (vj > vr) | ((vj == vr) & (row8 > j2 - r0))
            cnt = cnt + jnp.where(ahead, 1.0, 0.0)
        ranks.append(cnt)
    rank = jnp.concatenate(ranks, axis=0)
    sel_neg = jnp.where((rank < min(SEL_TOPK, n_slc)) & (jj <= cur), 0.0, NEG)
    q_sel = q_operand(jnp.concatenate([sel_neg, jnp.zeros((HEAD_DIM - n_slc, tq), F32)], axis=0))

    n_grp = s_ref.shape[0]
    gw = nh * tq // n_grp

    def scores(c, g):
        keys = ksel_ref[pl.ds(pl.multiple_of(c * tk, tk), tk), :]
        s = _dot(keys, q_sel[:, g * gw:(g + 1) * gw])
        s_ref[g] = s
        return jnp.max(s, axis=0, keepdims=True)

    def absorb(c, g, s, m_tile, m, l):
        m_new = jnp.maximum(m, m_tile)
        alpha = jnp.exp2(m - m_new)
        p = jnp.exp2(s - m_new)
        l = alpha * l + jnp.sum(p, axis=0, keepdims=True)
        acc_ref[g] = alpha * acc_ref[g] + _dot(vtsel_ref[c], p.astype(BF16))
        return m_new, l

    n_chunks = (q0 + tq + tk - 1) // tk
    acc_ref[...] = jnp.zeros_like(acc_ref)
    first_max = [scores(0, g) for g in range(n_grp)]
    o_win.append(window_softmax(n_sub - 1))

    def sel_step(c, carry):
        out = []
        for g in range(n_grp):
            m, l, m_tile = carry[g]
            s = s_ref[g]
            m_tile_next = scores(c + 1, g)
            out.append(absorb(c, g, s, m_tile, m, l) + (m_tile_next,))
        return tuple(out)

    stats = tuple((jnp.full((1, gw), NEG, F32), jnp.zeros((1, gw), F32), first_max[g]) for g in range(n_grp))
    stats = lax.fori_loop(0, n_chunks - 1, sel_step, stats)
    last = n_chunks - 1
    tri = tri_ref[(q0 - last * tk) // tq]
    tri_g = jnp.concatenate([tri] * (gw // tq), axis=1)
    l_last = []
    for g in range(n_grp):
        s_last = s_ref[g] + tri_g
        l_last.append(absorb(last, g, s_last, jnp.max(s_last, axis=0, keepdims=True), *stats[g][:2])[1])

    gate_t = _sigmoid(gate_ref[...]).T
    z_gate = _silu(z_ref[...])
    partial = []
    for h in range(nh):
        cols = slice(h * tq, (h + 1) * tq)
        o_win_h = jnp.concatenate([ow[:, h * tw:(h + 1) * tw] for ow in o_win], axis=1)
        partial.append(gate_t[0 * nh + h:0 * nh + h + 1, :] * o_cmp[:, cols]
                       + gate_t[2 * nh + h:2 * nh + h + 1, :] * o_win_h)
    o_slc = jnp.concatenate([acc_ref[g] / l_last[g] for g in range(n_grp)], axis=1)
    combs = [partial[h] + gate_t[1 * nh + h:1 * nh + h + 1, :] * o_slc[:, h * tq:(h + 1) * tq] for h in range(nh)]
    for c in range(nh // 2):
        blk = jnp.concatenate([combs[2 * c], combs[2 * c + 1]], axis=0).T
        o_ref[:, c * LANES:(c + 1) * LANES] = (blk * z_gate[:, c * LANES:(c + 1) * LANES]).astype(o_ref.dtype)


def _nsa(pb, pf, kcvc, ovl, tri, band, nb, seq):
    tq = TQ_NSA
    tk = TK_SEL
    nq = seq // tq
    return pl.pallas_call(
        _nsa_kernel,
        grid=(nb, nq),
        in_specs=[
            pl.BlockSpec((tq, 2 * LANES), lambda b, i: (b * nq + i, PB_AQ // 2)),
            pl.BlockSpec((seq, LANES), lambda b, i: (b, PB_SLC)),
            pl.BlockSpec((seq, LANES), lambda b, i: (b, PB_WIN)),
            pl.BlockSpec((None, LANES, LANES), lambda b, i: (b, 0, 0)),
            pl.BlockSpec((tq, LANES), lambda b, i: (b * nq + i, PF_AG)),
            pl.BlockSpec((tq, 2 * LANES), lambda b, i: (b * nq + i, PF_AZ // 2)),
            pl.BlockSpec((LANES, LANES), lambda b, i: (0, 0)),
            pl.BlockSpec(tri.shape, lambda b, i: (0, 0, 0)),
            pl.BlockSpec(band.shape, lambda b, i: (0, 0, 0)),
        ],
        out_specs=pl.BlockSpec((tq, GROUP_W), lambda b, i: (b * nq + i, 0)),
        out_shape=jax.ShapeDtypeStruct((nb * seq, GROUP_W), BF16),
        scratch_shapes=[
            pltpu.VMEM((seq, LANES), BF16),
            pltpu.VMEM((seq // tk, HEAD_DIM, tk), BF16),
            pltpu.VMEM((WIN + seq, LANES), BF16),
            pltpu.VMEM(((WIN + seq) // TW_WIN, HEAD_DIM, TW_WIN), BF16),
            pltpu.VMEM((HEAD_DIM, LANES), BF16),
            pltpu.VMEM((SEL_GROUPS, HEAD_DIM, N_HEADS * tq // SEL_GROUPS), F32),
            pltpu.VMEM((SEL_GROUPS, tk, N_HEADS * tq // SEL_GROUPS), F32),
        ],
        compiler_params=pltpu.CompilerParams(
            dimension_semantics=("arbitrary", "arbitrary"), vmem_limit_bytes=48 * 1024 * 1024),
        name="nsa",
    )(pb, pb, pb, kcvc, pf, pf, ovl, tri, band)


def _dilated_kernel(q_ref, qnext_ref, k_ref, v_ref, z_ref, bias_ref, o_ref, vt_ref, acc_ref, s_ref, mt_ref):
    t = q_ref.shape[0]
    seq = k_ref.shape[0]
    n_pairs = N_HEADS // 2
    rows_h = DEN_ROWS + HEAD_DIM
    qi = pl.program_id(1)

    @pl.when(qi == 0)
    def _():
        ones_rows = jnp.where(lax.broadcasted_iota(jnp.int32, (DEN_ROWS, t), 0) == 0, 1.0, 0.0).astype(BF16)
        for j in range(seq // t):
            v_t = v_ref[j * t:(j + 1) * t, :].astype(F32).T.astype(BF16)
            for h in range(N_HEADS):
                vt_ref[j, h * rows_h:h * rows_h + DEN_ROWS] = ones_rows
                vt_ref[j, h * rows_h + DEN_ROWS:(h + 1) * rows_h] = v_t[h * HEAD_DIM:(h + 1) * HEAD_DIM]

    upper = lax.broadcasted_iota(jnp.int32, (LANES, t), 0) >= HEAD_DIM

    def query_operands(ref):
        q_t = ref[...].astype(F32).T
        ops = []
        for c in range(n_pairs):
            blk = q_t[c * LANES:(c + 1) * LANES]
            ops.append(jnp.concatenate([jnp.where(upper, 0.0, blk), jnp.where(upper, blk, 0.0)], axis=1).astype(BF16))
        return ops

    qs = query_operands(q_ref)
    qs_next = query_operands(qnext_ref)
    acc_ref[...] = jnp.zeros_like(acc_ref)

    def scores(j, c, q_op, q_tile):
        k0 = pl.multiple_of(j * t, t)
        bias = bias_ref[jnp.minimum(q_tile - j, bias_ref.shape[0] - 1)]
        s = _dot(k_ref[pl.ds(k0, t), c * LANES:(c + 1) * LANES], q_op) + jnp.concatenate([bias, bias], axis=1)
        s_ref[c] = s
        return jnp.max(s, axis=0, keepdims=True)

    @pl.when(qi == 0)
    def _():
        for c in range(n_pairs):
            mt_ref[c] = scores(0, c, qs[c], qi)

    def step(j, carry):
        is_last = j == qi
        j_next = jnp.where(is_last, 0, j + 1)
        q_tile = jnp.where(is_last, qi + 1, qi)
        out = []
        for c in range(n_pairs):
            m, m_tile = carry[c]
            s = s_ref[c]
            m_tile_next = scores(j_next, c, jnp.where(is_last, qs_next[c], qs[c]), q_tile)
            m_new = jnp.maximum(m, m_tile)
            alpha = jnp.exp2(m - m_new)
            pb = jnp.exp2(s - m_new).astype(BF16)
            for e in range(2):
                h = 2 * c + e
                pv = _dot(vt_ref[j, h * rows_h:(h + 1) * rows_h, :], pb[:, e * t:(e + 1) * t])
                acc_ref[h] = alpha[:, e * t:(e + 1) * t] * acc_ref[h] + pv
            out.append((m_new, m_tile_next))
        return tuple(out)

    init = tuple((jnp.full((1, 2 * t), NEG, F32), mt_ref[c]) for c in range(n_pairs))
    final = lax.fori_loop(0, qi + 1, step, init)
    for c in range(n_pairs):
        mt_ref[c] = final[c][1]
    for c in range(n_pairs):
        heads = [acc_ref[2 * c + e] for e in range(2)]
        o_pair = jnp.concatenate([a[DEN_ROWS:] / a[0:1] for a in heads], axis=0).T
        zc = z_ref[:, c * LANES:(c + 1) * LANES]
        o_ref[:, c * LANES:(c + 1) * LANES] = (o_pair * _silu(zc)).astype(o_ref.dtype)


def _dilated(pb, pf, bias_tab, nb, seq):
    t = T_DIL
    nq = seq // t
    return pl.pallas_call(
        _dilated_kernel,
        grid=(nb, nq),
        in_specs=[
            pl.BlockSpec((t, GROUP_W), lambda b, i: (b * nq + i, PB_BQ // 2)),
            pl.BlockSpec((t, GROUP_W), lambda b, i: (b * nq + jnp.minimum(i + 1, nq - 1), PB_BQ // 2)),
            pl.BlockSpec((seq, GROUP_W), lambda b, i: (b, PB_BK // 2)),
            pl.BlockSpec((seq, GROUP_W), lambda b, i: (b, PB_BV // 2)),
            pl.BlockSpec((t, GROUP_W), lambda b, i: (b * nq + i, PF_BZ // 2)),
            pl.BlockSpec((nq, t, t), lambda b, i: (0, 0, 0)),
        ],
        out_specs=pl.BlockSpec((t, GROUP_W), lambda b, i: (b * nq + i, 0)),
        out_shape=jax.ShapeDtypeStruct((nb * seq, GROUP_W), BF16),
        scratch_shapes=[
            pltpu.VMEM((nq, N_HEADS * (DEN_ROWS + HEAD_DIM), t), BF16),
            pltpu.VMEM((N_HEADS, DEN_ROWS + HEAD_DIM, t), F32),
            pltpu.VMEM((N_HEADS // 2, t, 2 * t), F32),
            pltpu.VMEM((N_HEADS // 2, 1, 2 * t), F32),
        ],
        compiler_params=pltpu.CompilerParams(
            dimension_semantics=("arbitrary", "arbitrary"), vmem_limit_bytes=48 * 1024 * 1024),
        name="dilated",
    )(pb, pb, pb, pb, pf, bias_tab)


def _pool_sgu_kernel(c_ref, cprev_ref, cz_ref, du_ref, dv_ref, dz_ref,
                     wpool_ref, pscale_ref, lng_ref, lnb_ref, wsp_ref, bsp_ref, yc_ref, yd_ref):
    ts = c_ref.shape[0]
    i = pl.program_id(1)
    lane = lax.broadcasted_iota(jnp.int32, (ts, GROUP_W), 1)

    cur = c_ref[...]
    prev = jnp.where(i > 0, cprev_ref[...], 0.0)
    ext = jnp.concatenate([prev, cur], axis=0)
    t_head = i * ts + lax.broadcasted_iota(jnp.int32, (POOL_HALO, 1), 0)
    pooled = None
    acc = ext
    width = 1
    for g, w in enumerate(POOL_SIZES):
        while width < w:
            acc = acc + pltpu.roll(acc, width, 0)
            width *= 2
        inv_head = 1.0 / jnp.minimum(t_head + 1, w).astype(F32)
        mean_w = jnp.concatenate([acc[POOL_HALO:2 * POOL_HALO] * inv_head,
                                  acc[2 * POOL_HALO:POOL_HALO + ts] * (1.0 / w)], axis=0)
        pooled = mean_w if pooled is None else jnp.where(lane >= g * HEAD_DIM, mean_w, pooled)
    pooled = pooled - cur
    mixed = _dot(pooled.astype(BF16), wpool_ref[...]) * pscale_ref[...]
    yc_ref[...] = (mixed * _silu(cz_ref[...])).astype(yc_ref.dtype)

    v = dv_ref[...]
    mu = jnp.mean(v, axis=-1, keepdims=True)
    var = jnp.mean(jnp.square(v - mu), axis=-1, keepdims=True)
    vn = ((v - mu) * lax.rsqrt(var + LN_EPS) * lng_ref[...] + lnb_ref[...]).astype(BF16)
    r = lax.broadcasted_iota(jnp.int32, (SG_CHUNK, SG_CHUNK), 0)
    cidx = lax.broadcasted_iota(jnp.int32, (SG_CHUNK, SG_CHUNK), 1)
    w_tril = [jnp.where(cidx <= r, wsp_ref[g], 0.0).astype(BF16) for g in range(N_HEADS)]
    lane_c = lax.broadcasted_iota(jnp.int32, (SG_CHUNK, GROUP_W), 1)
    for ci in range(ts // SG_CHUNK):
        rows = slice(ci * SG_CHUNK, (ci + 1) * SG_CHUNK)
        vc = vn[rows]
        zmix = _dot(w_tril[0], vc)
        for g in range(1, N_HEADS):
            zmix = jnp.where(lane_c >= g * HEAD_DIM, _dot(w_tril[g], vc), zmix)
        zfull = zmix + bsp_ref[...]
        yd_ref[rows, :] = (du_ref[rows, :] * zfull * _silu(dz_ref[rows, :])).astype(yd_ref.dtype)


def _pool_sgu(pf, wpool_bd, pool_scale, ln_g, ln_b, w_sp, bsp_exp, layer, nb, seq):
    ts = TS_POOL
    ns = seq // ts
    halo_blocks = ts // POOL_HALO

    def tile(col):
        return pl.BlockSpec((ts, GROUP_W), lambda b, i: (b * ns + i, col // 2))

    def per_layer(shape):
        nd = len(shape)
        return pl.BlockSpec((None,) + shape, lambda b, i: (layer,) + (0,) * nd)

    return pl.pallas_call(
        _pool_sgu_kernel,
        grid=(nb, ns),
        in_specs=[
            tile(PF_CIN),
            pl.BlockSpec((POOL_HALO, GROUP_W),
                         lambda b, i: (jnp.maximum((b * ns + i) * halo_blocks - 1, 0), PF_CIN // 2)),
            tile(PF_CZ), tile(PF_DU), tile(PF_DV), tile(PF_DZ),
            per_layer((GROUP_W, GROUP_W)),
            per_layer((1, GROUP_W)),
            per_layer((1, GROUP_W)),
            per_layer((1, GROUP_W)),
            per_layer((N_HEADS, SG_CHUNK, SG_CHUNK)),
            per_layer((SG_CHUNK, GROUP_W)),
        ],
        out_specs=[
            pl.BlockSpec((ts, GROUP_W), lambda b, i: (b * ns + i, 0)),
            pl.BlockSpec((ts, GROUP_W), lambda b, i: (b * ns + i, 0)),
        ],
        out_shape=[
            jax.ShapeDtypeStruct((nb * seq, GROUP_W), BF16),
            jax.ShapeDtypeStruct((nb * seq, GROUP_W), BF16),
        ],
        compiler_params=pltpu.CompilerParams(
            dimension_semantics=("arbitrary", "arbitrary"), vmem_limit_bytes=48 * 1024 * 1024),
        name="pool_sgu",
    )(pf, pf, pf, pf, pf, pf, wpool_bd, pool_scale, ln_g, ln_b, w_sp, bsp_exp)


def _project_out(ya_ref, yb_ref, yc_ref, yd_ref, w_ref, x_ref, g_ref):
    y = jnp.concatenate([ya_ref[...], yb_ref[...], yc_ref[...], yd_ref[...]], axis=1)
    out = _dot(y, w_ref[...])
    inv = lax.rsqrt(jnp.mean(out * out, axis=-1, keepdims=True) + RMS_EPS)
    return x_ref[...] + out * inv * g_ref[...]


def _out_proj_kernel(ya_ref, yb_ref, yc_ref, yd_ref, w_ref, x_ref, g_ref, o_ref):
    o_ref[...] = _project_out(ya_ref, yb_ref, yc_ref, yd_ref, w_ref, x_ref, g_ref)


def _out_in_proj_kernel(ya_ref, yb_ref, yc_ref, yd_ref, wo_ref, x_ref, gpost_ref,
                        gpre_ref, wa_ref, wg_ref, wb_ref, o_ref, pb_ref, pf_ref):
    half = o_ref.shape[0] // 2
    x_new = []
    for r in range(2):
        rows = pl.ds(r * half, half)
        x_new.append(_project_out(ya_ref.at[rows], yb_ref.at[rows], yc_ref.at[rows], yd_ref.at[rows],
                                  wo_ref, x_ref.at[rows], gpost_ref))
        o_ref[rows, :] = x_new[r]
    for r in range(2):
        rows = pl.ds(r * half, half)
        _project_in(x_new[r], gpre_ref, wa_ref, wg_ref, wb_ref, pb_ref.at[rows], pf_ref.at[rows])


def _out_in_proj(ya, yb, yc, yd, w_out, x2, g_post, g_pre, w_a, w_g, w_b, layer):
    m, d = x2.shape
    tm = TM_OUT
    ytile = pl.BlockSpec((tm, GROUP_W), lambda i: (i, 0))
    nxt = layer + 1
    return pl.pallas_call(
        _out_in_proj_kernel,
        grid=(m // tm,),
        in_specs=[
            ytile, ytile, ytile, ytile,
            pl.BlockSpec((None, 4 * GROUP_W, d), lambda i: (layer, 0, 0)),
            pl.BlockSpec((tm, d), lambda i: (i, 0)),
            pl.BlockSpec((None, 1, d), lambda i: (layer, 0, 0)),
            pl.BlockSpec((None, 1, d), lambda i: (nxt, 0, 0)),
            pl.BlockSpec((None, d, w_a.shape[2]), lambda i: (nxt, 0, 0)),
            pl.BlockSpec((None, d, w_g.shape[2]), lambda i: (nxt, 0, 0)),
            pl.BlockSpec((None, d, w_b.shape[2]), lambda i: (nxt, 0, 0)),
        ],
        out_specs=[
            pl.BlockSpec((tm, d), lambda i: (i, 0)),
            pl.BlockSpec((tm, PB_COLS), lambda i: (i, 0)),
            pl.BlockSpec((tm, PF_COLS), lambda i: (i, 0)),
        ],
        out_shape=[
            jax.ShapeDtypeStruct((m, d), F32),
            jax.ShapeDtypeStruct((m, PB_COLS), BF16),
            jax.ShapeDtypeStruct((m, PF_COLS), F32),
        ],
        compiler_params=pltpu.CompilerParams(
            dimension_semantics=("arbitrary",), vmem_limit_bytes=56 * 1024 * 1024),
        name="out_in_proj",
    )(ya, yb, yc, yd, w_out, x2, g_post, g_pre, w_a, w_g, w_b)


def _out_proj(ya, yb, yc, yd, w_out, x2, g_post, layer):
    m, d = x2.shape
    tm = TM_OUT
    ytile = pl.BlockSpec((tm, GROUP_W), lambda i: (i, 0))
    return pl.pallas_call(
        _out_proj_kernel,
        grid=(m // tm,),
        in_specs=[
            ytile, ytile, ytile, ytile,
            pl.BlockSpec((None, 4 * GROUP_W, d), lambda i: (layer, 0, 0)),
            pl.BlockSpec((tm, d), lambda i: (i, 0)),
            pl.BlockSpec((None, 1, d), lambda i: (layer, 0, 0)),
        ],
        out_specs=pl.BlockSpec((tm, d), lambda i: (i, 0)),
        out_shape=jax.ShapeDtypeStruct((m, d), F32),
        compiler_params=pltpu.CompilerParams(
            dimension_semantics=("arbitrary",), vmem_limit_bytes=48 * 1024 * 1024),
        name="out_proj",
    )(ya, yb, yc, yd, w_out, x2, g_post)


def _overlap_t(seq):
    n_cmp = (seq - CMP_LEN) // CMP_STRIDE + 1
    n_slc = seq // SEL_BLOCK
    cs = np.arange(n_cmp) * CMP_STRIDE
    ss = np.arange(n_slc) * SEL_BLOCK
    ov = (cs[None, :] < ss[:, None] + SEL_BLOCK) & (cs[None, :] + CMP_LEN > ss[:, None])
    out = np.zeros((LANES, LANES), np.float32)
    out[:n_slc, :n_cmp] = ov
    return out


def _causal_tri():
    a = np.arange(TK_SEL)[:, None]
    b = np.arange(TQ_NSA)[None, :]
    return np.stack([np.where(a - o * TQ_NSA <= b, 0.0, NEG) for o in range(TK_SEL // TQ_NSA)]).astype(np.float32)


def _window_band():
    a = np.arange(TW_WIN)[:, None]
    b = np.arange(TW_WIN)[None, :]
    return np.stack([np.where(a > b, 0.0, NEG), np.where(a <= b, 0.0, NEG)]).astype(np.float32)


def _dilated_log_multiplicity(seq):
    t = T_DIL
    d0 = np.arange(t)[:, None] - np.arange(t)[None, :]
    tabs = []
    for delta in range(seq // t):
        d = d0 + delta * t
        mult = np.zeros_like(d)
        for window, dil in DILATED_PAIRS:
            mult += (d >= 0) & (d % dil == 0) & (d // dil <= window // dil) & (d // dil <= seq // dil - 1)
        tabs.append(np.where(mult > 0, np.log2(np.maximum(mult, 1)), NEG).T)
    return np.stack(tabs).astype(np.float32)


TR_PREP = 256


def _split_w_in(w_in):
    gw = GROUP_W
    n_a = gw + 6 * HEAD_DIM
    n_b = 10 * gw
    assert w_in.shape[-1] == n_a + N_GATES + n_b, w_in.shape
    scale_a = np.ones((n_a,), np.float32)
    scale_a[:gw] = Q_SCALE
    scale_b = np.ones((n_b,), np.float32)
    scale_b[gw:2 * gw] = Q_SCALE
    w_a = (w_in[..., :n_a] * scale_a).astype(BF16)
    w_g = jnp.pad(w_in[..., n_a:n_a + N_GATES], ((0, 0), (0, 0), (0, LANES - N_GATES))).astype(BF16)
    w_b = (w_in[..., n_a + N_GATES:] * scale_b).astype(BF16)
    return w_a, w_g, w_b


def _cast_kernel(x_ref, o_ref):
    o_ref[...] = x_ref[...].astype(o_ref.dtype)


def _to_bf16(w):
    depth, r, c = w.shape
    return pl.pallas_call(
        _cast_kernel,
        grid=(depth, r // TR_PREP),
        in_specs=[pl.BlockSpec((None, TR_PREP, c), lambda l, i: (l, i, 0))],
        out_specs=pl.BlockSpec((None, TR_PREP, c), lambda l, i: (l, i, 0)),
        out_shape=jax.ShapeDtypeStruct(w.shape, BF16),
        compiler_params=pltpu.CompilerParams(dimension_semantics=("arbitrary", "arbitrary")),
        name="to_bf16",
    )(w)


def kernel(x, g_pre, w_in, pe_cmp, w_cmp1, w_cmp2, w_pool, pool_scale, sg_ln_g, sg_ln_b, w_sp, b_sp, w_out, g_post):
    nb, seq, d = x.shape
    depth = w_in.shape[0]
    assert seq % TK_SEL == 0 and seq % T_DIL == 0 and seq % TS_POOL == 0 and (nb * seq) % TM_PROJ == 0
    assert TQ_NSA % TW_WIN == 0 and TK_SEL % TQ_NSA == 0 and WIN % TW_WIN == 0 and TW_WIN % LANES == 0
    assert seq // SEL_BLOCK <= LANES - HEAD_DIM and (seq - CMP_LEN) // CMP_STRIDE + 1 < LANES
    assert seq % CMP_STRIDE == 0 and CMP_LEN == 2 * CMP_STRIDE

    w_a, w_g, w_b = _split_w_in(w_in)
    w_out_b = _to_bf16(w_out)
    w1_l = w_cmp1.reshape(depth, 2, CMP_LEN, HEAD_DIM, CMP_HIDDEN)
    w2_b = w_cmp2.astype(BF16)
    pe_rows = jnp.concatenate([pe_cmp[:, 0], pe_cmp[:, 1]], axis=-1)
    eye = jnp.eye(N_HEADS, dtype=w_pool.dtype)
    wpool_bd = jnp.einsum('lgcd,gh->lgchd', w_pool, eye).reshape(depth, GROUP_W, GROUP_W).astype(BF16)
    bsp_exp = jnp.repeat(jnp.swapaxes(b_sp, 1, 2), HEAD_DIM, axis=2)
    g_pre3 = g_pre.reshape(depth, 1, d)
    g_post3 = g_post.reshape(depth, 1, d)
    pscale3 = pool_scale.reshape(depth, 1, GROUP_W)
    lng3 = sg_ln_g.reshape(depth, 1, GROUP_W)
    lnb3 = sg_ln_b.reshape(depth, 1, GROUP_W)
    ovl_t = jnp.asarray(_overlap_t(seq), BF16)
    tri = jnp.asarray(_causal_tri())
    band = jnp.asarray(_window_band())
    dil_bias = jnp.asarray(_dilated_log_multiplicity(seq))

    x2 = x.reshape(nb * seq, d)
    pb, pf = _in_proj(x2, g_pre3, w_a, w_g, w_b, 0)
    for layer in range(depth):
        kcvc = _compress(pf, pe_rows, w1_l, w2_b, layer, nb, seq)
        ya = _nsa(pb, pf, kcvc, ovl_t, tri, band, nb, seq)
        yb = _dilated(pb, pf, dil_bias, nb, seq)
        yc, yd = _pool_sgu(pf, wpool_bd, pscale3, lng3, lnb3, w_sp, bsp_exp, layer, nb, seq)
        if layer + 1 < depth:
            x2, pb, pf = _out_in_proj(ya, yb, yc, yd, w_out_b, x2, g_post3, g_pre3, w_a, w_g, w_b, layer)
        else:
            x2 = _out_proj(ya, yb, yc, yd, w_out_b, x2, g_post3, layer)
    return x2.reshape(nb, seq, d)
```

```python
import functools

import numpy as np
import jax
import jax.numpy as jnp
from jax import lax
from jax.experimental import pallas as pl
from jax.experimental.pallas import tpu as pltpu

F32 = jnp.float32
BF16 = jnp.bfloat16

HEAD_DIM = 64
N_HEADS = 4
GROUP_W = N_HEADS * HEAD_DIM
CMP_LEN = 32
CMP_STRIDE = 16
CMP_HIDDEN = 256
SEL_BLOCK = 64
SEL_TOPK = 16
WIN = 512
FORCE = 1e4
DILATED_PAIRS = ((128, 1), (512, 4), (2048, 16))
POOL_SIZES = (2, 4, 8, 16)
SG_CHUNK = 128
RMS_EPS = 1e-6
LN_EPS = 1e-5
NEG = -1e30

LANES = 128
N_GATES = 3 * N_HEADS

PB_AQ, PB_SLC, PB_WIN, PB_BQ, PB_BK, PB_BV = 0, 2, 3, 4, 6, 8
PB_COLS = 10 * LANES
PF_CMP, PF_AG, PF_AZ, PF_BZ, PF_CIN, PF_CZ, PF_DU, PF_DV, PF_DZ = 0, 1, 2, 4, 6, 8, 10, 12, 14
PF_COLS = 16 * LANES

TM_PROJ = 512
TM_OUT = 512
TQ_NSA = 256
TK_SEL = 256
T_DIL = 256
TS_POOL = 1024
POOL_HALO = 16
TW_WIN = 128
SEL_GROUPS = 2
DEN_ROWS = 16

Q_SCALE = float(np.log2(np.e)) * HEAD_DIM ** -0.5


def _sigmoid(x):
    return 0.5 * jnp.tanh(0.5 * x) + 0.5


def _silu(x):
    return x * _sigmoid(x)


def _dot_nt(a, b):
    return lax.dot_general(a, b, (((1,), (1,)), ((), ())), preferred_element_type=F32)


def _dot(a, b):
    return jnp.dot(a, b, preferred_element_type=F32)


def _in_proj_kernel(x_ref, g_ref, wa_ref, wg_ref, wb_ref, pb_ref, pf_ref):
    half = x_ref.shape[0] // 2
    for r in range(2):
        rows = pl.ds(r * half, half)
        _project_in(x_ref[rows, :], g_ref, wa_ref, wg_ref, wb_ref, pb_ref.at[rows], pf_ref.at[rows])


def _project_in(x, g_ref, wa_ref, wg_ref, wb_ref, pb_ref, pf_ref):
    inv = lax.rsqrt(jnp.mean(x * x, axis=-1, keepdims=True) + RMS_EPS)
    h = (x * inv * g_ref[...]).astype(BF16)
    cw = 2 * LANES

    def put(dst, c, w):
        dst[:, c * cw:(c + 1) * cw] = _dot(h, w).astype(dst.dtype)

    put(pb_ref, PB_AQ // 2, wa_ref[:, 0:cw])
    put(pb_ref, PB_SLC // 2, wa_ref[:, cw + LANES:2 * cw + LANES])
    for c in range(3):
        put(pb_ref, PB_BQ // 2 + c, wb_ref[:, (1 + c) * cw:(2 + c) * cw])
    put(pf_ref, PF_CMP // 2, jnp.concatenate([wa_ref[:, cw:cw + LANES], wg_ref[...]], axis=1))
    put(pf_ref, PF_AZ // 2, wb_ref[:, 0:cw])
    for c in range(6):
        put(pf_ref, PF_BZ // 2 + c, wb_ref[:, (4 + c) * cw:(5 + c) * cw])


def _in_proj(x2, g_pre, w_a, w_g, w_b, layer):
    m, d = x2.shape
    return pl.pallas_call(
        _in_proj_kernel,
        grid=(m // TM_PROJ,),
        in_specs=[
            pl.BlockSpec((TM_PROJ, d), lambda i: (i, 0)),
            pl.BlockSpec((None, 1, d), lambda i: (layer, 0, 0)),
            pl.BlockSpec((None, d, w_a.shape[2]), lambda i: (layer, 0, 0)),
            pl.BlockSpec((None, d, w_g.shape[2]), lambda i: (layer, 0, 0)),
            pl.BlockSpec((None, d, w_b.shape[2]), lambda i: (layer, 0, 0)),
        ],
        out_specs=[
            pl.BlockSpec((TM_PROJ, PB_COLS), lambda i: (i, 0)),
            pl.BlockSpec((TM_PROJ, PF_COLS), lambda i: (i, 0)),
        ],
        out_shape=[
            jax.ShapeDtypeStruct((m, PB_COLS), BF16),
            jax.ShapeDtypeStruct((m, PF_COLS), F32),
        ],
        compiler_params=pltpu.CompilerParams(
            dimension_semantics=("arbitrary",), vmem_limit_bytes=48 * 1024 * 1024),
        name="in_proj",
    )(x2, g_pre, w_a, w_g, w_b)


def _gelu_tanh(x):
    return 0.5 * x * (1.0 + jnp.tanh(np.sqrt(2.0 / np.pi).astype(np.float32) * (x + 0.044715 * (x * x * x))))


def _compress_kernel(x_ref, pe_ref, w1_ref, w2_ref, o_ref):
    n_grp = x_ref.shape[0] // CMP_STRIDE
    ng = o_ref.shape[1]
    lo = [None, None]
    hi = [None, None]
    for l0 in range(CMP_STRIDE):
        rows = x_ref[pl.ds(l0, n_grp, stride=CMP_STRIDE), :]
        r_lo = (rows + pe_ref[l0:l0 + 1, :]).astype(BF16)
        r_hi = (rows + pe_ref[CMP_STRIDE + l0:CMP_STRIDE + l0 + 1, :]).astype(BF16)
        for j in range(2):
            cols = slice(j * HEAD_DIM, (j + 1) * HEAD_DIM)
            d_lo = _dot(r_lo[:, cols], w1_ref[j, l0].astype(BF16))
            d_hi = _dot(r_hi[:, cols], w1_ref[j, CMP_STRIDE + l0].astype(BF16))
            lo[j] = d_lo if lo[j] is None else lo[j] + d_lo
            hi[j] = d_hi if hi[j] is None else hi[j] + d_hi
    row = lax.broadcasted_iota(jnp.int32, (n_grp, HEAD_DIM), 0) % ng
    for j in range(2):
        hid = _gelu_tanh(lo[j] + pltpu.roll(hi[j], n_grp - 1, 0)).astype(BF16)
        comp = jnp.where(row < ng - 1, _dot(hid, w2_ref[j]), 0.0)
        o_ref[:, :, j * HEAD_DIM:(j + 1) * HEAD_DIM] = comp.reshape(o_ref.shape[0], ng, HEAD_DIM).astype(BF16)


def _compress(pf, pe_rows, w1, w2, layer, nb, seq):
    ng = seq // CMP_STRIDE
    return pl.pallas_call(
        _compress_kernel,
        grid=(1,),
        in_specs=[
            pl.BlockSpec((nb * seq, LANES), lambda i: (0, PF_CMP)),
            pl.BlockSpec((None, CMP_LEN, LANES), lambda i: (layer, 0, 0)),
            pl.BlockSpec((None, 2, CMP_LEN, HEAD_DIM, CMP_HIDDEN), lambda i: (layer, 0, 0, 0, 0)),
            pl.BlockSpec((None, 2, CMP_HIDDEN, HEAD_DIM), lambda i: (layer, 0, 0, 0)),
        ],
        out_specs=pl.BlockSpec((nb, ng, LANES), lambda i: (0, 0, 0)),
        out_shape=jax.ShapeDtypeStruct((nb, ng, LANES), BF16),
        compiler_params=pltpu.CompilerParams(
            dimension_semantics=("arbitrary",), vmem_limit_bytes=48 * 1024 * 1024),
        name="compress",
    )(pf, pe_rows, w1, w2)


def _nsa_kernel(q_ref, slc_ref, win_ref, kc_ref, gate_ref, z_ref, ovl_ref, tri_ref, band_ref, o_ref,
                ksel_ref, vtsel_ref, kwin_ref, vtwin_ref, kct_ref, acc_ref, s_ref):
    tq = q_ref.shape[0]
    seq = slc_ref.shape[0]
    tk = s_ref.shape[1]
    tw = band_ref.shape[1]
    nh = N_HEADS
    n_slc = seq // SEL_BLOCK
    qi = pl.program_id(1)
    q0 = qi * tq

    @pl.when(qi == 0)
    def _():
        lane_k = lax.broadcasted_iota(jnp.int32, (tk, LANES), 1)
        for j in range(seq // tk):
            rows = slice(j * tk, (j + 1) * tk)
            blk = slc_ref[rows, :]
            key_blk = (j * tk + lax.broadcasted_iota(jnp.int32, (tk, LANES), 0)) // SEL_BLOCK
            onehot = jnp.where(lane_k - HEAD_DIM == key_blk, 1.0, 0.0).astype(BF16)
            ksel_ref[rows, :] = jnp.where(lane_k < HEAD_DIM, blk, onehot)
            vtsel_ref[j] = blk.astype(F32).T[HEAD_DIM:].astype(BF16)
        lane_w = lax.broadcasted_iota(jnp.int32, (WIN, LANES), 1)
        kwin_ref[0:WIN, :] = jnp.where(lane_w == HEAD_DIM, 1.0, 0.0).astype(BF16)
        for j in range(WIN // tw):
            vtwin_ref[j] = jnp.zeros((HEAD_DIM, tw), BF16)
        lane_t = lax.broadcasted_iota(jnp.int32, (tw, LANES), 1)
        for j in range(seq // tw):
            blk = win_ref[j * tw:(j + 1) * tw, :]
            kwin_ref[WIN + j * tw:WIN + (j + 1) * tw, :] = jnp.where(lane_t < HEAD_DIM, blk, jnp.zeros_like(blk))
            vtwin_ref[WIN // tw + j] = blk.astype(F32).T[HEAD_DIM:].astype(BF16)
        kct_ref[...] = kc_ref[...].astype(F32).T[HEAD_DIM:].astype(BF16)

    q_t = q_ref[...].astype(F32).T

    def q_operand(extra):
        return jnp.concatenate([jnp.concatenate([q_t[h * HEAD_DIM:(h + 1) * HEAD_DIM], extra], axis=0)
                                for h in range(nh)], axis=1).astype(BF16)

    def tile4(x):
        return jnp.concatenate([x] * nh, axis=1)


    span = WIN + tw
    extra_row = lax.broadcasted_iota(jnp.int32, (HEAD_DIM, tq), 0)
    q_win = q_operand(jnp.where(extra_row == 0, NEG, 0.0))
    n_sub = tq // tw

    def window_scores(u):
        q_sub = jnp.concatenate([q_win[:, h * tq + u * tw:h * tq + (u + 1) * tw] for h in range(nh)], axis=1)
        start = pl.multiple_of(q0 + u * tw, tw)
        return _dot(kwin_ref[pl.ds(start, span), :], q_sub)

    kc = kc_ref[...]
    s_c = _dot(kc, q_operand(jnp.zeros((HEAD_DIM, tq), F32)))
    s_win = [window_scores(0)]

    def window_softmax(u):
        edge = [jnp.concatenate([band_ref[e]] * nh, axis=1) for e in range(2)]
        s_w = s_win[u]
        s_w = jnp.concatenate([s_w[0:tw] + edge[0], s_w[tw:span - tw], s_w[span - tw:] + edge[1]], axis=0)
        e_w = jnp.exp2(s_w - jnp.max(s_w, axis=0, keepdims=True))
        t0 = qi * n_sub + u
        vt_w = jnp.concatenate([vtwin_ref[t0 + d] for d in range(span // tw)], axis=1)
        return _dot(vt_w, e_w.astype(BF16)) / jnp.sum(e_w, axis=0, keepdims=True)

    n_cmp = (seq - CMP_LEN) // CMP_STRIDE + 1
    n_idx = lax.broadcasted_iota(jnp.int32, (LANES, tq), 0)
    t_idx = q0 + lax.broadcasted_iota(jnp.int32, (LANES, tq), 1)
    valid_c = tile4((n_idx * CMP_STRIDE + (CMP_LEN - 1) <= t_idx) & (n_idx < n_cmp))
    s_c = jnp.where(valid_c, s_c, NEG)
    e_c = jnp.exp2(s_c - jnp.max(s_c, axis=0, keepdims=True))
    p_c = jnp.where(valid_c, e_c / jnp.sum(e_c, axis=0, keepdims=True), 0.0)
    o_cmp = _dot(kct_ref[...], p_c.astype(BF16))

    psum = p_c[:, 0:tq]
    for h in range(1, nh):
        psum = psum + p_c[:, h * tq:(h + 1) * tq]
    p_hi = psum.astype(BF16)
    r1 = psum - p_hi.astype(F32)
    p_mid = r1.astype(BF16)
    p_lo = (r1 - p_mid.astype(F32)).astype(BF16)
    ovl = ovl_ref[...]
    imp = _dot(ovl, p_hi) + _dot(ovl, p_mid) + _dot(ovl, p_lo)

    o_win = []
    for u in range(n_sub - 1):
        s_win.append(window_scores(u + 1))
        o_win.append(window_softmax(u))

    v = imp[0:n_slc]
    jj = lax.broadcasted_iota(jnp.int32, (n_slc, tq), 0)
    cur = (q0 + lax.broadcasted_iota(jnp.int32, (n_slc, tq), 1)) // SEL_BLOCK
    forced = (jj == 0) | (jj == cur) | (jj == cur - 1)
    v = jnp.where(forced, FORCE, jnp.where(jj <= cur, v, -FORCE))
    sub = 8
    ranks = []
    row8 = lax.broadcasted_iota(jnp.int32, (sub, tq), 0)
    for r0 in range(0, n_slc, sub):
        vr = v[r0:r0 + sub]
        cnt = jnp.zeros((sub, tq), F32)
        for j2 in range(n_slc):
            vj = v[j2:j2 + 1, :]
            if j2 < r0:
                ahead = vj >= vr
            elif j2 >= r0 + sub:
                ahead = vj > vr
            else:
                ahead = (vj > vr) | ((vj == vr) & (row8 > j2 - r0))
            cnt = cnt + jnp.where(ahead, 1.0, 0.0)
        ranks.append(cnt)
    rank = jnp.concatenate(ranks, axis=0)
    sel_neg = jnp.where((rank < min(SEL_TOPK, n_slc)) & (jj <= cur), 0.0, NEG)
    q_sel = q_operand(jnp.concatenate([sel_neg, jnp.zeros((HEAD_DIM - n_slc, tq), F32)], axis=0))

    n_grp = s_ref.shape[0]
    gw = nh * tq // n_grp

    def scores(c, g):
        keys = ksel_ref[pl.ds(pl.multiple_of(c * tk, tk), tk), :]
        s = _dot(keys, q_sel[:, g * gw:(g + 1) * gw])
        s_ref[g] = s
        return jnp.max(s, axis=0, keepdims=True)

    def absorb(c, g, s, m_tile, m, l):
        m_new = jnp.maximum(m, m_tile)
        alpha = jnp.exp2(m - m_new)
        p = jnp.exp2(s - m_new)
        l = alpha * l + jnp.sum(p, axis=0, keepdims=True)
        acc_ref[g] = alpha * acc_ref[g] + _dot(vtsel_ref[c], p.astype(BF16))
        return m_new, l

    n_chunks = (q0 + tq + tk - 1) // tk
    acc_ref[...] = jnp.zeros_like(acc_ref)
    first_max = [scores(0, g) for g in range(n_grp)]
    o_win.append(window_softmax(n_sub - 1))

    def sel_step(c, carry):
        out = []
        for g in range(n_grp):
            m, l, m_tile = carry[g]
            s = s_ref[g]
            m_tile_next = scores(c + 1, g)
            out.append(absorb(c, g, s, m_tile, m, l) + (m_tile_next,))
        return tuple(out)

    stats = tuple((jnp.full((1, gw), NEG, F32), jnp.zeros((1, gw), F32), first_max[g]) for g in range(n_grp))
    stats = lax.fori_loop(0, n_chunks - 1, sel_step, stats)
    last = n_chunks - 1
    tri = tri_ref[(q0 - last * tk) // tq]
    tri_g = jnp.concatenate([tri] * (gw // tq), axis=1)
    l_last = []
    for g in range(n_grp):
        s_last = s_ref[g] + tri_g
        l_last.append(absorb(last, g, s_last, jnp.max(s_last, axis=0, keepdims=True), *stats[g][:2])[1])

    gate_t = _sigmoid(gate_ref[...]).T
    z_gate = _silu(z_ref[...])
    partial = []
    for h in range(nh):
        cols = slice(h * tq, (h + 1) * tq)
        o_win_h = jnp.concatenate([ow[:, h * tw:(h + 1) * tw] for ow in o_win], axis=1)
        partial.append(gate_t[0 * nh + h:0 * nh + h + 1, :] * o_cmp[:, cols]
                       + gate_t[2 * nh + h:2 * nh + h + 1, :] * o_win_h)
    o_slc = jnp.concatenate([acc_ref[g] / l_last[g] for g in range(n_grp)], axis=1)
    combs = [partial[h] + gate_t[1 * nh + h:1 * nh + h + 1, :] * o_slc[:, h * tq:(h + 1) * tq] for h in range(nh)]
    for c in range(nh // 2):
        blk = jnp.concatenate([combs[2 * c], combs[2 * c + 1]], axis=0).T
        o_ref[:, c * LANES:(c + 1) * LANES] = (blk * z_gate[:, c * LANES:(c + 1) * LANES]).astype(o_ref.dtype)


def _nsa(pb, pf, kcvc, ovl, tri, band, nb, seq):
    tq = TQ_NSA
    tk = TK_SEL
    nq = seq // tq
    return pl.pallas_call(
        _nsa_kernel,
        grid=(nb, nq),
        in_specs=[
            pl.BlockSpec((tq, 2 * LANES), lambda b, i: (b * nq + i, PB_AQ // 2)),
            pl.BlockSpec((seq, LANES), lambda b, i: (b, PB_SLC)),
            pl.BlockSpec((seq, LANES), lambda b, i: (b, PB_WIN)),
            pl.BlockSpec((None, LANES, LANES), lambda b, i: (b, 0, 0)),
            pl.BlockSpec((tq, LANES), lambda b, i: (b * nq + i, PF_AG)),
            pl.BlockSpec((tq, 2 * LANES), lambda b, i: (b * nq + i, PF_AZ // 2)),
            pl.BlockSpec((LANES, LANES), lambda b, i: (0, 0)),
            pl.BlockSpec(tri.shape, lambda b, i: (0, 0, 0)),
            pl.BlockSpec(band.shape, lambda b, i: (0, 0, 0)),
        ],
        out_specs=pl.BlockSpec((tq, GROUP_W), lambda b, i: (b * nq + i, 0)),
        out_shape=jax.ShapeDtypeStruct((nb * seq, GROUP_W), BF16),
        scratch_shapes=[
            pltpu.VMEM((seq, LANES), BF16),
            pltpu.VMEM((seq // tk, HEAD_DIM, tk), BF16),
            pltpu.VMEM((WIN + seq, LANES), BF16),
            pltpu.VMEM(((WIN + seq) // TW_WIN, HEAD_DIM, TW_WIN), BF16),
            pltpu.VMEM((HEAD_DIM, LANES), BF16),
            pltpu.VMEM((SEL_GROUPS, HEAD_DIM, N_HEADS * tq // SEL_GROUPS), F32),
            pltpu.VMEM((SEL_GROUPS, tk, N_HEADS * tq // SEL_GROUPS), F32),
        ],
        compiler_params=pltpu.CompilerParams(
            dimension_semantics=("arbitrary", "arbitrary"), vmem_limit_bytes=48 * 1024 * 1024),
        name="nsa",
    )(pb, pb, pb, kcvc, pf, pf, ovl, tri, band)


def _dilated_kernel(q_ref, qnext_ref, k_ref, v_ref, z_ref, bias_ref, o_ref, vt_ref, acc_ref, s_ref, mt_ref):
    t = q_ref.shape[0]
    seq = k_ref.shape[0]
    n_pairs = N_HEADS // 2
    rows_h = DEN_ROWS + HEAD_DIM
    qi = pl.program_id(1)

    @pl.when(qi == 0)
    def _():
        ones_rows = jnp.where(lax.broadcasted_iota(jnp.int32, (DEN_ROWS, t), 0) == 0, 1.0, 0.0).astype(BF16)
        for j in range(seq // t):
            v_t = v_ref[j * t:(j + 1) * t, :].astype(F32).T.astype(BF16)
            for h in range(N_HEADS):
                vt_ref[j, h * rows_h:h * rows_h + DEN_ROWS] = ones_rows
                vt_ref[j, h * rows_h + DEN_ROWS:(h + 1) * rows_h] = v_t[h * HEAD_DIM:(h + 1) * HEAD_DIM]

    upper = lax.broadcasted_iota(jnp.int32, (LANES, t), 0) >= HEAD_DIM

    def query_operands(ref):
        q_t = ref[...].astype(F32).T
        ops = []
        for c in range(n_pairs):
            blk = q_t[c * LANES:(c + 1) * LANES]
            ops.append(jnp.concatenate([jnp.where(upper, 0.0, blk), jnp.where(upper, blk, 0.0)], axis=1).astype(BF16))
        return ops

    qs = query_operands(q_ref)
    qs_next = query_operands(qnext_ref)
    acc_ref[...] = jnp.zeros_like(acc_ref)

    def scores(j, c, q_op, q_tile):
        k0 = pl.multiple_of(j * t, t)
        bias = bias_ref[jnp.minimum(q_tile - j, bias_ref.shape[0] - 1)]
        s = _dot(k_ref[pl.ds(k0, t), c * LANES:(c + 1) * LANES], q_op) + jnp.concatenate([bias, bias], axis=1)
        s_ref[c] = s
        return jnp.max(s, axis=0, keepdims=True)

    @pl.when(qi == 0)
    def _():
        for c in range(n_pairs):
            mt_ref[c] = scores(0, c, qs[c], qi)

    def step(j, carry):
        is_last = j == qi
        j_next = jnp.where(is_last, 0, j + 1)
        q_tile = jnp.where(is_last, qi + 1, qi)
        out = []
        for c in range(n_pairs):
            m, m_tile = carry[c]
            s = s_ref[c]
            m_tile_next = scores(j_next, c, jnp.where(is_last, qs_next[c], qs[c]), q_tile)
            m_new = jnp.maximum(m, m_tile)
            alpha = jnp.exp2(m - m_new)
            pb = jnp.exp2(s - m_new).astype(BF16)
            for e in range(2):
                h = 2 * c + e
                pv = _dot(vt_ref[j, h * rows_h:(h + 1) * rows_h, :], pb[:, e * t:(e + 1) * t])
                acc_ref[h] = alpha[:, e * t:(e + 1) * t] * acc_ref[h] + pv
            out.append((m_new, m_tile_next))
        return tuple(out)

    init = tuple((jnp.full((1, 2 * t), NEG, F32), mt_ref[c]) for c in range(n_pairs))
    final = lax.fori_loop(0, qi + 1, step, init)
    for c in range(n_pairs):
        mt_ref[c] = final[c][1]
    for c in range(n_pairs):
        heads = [acc_ref[2 * c + e] for e in range(2)]
        o_pair = jnp.concatenate([a[DEN_ROWS:] / a[0:1] for a in heads], axis=0).T
        zc = z_ref[:, c * LANES:(c + 1) * LANES]
        o_ref[:, c * LANES:(c + 1) * LANES] = (o_pair * _silu(zc)).astype(o_ref.dtype)


def _dilated(pb, pf, bias_tab, nb, seq):
    t = T_DIL
    nq = seq // t
    return pl.pallas_call(
        _dilated_kernel,
        grid=(nb, nq),
        in_specs=[
            pl.BlockSpec((t, GROUP_W), lambda b, i: (b * nq + i, PB_BQ // 2)),
            pl.BlockSpec((t, GROUP_W), lambda b, i: (b * nq + jnp.minimum(i + 1, nq - 1), PB_BQ // 2)),
            pl.BlockSpec((seq, GROUP_W), lambda b, i: (b, PB_BK // 2)),
            pl.BlockSpec((seq, GROUP_W), lambda b, i: (b, PB_BV // 2)),
            pl.BlockSpec((t, GROUP_W), lambda b, i: (b * nq + i, PF_BZ // 2)),
            pl.BlockSpec((nq, t, t), lambda b, i: (0, 0, 0)),
        ],
        out_specs=pl.BlockSpec((t, GROUP_W), lambda b, i: (b * nq + i, 0)),
        out_shape=jax.ShapeDtypeStruct((nb * seq, GROUP_W), BF16),
        scratch_shapes=[
            pltpu.VMEM((nq, N_HEADS * (DEN_ROWS + HEAD_DIM), t), BF16),
            pltpu.VMEM((N_HEADS, DEN_ROWS + HEAD_DIM, t), F32),
            pltpu.VMEM((N_HEADS // 2, t, 2 * t), F32),
            pltpu.VMEM((N_HEADS // 2, 1, 2 * t), F32),
        ],
        compiler_params=pltpu.CompilerParams(
            dimension_semantics=("arbitrary", "arbitrary"), vmem_limit_bytes=48 * 1024 * 1024),
        name="dilated",
    )(pb, pb, pb, pb, pf, bias_tab)


def _pool_sgu_kernel(c_ref, cprev_ref, cz_ref, du_ref, dv_ref, dz_ref,
                     wpool_ref, pscale_ref, lng_ref, lnb_ref, wsp_ref, bsp_ref, yc_ref, yd_ref):
    ts = c_ref.shape[0]
    i = pl.program_id(1)
    lane = lax.broadcasted_iota(jnp.int32, (ts, GROUP_W), 1)

    cur = c_ref[...]
    prev = jnp.where(i > 0, cprev_ref[...], 0.0)
    ext = jnp.concatenate([prev, cur], axis=0)
    t_head = i * ts + lax.broadcasted_iota(jnp.int32, (POOL_HALO, 1), 0)
    pooled = None
    acc = ext
    width = 1
    for g, w in enumerate(POOL_SIZES):
        while width < w:
            acc = acc + pltpu.roll(acc, width, 0)
            width *= 2
        inv_head = 1.0 / jnp.minimum(t_head + 1, w).astype(F32)
        mean_w = jnp.concatenate([acc[POOL_HALO:2 * POOL_HALO] * inv_head,
                                  acc[2 * POOL_HALO:POOL_HALO + ts] * (1.0 / w)], axis=0)
        pooled = mean_w if pooled is None else jnp.where(lane >= g * HEAD_DIM, mean_w, pooled)
    pooled = pooled - cur
    mixed = _dot(pooled.astype(BF16), wpool_ref[...]) * pscale_ref[...]
    yc_ref[...] = (mixed * _silu(cz_ref[...])).astype(yc_ref.dtype)

    v = dv_ref[...]
    mu = jnp.mean(v, axis=-1, keepdims=True)
    var = jnp.mean(jnp.square(v - mu), axis=-1, keepdims=True)
    vn = ((v - mu) * lax.rsqrt(var + LN_EPS) * lng_ref[...] + lnb_ref[...]).astype(BF16)
    r = lax.broadcasted_iota(jnp.int32, (SG_CHUNK, SG_CHUNK), 0)
    cidx = lax.broadcasted_iota(jnp.int32, (SG_CHUNK, SG_CHUNK), 1)
    w_tril = [jnp.where(cidx <= r, wsp_ref[g], 0.0).astype(BF16) for g in range(N_HEADS)]
    lane_c = lax.broadcasted_iota(jnp.int32, (SG_CHUNK, GROUP_W), 1)
    for ci in range(ts // SG_CHUNK):
        rows = slice(ci * SG_CHUNK, (ci + 1) * SG_CHUNK)
        vc = vn[rows]
        zmix = _dot(w_tril[0], vc)
        for g in range(1, N_HEADS):
            zmix = jnp.where(lane_c >= g * HEAD_DIM, _dot(w_tril[g], vc), zmix)
        zfull = zmix + bsp_ref[...]
        yd_ref[rows, :] = (du_ref[rows, :] * zfull * _silu(dz_ref[rows, :])).astype(yd_ref.dtype)


def _pool_sgu(pf, wpool_bd, pool_scale, ln_g, ln_b, w_sp, bsp_exp, layer, nb, seq):
    ts = TS_POOL
    ns = seq // ts
    halo_blocks = ts // POOL_HALO

    def tile(col):
        return pl.BlockSpec((ts, GROUP_W), lambda b, i: (b * ns + i, col // 2))

    def per_layer(shape):
        nd = len(shape)
        return pl.BlockSpec((None,) + shape, lambda b, i: (layer,) + (0,) * nd)

    return pl.pallas_call(
        _pool_sgu_kernel,
        grid=(nb, ns),
        in_specs=[
            tile(PF_CIN),
            pl.BlockSpec((POOL_HALO, GROUP_W),
                         lambda b, i: (jnp.maximum((b * ns + i) * halo_blocks - 1, 0), PF_CIN // 2)),
            tile(PF_CZ), tile(PF_DU), tile(PF_DV), tile(PF_DZ),
            per_layer((GROUP_W, GROUP_W)),
            per_layer((1, GROUP_W)),
            per_layer((1, GROUP_W)),
            per_layer((1, GROUP_W)),
            per_layer((N_HEADS, SG_CHUNK, SG_CHUNK)),
            per_layer((SG_CHUNK, GROUP_W)),
        ],
        out_specs=[
            pl.BlockSpec((ts, GROUP_W), lambda b, i: (b * ns + i, 0)),
            pl.BlockSpec((ts, GROUP_W), lambda b, i: (b * ns + i, 0)),
        ],
        out_shape=[
            jax.ShapeDtypeStruct((nb * seq, GROUP_W), BF16),
            jax.ShapeDtypeStruct((nb * seq, GROUP_W), BF16),
        ],
        compiler_params=pltpu.CompilerParams(
            dimension_semantics=("arbitrary", "arbitrary"), vmem_limit_bytes=48 * 1024 * 1024),
        name="pool_sgu",
    )(pf, pf, pf, pf, pf, pf, wpool_bd, pool_scale, ln_g, ln_b, w_sp, bsp_exp)


def _project_out(ya_ref, yb_ref, yc_ref, yd_ref, w_ref, x_ref, g_ref):
    y = jnp.concatenate([ya_ref[...], yb_ref[...], yc_ref[...], yd_ref[...]], axis=1)
    out = _dot(y, w_ref[...])
    inv = lax.rsqrt(jnp.mean(out * out, axis=-1, keepdims=True) + RMS_EPS)
    return x_ref[...] + out * inv * g_ref[...]


def _out_proj_kernel(ya_ref, yb_ref, yc_ref, yd_ref, w_ref, x_ref, g_ref, o_ref):
    o_ref[...] = _project_out(ya_ref, yb_ref, yc_ref, yd_ref, w_ref, x_ref, g_ref)


def _out_in_proj_kernel(ya_ref, yb_ref, yc_ref, yd_ref, wo_ref, x_ref, gpost_ref,
                        gpre_ref, wa_ref, wg_ref, wb_ref, o_ref, pb_ref, pf_ref):
    half = o_ref.shape[0] // 2
    x_new = []
    for r in range(2):
        rows = pl.ds(r * half, half)
        x_new.append(_project_out(ya_ref.at[rows], yb_ref.at[rows], yc_ref.at[rows], yd_ref.at[rows],
                                  wo_ref, x_ref.at[rows], gpost_ref))
        o_ref[rows, :] = x_new[r]
    for r in range(2):
        rows = pl.ds(r * half, half)
        _project_in(x_new[r], gpre_ref, wa_ref, wg_ref, wb_ref, pb_ref.at[rows], pf_ref.at[rows])


def _out_in_proj(ya, yb, yc, yd, w_out, x2, g_post, g_pre, w_a, w_g, w_b, layer):
    m, d = x2.shape
    tm = TM_OUT
    ytile = pl.BlockSpec((tm, GROUP_W), lambda i: (i, 0))
    nxt = layer + 1
    return pl.pallas_call(
        _out_in_proj_kernel,
        grid=(m // tm,),
        in_specs=[
            ytile, ytile, ytile, ytile,
            pl.BlockSpec((None, 4 * GROUP_W, d), lambda i: (layer, 0, 0)),
            pl.BlockSpec((tm, d), lambda i: (i, 0)),
            pl.BlockSpec((None, 1, d), lambda i: (layer, 0, 0)),
            pl.BlockSpec((None, 1, d), lambda i: (nxt, 0, 0)),
            pl.BlockSpec((None, d, w_a.shape[2]), lambda i: (nxt, 0, 0)),
            pl.BlockSpec((None, d, w_g.shape[2]), lambda i: (nxt, 0, 0)),
            pl.BlockSpec((None, d, w_b.shape[2]), lambda i: (nxt, 0, 0)),
        ],
        out_specs=[
            pl.BlockSpec((tm, d), lambda i: (i, 0)),
            pl.BlockSpec((tm, PB_COLS), lambda i: (i, 0)),
            pl.BlockSpec((tm, PF_COLS), lambda i: (i, 0)),
        ],
        out_shape=[
            jax.ShapeDtypeStruct((m, d), F32),
            jax.ShapeDtypeStruct((m, PB_COLS), BF16),
            jax.ShapeDtypeStruct((m, PF_COLS), F32),
        ],
        compiler_params=pltpu.CompilerParams(
            dimension_semantics=("arbitrary",), vmem_limit_bytes=56 * 1024 * 1024),
        name="out_in_proj",
    )(ya, yb, yc, yd, w_out, x2, g_post, g_pre, w_a, w_g, w_b)


def _out_proj(ya, yb, yc, yd, w_out, x2, g_post, layer):
    m, d = x2.shape
    tm = TM_OUT
    ytile = pl.BlockSpec((tm, GROUP_W), lambda i: (i, 0))
    return pl.pallas_call(
        _out_proj_kernel,
        grid=(m // tm,),
        in_specs=[
            ytile, ytile, ytile, ytile,
            pl.BlockSpec((None, 4 * GROUP_W, d), lambda i: (layer, 0, 0)),
            pl.BlockSpec((tm, d), lambda i: (i, 0)),
            pl.BlockSpec((None, 1, d), lambda i: (layer, 0, 0)),
        ],
        out_specs=pl.BlockSpec((tm, d), lambda i: (i, 0)),
        out_shape=jax.ShapeDtypeStruct((m, d), F32),
        compiler_params=pltpu.CompilerParams(
            dimension_semantics=("arbitrary",), vmem_limit_bytes=48 * 1024 * 1024),
        name="out_proj",
    )(ya, yb, yc, yd, w_out, x2, g_post)


def _overlap_t(seq):
    n_cmp = (seq - CMP_LEN) // CMP_STRIDE + 1
    n_slc = seq // SEL_BLOCK
    cs = np.arange(n_cmp) * CMP_STRIDE
    ss = np.arange(n_slc) * SEL_BLOCK
    ov = (cs[None, :] < ss[:, None] + SEL_BLOCK) & (cs[None, :] + CMP_LEN > ss[:, None])
    out = np.zeros((LANES, LANES), np.float32)
    out[:n_slc, :n_cmp] = ov
    return out


def _causal_tri():
    a = np.arange(TK_SEL)[:, None]
    b = np.arange(TQ_NSA)[None, :]
    return np.stack([np.where(a - o * TQ_NSA <= b, 0.0, NEG) for o in range(TK_SEL // TQ_NSA)]).astype(np.float32)


def _window_band():
    a = np.arange(TW_WIN)[:, None]
    b = np.arange(TW_WIN)[None, :]
    return np.stack([np.where(a > b, 0.0, NEG), np.where(a <= b, 0.0, NEG)]).astype(np.float32)


def _dilated_log_multiplicity(seq):
    t = T_DIL
    d0 = np.arange(t)[:, None] - np.arange(t)[None, :]
    tabs = []
    for delta in range(seq // t):
        d = d0 + delta * t
        mult = np.zeros_like(d)
        for window, dil in DILATED_PAIRS:
            mult += (d >= 0) & (d % dil == 0) & (d // dil <= window // dil) & (d // dil <= seq // dil - 1)
        tabs.append(np.where(mult > 0, np.log2(np.maximum(mult, 1)), NEG).T)
    return np.stack(tabs).astype(np.float32)


def _split_w_in(w_in):
    gw = GROUP_W
    n_a = gw + 6 * HEAD_DIM
    n_b = 10 * gw
    assert w_in.shape[-1] == n_a + N_GATES + n_b, w_in.shape
    scale_a = np.ones((n_a,), np.float32)
    scale_a[:gw] = Q_SCALE
    scale_b = np.ones((n_b,), np.float32)
    scale_b[gw:2 * gw] = Q_SCALE
    w_a = (w_in[..., :n_a] * scale_a).astype(BF16)
    w_g = jnp.pad(w_in[..., n_a:n_a + N_GATES], ((0, 0), (0, 0), (0, LANES - N_GATES))).astype(BF16)
    w_b = (w_in[..., n_a + N_GATES:] * scale_b).astype(BF16)
    return w_a, w_g, w_b


def kernel(x, g_pre, w_in, pe_cmp, w_cmp1, w_cmp2, w_pool, pool_scale, sg_ln_g, sg_ln_b, w_sp, b_sp, w_out, g_post):
    nb, seq, d = x.shape
    depth = w_in.shape[0]
    assert seq % TK_SEL == 0 and seq % T_DIL == 0 and seq % TS_POOL == 0 and (nb * seq) % TM_PROJ == 0
    assert TQ_NSA % TW_WIN == 0 and TK_SEL % TQ_NSA == 0 and WIN % TW_WIN == 0 and TW_WIN % LANES == 0
    assert seq // SEL_BLOCK <= LANES - HEAD_DIM and (seq - CMP_LEN) // CMP_STRIDE + 1 < LANES
    assert seq % CMP_STRIDE == 0 and CMP_LEN == 2 * CMP_STRIDE

    w_a, w_g, w_b = _split_w_in(w_in)
    w_out_b = w_out.astype(BF16)
    w1_l = w_cmp1.reshape(depth, 2, CMP_LEN, HEAD_DIM, CMP_HIDDEN)
    w2_b = w_cmp2.astype(BF16)
    pe_rows = jnp.concatenate([pe_cmp[:, 0], pe_cmp[:, 1]], axis=-1)
    eye = jnp.eye(N_HEADS, dtype=w_pool.dtype)
    wpool_bd = jnp.einsum('lgcd,gh->lgchd', w_pool, eye).reshape(depth, GROUP_W, GROUP_W).astype(BF16)
    bsp_exp = jnp.repeat(jnp.swapaxes(b_sp, 1, 2), HEAD_DIM, axis=2)
    g_pre3 = g_pre.reshape(depth, 1, d)
    g_post3 = g_post.reshape(depth, 1, d)
    pscale3 = pool_scale.reshape(depth, 1, GROUP_W)
    lng3 = sg_ln_g.reshape(depth, 1, GROUP_W)
    lnb3 = sg_ln_b.reshape(depth, 1, GROUP_W)
    ovl_t = jnp.asarray(_overlap_t(seq), BF16)
    tri = jnp.asarray(_causal_tri())
    band = jnp.asarray(_window_band())
    dil_bias = jnp.asarray(_dilated_log_multiplicity(seq))

    x2 = x.reshape(nb * seq, d)
    pb, pf = _in_proj(x2, g_pre3, w_a, w_g, w_b, 0)
    for layer in range(depth):
        kcvc = _compress(pf, pe_rows, w1_l, w2_b, layer, nb, seq)
        ya = _nsa(pb, pf, kcvc, ovl_t, tri, band, nb, seq)
        yb = _dilated(pb, pf, dil_bias, nb, seq)
        yc, yd = _pool_sgu(pf, wpool_bd, pscale3, lng3, lnb3, w_sp, bsp_exp, layer, nb, seq)
        if layer + 1 < depth:
            x2, pb, pf = _out_in_proj(ya, yb, yc, yd, w_out_b, x2, g_post3, g_pre3, w_a, w_g, w_b, layer)
        else:
            x2 = _out_proj(ya, yb, yc, yd, w_out_b, x2, g_post3, layer)
    return x2.reshape(nb, seq, d)
```

```python
import functools

import numpy as np
import jax
import jax.numpy as jnp
from jax import lax
from jax.experimental import pallas as pl
from jax.experimental.pallas import tpu as pltpu

F32 = jnp.float32
BF16 = jnp.bfloat16

HEAD_DIM = 64
N_HEADS = 4
GROUP_W = N_HEADS * HEAD_DIM
CMP_LEN = 32
CMP_STRIDE = 16
CMP_HIDDEN = 256
SEL_BLOCK = 64
SEL_TOPK = 16
WIN = 512
FORCE = 1e4
DILATED_PAIRS = ((128, 1), (512, 4), (2048, 16))
POOL_SIZES = (2, 4, 8, 16)
SG_CHUNK = 128
RMS_EPS = 1e-6
LN_EPS = 1e-5
NEG = -1e30

LANES = 128
N_GATES = 3 * N_HEADS

PB_AQ, PB_SLC, PB_WIN, PB_BQ, PB_BK, PB_BV = 0, 2, 3, 4, 6, 8
PB_COLS = 10 * LANES
PF_CMP, PF_AG, PF_AZ, PF_BZ, PF_CIN, PF_CZ, PF_DU, PF_DV, PF_DZ = 0, 1, 2, 4, 6, 8, 10, 12, 14
PF_COLS = 16 * LANES

TM_PROJ = 512
TM_OUT = 512
TQ_NSA = 256
TK_SEL = 256
T_DIL = 256
TS_POOL = 1024
POOL_HALO = 16
TW_WIN = 128
SEL_GROUPS = 2
NSA_TILES = 2
DEN_ROWS = 16

Q_SCALE = float(np.log2(np.e)) * HEAD_DIM ** -0.5


def _sigmoid(x):
    return 0.5 * jnp.tanh(0.5 * x) + 0.5


def _silu(x):
    return x * _sigmoid(x)


def _dot_nt(a, b):
    return lax.dot_general(a, b, (((1,), (1,)), ((), ())), preferred_element_type=F32)


def _dot(a, b):
    return jnp.dot(a, b, preferred_element_type=F32)


def _in_proj_kernel(x_ref, g_ref, wa_ref, wg_ref, wb_ref, pb_ref, pf_ref):
    half = x_ref.shape[0] // 2
    for r in range(2):
        rows = pl.ds(r * half, half)
        _project_in(x_ref[rows, :], g_ref, wa_ref, wg_ref, wb_ref, pb_ref.at[rows], pf_ref.at[rows])


def _project_in(x, g_ref, wa_ref, wg_ref, wb_ref, pb_ref, pf_ref):
    inv = lax.rsqrt(jnp.mean(x * x, axis=-1, keepdims=True) + RMS_EPS)
    h = (x * inv * g_ref[...]).astype(BF16)
    cw = 2 * LANES

    def put(dst, c, w):
        dst[:, c * cw:(c + 1) * cw] = _dot(h, w).astype(dst.dtype)

    put(pb_ref, PB_AQ // 2, wa_ref[:, 0:cw])
    put(pb_ref, PB_SLC // 2, wa_ref[:, cw + LANES:2 * cw + LANES])
    for c in range(3):
        put(pb_ref, PB_BQ // 2 + c, wb_ref[:, (1 + c) * cw:(2 + c) * cw])
    put(pf_ref, PF_CMP // 2, jnp.concatenate([wa_ref[:, cw:cw + LANES], wg_ref[...]], axis=1))
    put(pf_ref, PF_AZ // 2, wb_ref[:, 0:cw])
    for c in range(6):
        put(pf_ref, PF_BZ // 2 + c, wb_ref[:, (4 + c) * cw:(5 + c) * cw])


def _in_proj(x2, g_pre, w_a, w_g, w_b, layer):
    m, d = x2.shape
    return pl.pallas_call(
        _in_proj_kernel,
        grid=(m // TM_PROJ,),
        in_specs=[
            pl.BlockSpec((TM_PROJ, d), lambda i: (i, 0)),
            pl.BlockSpec((None, 1, d), lambda i: (layer, 0, 0)),
            pl.BlockSpec((None, d, w_a.shape[2]), lambda i: (layer, 0, 0)),
            pl.BlockSpec((None, d, w_g.shape[2]), lambda i: (layer, 0, 0)),
            pl.BlockSpec((None, d, w_b.shape[2]), lambda i: (layer, 0, 0)),
        ],
        out_specs=[
            pl.BlockSpec((TM_PROJ, PB_COLS), lambda i: (i, 0)),
            pl.BlockSpec((TM_PROJ, PF_COLS), lambda i: (i, 0)),
        ],
        out_shape=[
            jax.ShapeDtypeStruct((m, PB_COLS), BF16),
            jax.ShapeDtypeStruct((m, PF_COLS), F32),
        ],
        compiler_params=pltpu.CompilerParams(
            dimension_semantics=("arbitrary",), vmem_limit_bytes=48 * 1024 * 1024),
        name="in_proj",
    )(x2, g_pre, w_a, w_g, w_b)


def _gelu_tanh(x):
    return 0.5 * x * (1.0 + jnp.tanh(np.sqrt(2.0 / np.pi).astype(np.float32) * (x + 0.044715 * (x * x * x))))


def _compress_kernel(x_ref, pe_ref, w1_ref, w2_ref, o_ref):
    n_grp = x_ref.shape[0] // CMP_STRIDE
    ng = o_ref.shape[1]
    lo = [None, None]
    hi = [None, None]
    for l0 in range(CMP_STRIDE):
        rows = x_ref[pl.ds(l0, n_grp, stride=CMP_STRIDE), :]
        r_lo = (rows + pe_ref[l0:l0 + 1, :]).astype(BF16)
        r_hi = (rows + pe_ref[CMP_STRIDE + l0:CMP_STRIDE + l0 + 1, :]).astype(BF16)
        for j in range(2):
            cols = slice(j * HEAD_DIM, (j + 1) * HEAD_DIM)
            d_lo = _dot(r_lo[:, cols], w1_ref[j, l0].astype(BF16))
            d_hi = _dot(r_hi[:, cols], w1_ref[j, CMP_STRIDE + l0].astype(BF16))
            lo[j] = d_lo if lo[j] is None else lo[j] + d_lo
            hi[j] = d_hi if hi[j] is None else hi[j] + d_hi
    row = lax.broadcasted_iota(jnp.int32, (n_grp, HEAD_DIM), 0) % ng
    for j in range(2):
        hid = _gelu_tanh(lo[j] + pltpu.roll(hi[j], n_grp - 1, 0)).astype(BF16)
        comp = jnp.where(row < ng - 1, _dot(hid, w2_ref[j]), 0.0)
        o_ref[:, :, j * HEAD_DIM:(j + 1) * HEAD_DIM] = comp.reshape(o_ref.shape[0], ng, HEAD_DIM).astype(BF16)


def _compress(pf, pe_rows, w1, w2, layer, nb, seq):
    ng = seq // CMP_STRIDE
    return pl.pallas_call(
        _compress_kernel,
        grid=(1,),
        in_specs=[
            pl.BlockSpec((nb * seq, LANES), lambda i: (0, PF_CMP)),
            pl.BlockSpec((None, CMP_LEN, LANES), lambda i: (layer, 0, 0)),
            pl.BlockSpec((None, 2, CMP_LEN, HEAD_DIM, CMP_HIDDEN), lambda i: (layer, 0, 0, 0, 0)),
            pl.BlockSpec((None, 2, CMP_HIDDEN, HEAD_DIM), lambda i: (layer, 0, 0, 0)),
        ],
        out_specs=pl.BlockSpec((nb, ng, LANES), lambda i: (0, 0, 0)),
        out_shape=jax.ShapeDtypeStruct((nb, ng, LANES), BF16),
        compiler_params=pltpu.CompilerParams(
            dimension_semantics=("arbitrary",), vmem_limit_bytes=48 * 1024 * 1024),
        name="compress",
    )(pf, pe_rows, w1, w2)


def _nsa_kernel(q_ref, slc_ref, win_ref, kc_ref, gate_ref, z_ref, ovl_ref, tri_ref, band_ref, o_ref,
                ksel_ref, vtsel_ref, kwin_ref, vtwin_ref, kct_ref, acc_ref, s_ref):
    tq = q_ref.shape[0] // NSA_TILES
    seq = slc_ref.shape[0]
    tk = s_ref.shape[2]
    tw = band_ref.shape[1]
    nh = N_HEADS
    n_slc = seq // SEL_BLOCK
    step = pl.program_id(1)

    @pl.when(step == 0)
    def _():
        lane_k = lax.broadcasted_iota(jnp.int32, (tk, LANES), 1)
        for j in range(seq // tk):
            rows = slice(j * tk, (j + 1) * tk)
            blk = slc_ref[rows, :]
            key_blk = (j * tk + lax.broadcasted_iota(jnp.int32, (tk, LANES), 0)) // SEL_BLOCK
            onehot = jnp.where(lane_k - HEAD_DIM == key_blk, 1.0, 0.0).astype(BF16)
            ksel_ref[rows, :] = jnp.where(lane_k < HEAD_DIM, blk, onehot)
            vtsel_ref[j] = blk.astype(F32).T[HEAD_DIM:].astype(BF16)
        lane_w = lax.broadcasted_iota(jnp.int32, (WIN, LANES), 1)
        kwin_ref[0:WIN, :] = jnp.where(lane_w == HEAD_DIM, 1.0, 0.0).astype(BF16)
        for j in range(WIN // tw):
            vtwin_ref[j] = jnp.zeros((HEAD_DIM, tw), BF16)
        lane_t = lax.broadcasted_iota(jnp.int32, (tw, LANES), 1)
        for j in range(seq // tw):
            blk = win_ref[j * tw:(j + 1) * tw, :]
            kwin_ref[WIN + j * tw:WIN + (j + 1) * tw, :] = jnp.where(lane_t < HEAD_DIM, blk, jnp.zeros_like(blk))
            vtwin_ref[WIN // tw + j] = blk.astype(F32).T[HEAD_DIM:].astype(BF16)
        kct_ref[...] = kc_ref[...].astype(F32).T[HEAD_DIM:].astype(BF16)

    def tile_program(t_u):
        qi = step * NSA_TILES + t_u
        q0 = qi * tq
        rows_u = slice(t_u * tq, (t_u + 1) * tq)
        q_t = q_ref[rows_u, :].astype(F32).T

        def q_operand(extra):
            return jnp.concatenate([jnp.concatenate([q_t[h * HEAD_DIM:(h + 1) * HEAD_DIM], extra], axis=0)
                                    for h in range(nh)], axis=1).astype(BF16)

        def tile4(x):
            return jnp.concatenate([x] * nh, axis=1)


        span = WIN + tw
        extra_row = lax.broadcasted_iota(jnp.int32, (HEAD_DIM, tq), 0)
        q_win = q_operand(jnp.where(extra_row == 0, NEG, 0.0))
        n_sub = tq // tw

        def window_scores(u):
            q_sub = jnp.concatenate([q_win[:, h * tq + u * tw:h * tq + (u + 1) * tw] for h in range(nh)], axis=1)
            start = pl.multiple_of(q0 + u * tw, tw)
            return _dot(kwin_ref[pl.ds(start, span), :], q_sub)

        kc = kc_ref[...]
        s_c = _dot(kc, q_operand(jnp.zeros((HEAD_DIM, tq), F32)))
        s_win = [window_scores(0)]
        yield

        def window_softmax(u):
            edge = [jnp.concatenate([band_ref[e]] * nh, axis=1) for e in range(2)]
            s_w = s_win[u]
            s_w = jnp.concatenate([s_w[0:tw] + edge[0], s_w[tw:span - tw], s_w[span - tw:] + edge[1]], axis=0)
            e_w = jnp.exp2(s_w - jnp.max(s_w, axis=0, keepdims=True))
            t0 = qi * n_sub + u
            vt_w = jnp.concatenate([vtwin_ref[t0 + d] for d in range(span // tw)], axis=1)
            return _dot(vt_w, e_w.astype(BF16)) / jnp.sum(e_w, axis=0, keepdims=True)

        n_cmp = (seq - CMP_LEN) // CMP_STRIDE + 1
        n_idx = lax.broadcasted_iota(jnp.int32, (LANES, tq), 0)
        t_idx = q0 + lax.broadcasted_iota(jnp.int32, (LANES, tq), 1)
        valid_c = tile4((n_idx * CMP_STRIDE + (CMP_LEN - 1) <= t_idx) & (n_idx < n_cmp))
        s_c = jnp.where(valid_c, s_c, NEG)
        e_c = jnp.exp2(s_c - jnp.max(s_c, axis=0, keepdims=True))
        p_c = jnp.where(valid_c, e_c / jnp.sum(e_c, axis=0, keepdims=True), 0.0)
        o_cmp = _dot(kct_ref[...], p_c.astype(BF16))

        psum = p_c[:, 0:tq]
        for h in range(1, nh):
            psum = psum + p_c[:, h * tq:(h + 1) * tq]
        p_hi = psum.astype(BF16)
        r1 = psum - p_hi.astype(F32)
        p_mid = r1.astype(BF16)
        p_lo = (r1 - p_mid.astype(F32)).astype(BF16)
        ovl = ovl_ref[...]
        imp = _dot(ovl, p_hi) + _dot(ovl, p_mid) + _dot(ovl, p_lo)
        yield

        o_win = []
        for u in range(n_sub - 1):
            s_win.append(window_scores(u + 1))
            o_win.append(window_softmax(u))
        yield

        v = imp[0:n_slc]
        jj = lax.broadcasted_iota(jnp.int32, (n_slc, tq), 0)
        cur = (q0 + lax.broadcasted_iota(jnp.int32, (n_slc, tq), 1)) // SEL_BLOCK
        forced = (jj == 0) | (jj == cur) | (jj == cur - 1)
        v = jnp.where(forced, FORCE, jnp.where(jj <= cur, v, -FORCE))
        sub = 8
        ranks = []
        row8 = lax.broadcasted_iota(jnp.int32, (sub, tq), 0)
        for r0 in range(0, n_slc, sub):
            vr = v[r0:r0 + sub]
            cnt = jnp.zeros((sub, tq), F32)
            for j2 in range(n_slc):
                vj = v[j2:j2 + 1, :]
                if j2 < r0:
                    ahead = vj >= vr
                elif j2 >= r0 + sub:
                    ahead = vj > vr
                else:
                    ahead = (vj > vr) | ((vj == vr) & (row8 > j2 - r0))
                cnt = cnt + jnp.where(ahead, 1.0, 0.0)
            ranks.append(cnt)
        rank = jnp.concatenate(ranks, axis=0)
        sel_neg = jnp.where((rank < min(SEL_TOPK, n_slc)) & (jj <= cur), 0.0, NEG)
        q_sel = q_operand(jnp.concatenate([sel_neg, jnp.zeros((HEAD_DIM - n_slc, tq), F32)], axis=0))
        yield

        n_grp = s_ref.shape[1]
        gw = nh * tq // n_grp

        def scores(c, g):
            keys = ksel_ref[pl.ds(pl.multiple_of(c * tk, tk), tk), :]
            s = _dot(keys, q_sel[:, g * gw:(g + 1) * gw])
            s_ref[t_u, g] = s
            return jnp.max(s, axis=0, keepdims=True)

        def absorb(c, g, s, m_tile, m, l):
            m_new = jnp.maximum(m, m_tile)
            alpha = jnp.exp2(m - m_new)
            p = jnp.exp2(s - m_new)
            l = alpha * l + jnp.sum(p, axis=0, keepdims=True)
            acc_ref[t_u, g] = alpha * acc_ref[t_u, g] + _dot(vtsel_ref[c], p.astype(BF16))
            return m_new, l

        n_chunks = (q0 + tq + tk - 1) // tk
        acc_ref[t_u] = jnp.zeros(acc_ref.shape[1:], F32)
        first_max = [scores(0, g) for g in range(n_grp)]
        o_win.append(window_softmax(n_sub - 1))
        yield

        def sel_step(c, carry):
            out = []
            for g in range(n_grp):
                m, l, m_tile = carry[g]
                s = s_ref[t_u, g]
                m_tile_next = scores(c + 1, g)
                out.append(absorb(c, g, s, m_tile, m, l) + (m_tile_next,))
            return tuple(out)

        stats = tuple((jnp.full((1, gw), NEG, F32), jnp.zeros((1, gw), F32), first_max[g]) for g in range(n_grp))
        stats = lax.fori_loop(0, n_chunks - 1, sel_step, stats)
        yield
        last = n_chunks - 1
        tri = tri_ref[(q0 - last * tk) // tq]
        tri_g = jnp.concatenate([tri] * (gw // tq), axis=1)
        l_last = []
        for g in range(n_grp):
            s_last = s_ref[t_u, g] + tri_g
            l_last.append(absorb(last, g, s_last, jnp.max(s_last, axis=0, keepdims=True), *stats[g][:2])[1])
        yield

        gate_t = _sigmoid(gate_ref[rows_u, :]).T
        z_gate = _silu(z_ref[rows_u, :])
        partial = []
        for h in range(nh):
            cols = slice(h * tq, (h + 1) * tq)
            o_win_h = jnp.concatenate([ow[:, h * tw:(h + 1) * tw] for ow in o_win], axis=1)
            partial.append(gate_t[0 * nh + h:0 * nh + h + 1, :] * o_cmp[:, cols]
                           + gate_t[2 * nh + h:2 * nh + h + 1, :] * o_win_h)
        o_slc = jnp.concatenate([acc_ref[t_u, g] / l_last[g] for g in range(n_grp)], axis=1)
        combs = [partial[h] + gate_t[1 * nh + h:1 * nh + h + 1, :] * o_slc[:, h * tq:(h + 1) * tq] for h in range(nh)]
        for c in range(nh // 2):
            blk = jnp.concatenate([combs[2 * c], combs[2 * c + 1]], axis=0).T
            o_ref[rows_u, c * LANES:(c + 1) * LANES] = (blk * z_gate[:, c * LANES:(c + 1) * LANES]).astype(o_ref.dtype)

    programs = [tile_program(t_u) for t_u in range(NSA_TILES)]
    while programs:
        for prog in list(programs):
            try:
                next(prog)
            except StopIteration:
                programs.remove(prog)


def _nsa(pb, pf, kcvc, ovl, tri, band, nb, seq):
    tq = TQ_NSA * NSA_TILES
    tk = TK_SEL
    nq = seq // tq
    return pl.pallas_call(
        _nsa_kernel,
        grid=(nb, nq),
        in_specs=[
            pl.BlockSpec((tq, 2 * LANES), lambda b, i: (b * nq + i, PB_AQ // 2)),
            pl.BlockSpec((seq, LANES), lambda b, i: (b, PB_SLC)),
            pl.BlockSpec((seq, LANES), lambda b, i: (b, PB_WIN)),
            pl.BlockSpec((None, LANES, LANES), lambda b, i: (b, 0, 0)),
            pl.BlockSpec((tq, LANES), lambda b, i: (b * nq + i, PF_AG)),
            pl.BlockSpec((tq, 2 * LANES), lambda b, i: (b * nq + i, PF_AZ // 2)),
            pl.BlockSpec((LANES, LANES), lambda b, i: (0, 0)),
            pl.BlockSpec(tri.shape, lambda b, i: (0, 0, 0)),
            pl.BlockSpec(band.shape, lambda b, i: (0, 0, 0)),
        ],
        out_specs=pl.BlockSpec((tq, GROUP_W), lambda b, i: (b * nq + i, 0)),
        out_shape=jax.ShapeDtypeStruct((nb * seq, GROUP_W), BF16),
        scratch_shapes=[
            pltpu.VMEM((seq, LANES), BF16),
            pltpu.VMEM((seq // tk, HEAD_DIM, tk), BF16),
            pltpu.VMEM((WIN + seq, LANES), BF16),
            pltpu.VMEM(((WIN + seq) // TW_WIN, HEAD_DIM, TW_WIN), BF16),
            pltpu.VMEM((HEAD_DIM, LANES), BF16),
            pltpu.VMEM((NSA_TILES, SEL_GROUPS, HEAD_DIM, N_HEADS * TQ_NSA // SEL_GROUPS), F32),
            pltpu.VMEM((NSA_TILES, SEL_GROUPS, tk, N_HEADS * TQ_NSA // SEL_GROUPS), F32),
        ],
        compiler_params=pltpu.CompilerParams(
            dimension_semantics=("arbitrary", "arbitrary"), vmem_limit_bytes=48 * 1024 * 1024),
        name="nsa",
    )(pb, pb, pb, kcvc, pf, pf, ovl, tri, band)


def _dilated_kernel(q_ref, qnext_ref, k_ref, v_ref, z_ref, bias_ref, o_ref, vt_ref, acc_ref, s_ref, mt_ref):
    t = q_ref.shape[0]
    seq = k_ref.shape[0]
    n_pairs = N_HEADS // 2
    rows_h = DEN_ROWS + HEAD_DIM
    qi = pl.program_id(1)

    @pl.when(qi == 0)
    def _():
        ones_rows = jnp.where(lax.broadcasted_iota(jnp.int32, (DEN_ROWS, t), 0) == 0, 1.0, 0.0).astype(BF16)
        for j in range(seq // t):
            v_t = v_ref[j * t:(j + 1) * t, :].astype(F32).T.astype(BF16)
            for h in range(N_HEADS):
                vt_ref[j, h * rows_h:h * rows_h + DEN_ROWS] = ones_rows
                vt_ref[j, h * rows_h + DEN_ROWS:(h + 1) * rows_h] = v_t[h * HEAD_DIM:(h + 1) * HEAD_DIM]

    upper = lax.broadcasted_iota(jnp.int32, (LANES, t), 0) >= HEAD_DIM

    def query_operands(ref):
        q_t = ref[...].astype(F32).T
        ops = []
        for c in range(n_pairs):
            blk = q_t[c * LANES:(c + 1) * LANES]
            ops.append(jnp.concatenate([jnp.where(upper, 0.0, blk), jnp.where(upper, blk, 0.0)], axis=1).astype(BF16))
        return ops

    qs = query_operands(q_ref)
    qs_next = query_operands(qnext_ref)
    acc_ref[...] = jnp.zeros_like(acc_ref)

    def scores(j, c, q_op, q_tile):
        k0 = pl.multiple_of(j * t, t)
        bias = bias_ref[jnp.minimum(q_tile - j, bias_ref.shape[0] - 1)]
        s = _dot(k_ref[pl.ds(k0, t), c * LANES:(c + 1) * LANES], q_op) + jnp.concatenate([bias, bias], axis=1)
        s_ref[c] = s
        return jnp.max(s, axis=0, keepdims=True)

    @pl.when(qi == 0)
    def _():
        for c in range(n_pairs):
            mt_ref[c] = scores(0, c, qs[c], qi)

    def step(j, carry):
        is_last = j == qi
        j_next = jnp.where(is_last, 0, j + 1)
        q_tile = jnp.where(is_last, qi + 1, qi)
        out = []
        for c in range(n_pairs):
            m, m_tile = carry[c]
            s = s_ref[c]
            m_tile_next = scores(j_next, c, jnp.where(is_last, qs_next[c], qs[c]), q_tile)
            m_new = jnp.maximum(m, m_tile)
            alpha = jnp.exp2(m - m_new)
            pb = jnp.exp2(s - m_new).astype(BF16)
            for e in range(2):
                h = 2 * c + e
                pv = _dot(vt_ref[j, h * rows_h:(h + 1) * rows_h, :], pb[:, e * t:(e + 1) * t])
                acc_ref[h] = alpha[:, e * t:(e + 1) * t] * acc_ref[h] + pv
            out.append((m_new, m_tile_next))
        return tuple(out)

    init = tuple((jnp.full((1, 2 * t), NEG, F32), mt_ref[c]) for c in range(n_pairs))
    final = lax.fori_loop(0, qi + 1, step, init)
    for c in range(n_pairs):
        mt_ref[c] = final[c][1]
    for c in range(n_pairs):
        heads = [acc_ref[2 * c + e] for e in range(2)]
        o_pair = jnp.concatenate([a[DEN_ROWS:] / a[0:1] for a in heads], axis=0).T
        zc = z_ref[:, c * LANES:(c + 1) * LANES]
        o_ref[:, c * LANES:(c + 1) * LANES] = (o_pair * _silu(zc)).astype(o_ref.dtype)


def _dilated(pb, pf, bias_tab, nb, seq):
    t = T_DIL
    nq = seq // t
    return pl.pallas_call(
        _dilated_kernel,
        grid=(nb, nq),
        in_specs=[
            pl.BlockSpec((t, GROUP_W), lambda b, i: (b * nq + i, PB_BQ // 2)),
            pl.BlockSpec((t, GROUP_W), lambda b, i: (b * nq + jnp.minimum(i + 1, nq - 1), PB_BQ // 2)),
            pl.BlockSpec((seq, GROUP_W), lambda b, i: (b, PB_BK // 2)),
            pl.BlockSpec((seq, GROUP_W), lambda b, i: (b, PB_BV // 2)),
            pl.BlockSpec((t, GROUP_W), lambda b, i: (b * nq + i, PF_BZ // 2)),
            pl.BlockSpec((nq, t, t), lambda b, i: (0, 0, 0)),
        ],
        out_specs=pl.BlockSpec((t, GROUP_W), lambda b, i: (b * nq + i, 0)),
        out_shape=jax.ShapeDtypeStruct((nb * seq, GROUP_W), BF16),
        scratch_shapes=[
            pltpu.VMEM((nq, N_HEADS * (DEN_ROWS + HEAD_DIM), t), BF16),
            pltpu.VMEM((N_HEADS, DEN_ROWS + HEAD_DIM, t), F32),
            pltpu.VMEM((N_HEADS // 2, t, 2 * t), F32),
            pltpu.VMEM((N_HEADS // 2, 1, 2 * t), F32),
        ],
        compiler_params=pltpu.CompilerParams(
            dimension_semantics=("arbitrary", "arbitrary"), vmem_limit_bytes=48 * 1024 * 1024),
        name="dilated",
    )(pb, pb, pb, pb, pf, bias_tab)


def _pool_sgu_kernel(c_ref, cprev_ref, cz_ref, du_ref, dv_ref, dz_ref,
                     wpool_ref, pscale_ref, lng_ref, lnb_ref, wsp_ref, bsp_ref, yc_ref, yd_ref):
    ts = c_ref.shape[0]
    i = pl.program_id(1)
    lane = lax.broadcasted_iota(jnp.int32, (ts, GROUP_W), 1)

    cur = c_ref[...]
    prev = jnp.where(i > 0, cprev_ref[...], 0.0)
    ext = jnp.concatenate([prev, cur], axis=0)
    t_head = i * ts + lax.broadcasted_iota(jnp.int32, (POOL_HALO, 1), 0)
    pooled = None
    acc = ext
    width = 1
    for g, w in enumerate(POOL_SIZES):
        while width < w:
            acc = acc + pltpu.roll(acc, width, 0)
            width *= 2
        inv_head = 1.0 / jnp.minimum(t_head + 1, w).astype(F32)
        mean_w = jnp.concatenate([acc[POOL_HALO:2 * POOL_HALO] * inv_head,
                                  acc[2 * POOL_HALO:POOL_HALO + ts] * (1.0 / w)], axis=0)
        pooled = mean_w if pooled is None else jnp.where(lane >= g * HEAD_DIM, mean_w, pooled)
    pooled = pooled - cur
    mixed = _dot(pooled.astype(BF16), wpool_ref[...]) * pscale_ref[...]
    yc_ref[...] = (mixed * _silu(cz_ref[...])).astype(yc_ref.dtype)

    v = dv_ref[...]
    mu = jnp.mean(v, axis=-1, keepdims=True)
    var = jnp.mean(jnp.square(v - mu), axis=-1, keepdims=True)
    vn = ((v - mu) * lax.rsqrt(var + LN_EPS) * lng_ref[...] + lnb_ref[...]).astype(BF16)
    r = lax.broadcasted_iota(jnp.int32, (SG_CHUNK, SG_CHUNK), 0)
    cidx = lax.broadcasted_iota(jnp.int32, (SG_CHUNK, SG_CHUNK), 1)
    w_tril = [jnp.where(cidx <= r, wsp_ref[g], 0.0).astype(BF16) for g in range(N_HEADS)]
    lane_c = lax.broadcasted_iota(jnp.int32, (SG_CHUNK, GROUP_W), 1)
    for ci in range(ts // SG_CHUNK):
        rows = slice(ci * SG_CHUNK, (ci + 1) * SG_CHUNK)
        vc = vn[rows]
        zmix = _dot(w_tril[0], vc)
        for g in range(1, N_HEADS):
            zmix = jnp.where(lane_c >= g * HEAD_DIM, _dot(w_tril[g], vc), zmix)
        zfull = zmix + bsp_ref[...]
        yd_ref[rows, :] = (du_ref[rows, :] * zfull * _silu(dz_ref[rows, :])).astype(yd_ref.dtype)


def _pool_sgu(pf, wpool_bd, pool_scale, ln_g, ln_b, w_sp, bsp_exp, layer, nb, seq):
    ts = TS_POOL
    ns = seq // ts
    halo_blocks = ts // POOL_HALO

    def tile(col):
        return pl.BlockSpec((ts, GROUP_W), lambda b, i: (b * ns + i, col // 2))

    def per_layer(shape):
        nd = len(shape)
        return pl.BlockSpec((None,) + shape, lambda b, i: (layer,) + (0,) * nd)

    return pl.pallas_call(
        _pool_sgu_kernel,
        grid=(nb, ns),
        in_specs=[
            tile(PF_CIN),
            pl.BlockSpec((POOL_HALO, GROUP_W),
                         lambda b, i: (jnp.maximum((b * ns + i) * halo_blocks - 1, 0), PF_CIN // 2)),
            tile(PF_CZ), tile(PF_DU), tile(PF_DV), tile(PF_DZ),
            per_layer((GROUP_W, GROUP_W)),
            per_layer((1, GROUP_W)),
            per_layer((1, GROUP_W)),
            per_layer((1, GROUP_W)),
            per_layer((N_HEADS, SG_CHUNK, SG_CHUNK)),
            per_layer((SG_CHUNK, GROUP_W)),
        ],
        out_specs=[
            pl.BlockSpec((ts, GROUP_W), lambda b, i: (b * ns + i, 0)),
            pl.BlockSpec((ts, GROUP_W), lambda b, i: (b * ns + i, 0)),
        ],
        out_shape=[
            jax.ShapeDtypeStruct((nb * seq, GROUP_W), BF16),
            jax.ShapeDtypeStruct((nb * seq, GROUP_W), BF16),
        ],
        compiler_params=pltpu.CompilerParams(
            dimension_semantics=("arbitrary", "arbitrary"), vmem_limit_bytes=48 * 1024 * 1024),
        name="pool_sgu",
    )(pf, pf, pf, pf, pf, pf, wpool_bd, pool_scale, ln_g, ln_b, w_sp, bsp_exp)


def _project_out(ya_ref, yb_ref, yc_ref, yd_ref, w_ref, x_ref, g_ref):
    y = jnp.concatenate([ya_ref[...], yb_ref[...], yc_ref[...], yd_ref[...]], axis=1)
    out = _dot(y, w_ref[...])
    inv = lax.rsqrt(jnp.mean(out * out, axis=-1, keepdims=True) + RMS_EPS)
    return x_ref[...] + out * inv * g_ref[...]


def _out_proj_kernel(ya_ref, yb_ref, yc_ref, yd_ref, w_ref, x_ref, g_ref, o_ref):
    o_ref[...] = _project_out(ya_ref, yb_ref, yc_ref, yd_ref, w_ref, x_ref, g_ref)


def _out_in_proj_kernel(ya_ref, yb_ref, yc_ref, yd_ref, wo_ref, x_ref, gpost_ref,
                        gpre_ref, wa_ref, wg_ref, wb_ref, o_ref, pb_ref, pf_ref):
    half = o_ref.shape[0] // 2
    x_new = []
    for r in range(2):
        rows = pl.ds(r * half, half)
        x_new.append(_project_out(ya_ref.at[rows], yb_ref.at[rows], yc_ref.at[rows], yd_ref.at[rows],
                                  wo_ref, x_ref.at[rows], gpost_ref))
        o_ref[rows, :] = x_new[r]
    for r in range(2):
        rows = pl.ds(r * half, half)
        _project_in(x_new[r], gpre_ref, wa_ref, wg_ref, wb_ref, pb_ref.at[rows], pf_ref.at[rows])


def _out_in_proj(ya, yb, yc, yd, w_out, x2, g_post, g_pre, w_a, w_g, w_b, layer):
    m, d = x2.shape
    tm = TM_OUT
    ytile = pl.BlockSpec((tm, GROUP_W), lambda i: (i, 0))
    nxt = layer + 1
    return pl.pallas_call(
        _out_in_proj_kernel,
        grid=(m // tm,),
        in_specs=[
            ytile, ytile, ytile, ytile,
            pl.BlockSpec((None, 4 * GROUP_W, d), lambda i: (layer, 0, 0)),
            pl.BlockSpec((tm, d), lambda i: (i, 0)),
            pl.BlockSpec((None, 1, d), lambda i: (layer, 0, 0)),
            pl.BlockSpec((None, 1, d), lambda i: (nxt, 0, 0)),
            pl.BlockSpec((None, d, w_a.shape[2]), lambda i: (nxt, 0, 0)),
            pl.BlockSpec((None, d, w_g.shape[2]), lambda i: (nxt, 0, 0)),
            pl.BlockSpec((None, d, w_b.shape[2]), lambda i: (nxt, 0, 0)),
        ],
        out_specs=[
            pl.BlockSpec((tm, d), lambda i: (i, 0)),
            pl.BlockSpec((tm, PB_COLS), lambda i: (i, 0)),
            pl.BlockSpec((tm, PF_COLS), lambda i: (i, 0)),
        ],
        out_shape=[
            jax.ShapeDtypeStruct((m, d), F32),
            jax.ShapeDtypeStruct((m, PB_COLS), BF16),
            jax.ShapeDtypeStruct((m, PF_COLS), F32),
        ],
        compiler_params=pltpu.CompilerParams(
            dimension_semantics=("arbitrary",), vmem_limit_bytes=56 * 1024 * 1024),
        name="out_in_proj",
    )(ya, yb, yc, yd, w_out, x2, g_post, g_pre, w_a, w_g, w_b)


def _out_proj(ya, yb, yc, yd, w_out, x2, g_post, layer):
    m, d = x2.shape
    tm = TM_OUT
    ytile = pl.BlockSpec((tm, GROUP_W), lambda i: (i, 0))
    return pl.pallas_call(
        _out_proj_kernel,
        grid=(m // tm,),
        in_specs=[
            ytile, ytile, ytile, ytile,
            pl.BlockSpec((None, 4 * GROUP_W, d), lambda i: (layer, 0, 0)),
            pl.BlockSpec((tm, d), lambda i: (i, 0)),
            pl.BlockSpec((None, 1, d), lambda i: (layer, 0, 0)),
        ],
        out_specs=pl.BlockSpec((tm, d), lambda i: (i, 0)),
        out_shape=jax.ShapeDtypeStruct((m, d), F32),
        compiler_params=pltpu.CompilerParams(
            dimension_semantics=("arbitrary",), vmem_limit_bytes=48 * 1024 * 1024),
        name="out_proj",
    )(ya, yb, yc, yd, w_out, x2, g_post)


def _overlap_t(seq):
    n_cmp = (seq - CMP_LEN) // CMP_STRIDE + 1
    n_slc = seq // SEL_BLOCK
    cs = np.arange(n_cmp) * CMP_STRIDE
    ss = np.arange(n_slc) * SEL_BLOCK
    ov = (cs[None, :] < ss[:, None] + SEL_BLOCK) & (cs[None, :] + CMP_LEN > ss[:, None])
    out = np.zeros((LANES, LANES), np.float32)
    out[:n_slc, :n_cmp] = ov
    return out


def _causal_tri():
    a = np.arange(TK_SEL)[:, None]
    b = np.arange(TQ_NSA)[None, :]
    return np.stack([np.where(a - o * TQ_NSA <= b, 0.0, NEG) for o in range(TK_SEL // TQ_NSA)]).astype(np.float32)


def _window_band():
    a = np.arange(TW_WIN)[:, None]
    b = np.arange(TW_WIN)[None, :]
    return np.stack([np.where(a > b, 0.0, NEG), np.where(a <= b, 0.0, NEG)]).astype(np.float32)


def _dilated_log_multiplicity(seq):
    t = T_DIL
    d0 = np.arange(t)[:, None] - np.arange(t)[None, :]
    tabs = []
    for delta in range(seq // t):
        d = d0 + delta * t
        mult = np.zeros_like(d)
        for window, dil in DILATED_PAIRS:
            mult += (d >= 0) & (d % dil == 0) & (d // dil <= window // dil) & (d // dil <= seq // dil - 1)
        tabs.append(np.where(mult > 0, np.log2(np.maximum(mult, 1)), NEG).T)
    return np.stack(tabs).astype(np.float32)


def _split_w_in(w_in):
    gw = GROUP_W
    n_a = gw + 6 * HEAD_DIM
    n_b = 10 * gw
    assert w_in.shape[-1] == n_a + N_GATES + n_b, w_in.shape
    scale_a = np.ones((n_a,), np.float32)
    scale_a[:gw] = Q_SCALE
    scale_b = np.ones((n_b,), np.float32)
    scale_b[gw:2 * gw] = Q_SCALE
    w_a = (w_in[..., :n_a] * scale_a).astype(BF16)
    w_g = jnp.pad(w_in[..., n_a:n_a + N_GATES], ((0, 0), (0, 0), (0, LANES - N_GATES))).astype(BF16)
    w_b = (w_in[..., n_a + N_GATES:] * scale_b).astype(BF16)
    return w_a, w_g, w_b


def kernel(x, g_pre, w_in, pe_cmp, w_cmp1, w_cmp2, w_pool, pool_scale, sg_ln_g, sg_ln_b, w_sp, b_sp, w_out, g_post):
    nb, seq, d = x.shape
    depth = w_in.shape[0]
    assert seq % TK_SEL == 0 and seq % T_DIL == 0 and seq % TS_POOL == 0 and (nb * seq) % TM_PROJ == 0
    assert TQ_NSA % TW_WIN == 0 and TK_SEL % TQ_NSA == 0 and WIN % TW_WIN == 0 and TW_WIN % LANES == 0
    assert seq % (TQ_NSA * NSA_TILES) == 0
    assert seq // SEL_BLOCK <= LANES - HEAD_DIM and (seq - CMP_LEN) // CMP_STRIDE + 1 < LANES
    assert seq % CMP_STRIDE == 0 and CMP_LEN == 2 * CMP_STRIDE

    w_a, w_g, w_b = _split_w_in(w_in)
    w_out_b = w_out.astype(BF16)
    w1_l = w_cmp1.reshape(depth, 2, CMP_LEN, HEAD_DIM, CMP_HIDDEN)
    w2_b = w_cmp2.astype(BF16)
    pe_rows = jnp.concatenate([pe_cmp[:, 0], pe_cmp[:, 1]], axis=-1)
    eye = jnp.eye(N_HEADS, dtype=w_pool.dtype)
    wpool_bd = jnp.einsum('lgcd,gh->lgchd', w_pool, eye).reshape(depth, GROUP_W, GROUP_W).astype(BF16)
    bsp_exp = jnp.repeat(jnp.swapaxes(b_sp, 1, 2), HEAD_DIM, axis=2)
    g_pre3 = g_pre.reshape(depth, 1, d)
    g_post3 = g_post.reshape(depth, 1, d)
    pscale3 = pool_scale.reshape(depth, 1, GROUP_W)
    lng3 = sg_ln_g.reshape(depth, 1, GROUP_W)
    lnb3 = sg_ln_b.reshape(depth, 1, GROUP_W)
    ovl_t = jnp.asarray(_overlap_t(seq), BF16)
    tri = jnp.asarray(_causal_tri())
    band = jnp.asarray(_window_band())
    dil_bias = jnp.asarray(_dilated_log_multiplicity(seq))

    x2 = x.reshape(nb * seq, d)
    pb, pf = _in_proj(x2, g_pre3, w_a, w_g, w_b, 0)
    for layer in range(depth):
        kcvc = _compress(pf, pe_rows, w1_l, w2_b, layer, nb, seq)
        ya = _nsa(pb, pf, kcvc, ovl_t, tri, band, nb, seq)
        yb = _dilated(pb, pf, dil_bias, nb, seq)
        yc, yd = _pool_sgu(pf, wpool_bd, pscale3, lng3, lnb3, w_sp, bsp_exp, layer, nb, seq)
        if layer + 1 < depth:
            x2, pb, pf = _out_in_proj(ya, yb, yc, yd, w_out_b, x2, g_post3, g_pre3, w_a, w_g, w_b, layer)
        else:
            x2 = _out_proj(ya, yb, yc, yd, w_out_b, x2, g_post3, layer)
    return x2.reshape(nb, seq, d)
```

```python
import numpy as np
import jax
import jax.numpy as jnp
from jax import lax
from jax.experimental import pallas as pl
from jax.experimental.pallas import tpu as pltpu

F32 = jnp.float32
BF16 = jnp.bfloat16

HEAD_DIM = 64
N_HEADS = 4
GROUP_W = N_HEADS * HEAD_DIM
CMP_LEN = 32
CMP_STRIDE = 16
CMP_HIDDEN = 256
SEL_BLOCK = 64
SEL_TOPK = 16
WIN = 512
FORCE = 1e4
DILATED_PAIRS = ((128, 1), (512, 4), (2048, 16))
POOL_SIZES = (2, 4, 8, 16)
SG_CHUNK = 128
RMS_EPS = 1e-6
LN_EPS = 1e-5
NEG = -1e30

LANES = 128
V7X_VMEM_BYTES = 64 * 1024 * 1024
VMEM_LIMIT = V7X_VMEM_BYTES * 3 // 4
VMEM_LIMIT_FUSED = V7X_VMEM_BYTES * 7 // 8
N_GATES = 3 * N_HEADS

PB_AQ, PB_SLC, PB_WIN, PB_BQ, PB_BK, PB_BV = 0, 2, 3, 4, 6, 8
PB_COLS = 10 * LANES
PF_CMP, PF_AG, PF_AZ, PF_BZ, PF_CIN, PF_CZ, PF_DU, PF_DV, PF_DZ = 0, 1, 2, 4, 6, 8, 10, 12, 14
PF_COLS = 16 * LANES

TM_PROJ = 512
TM_OUT = 512
TM_LAST = 1024
TQ_NSA = 256
TK_SEL = 256
T_DIL = 256
TS_POOL = 1024
POOL_HALO = 16
TW_WIN = 128
SEL_GROUPS = 2
NSA_TILES = 4
DEN_ROWS = 16

Q_SCALE = float(np.log2(np.e)) * HEAD_DIM ** -0.5


def _sigmoid(x):
    return 0.5 * jnp.tanh(0.5 * x) + 0.5


def _silu(x):
    return x * _sigmoid(x)


def _dot(a, b):
    return jnp.dot(a, b, preferred_element_type=F32)


def _in_proj_kernel(x_ref, g_ref, wa_ref, wg_ref, wb_ref, pb_ref, pf_ref):
    half = x_ref.shape[0] // 2
    for r in range(2):
        rows = pl.ds(r * half, half)
        _project_in(x_ref[rows, :], g_ref, wa_ref, wg_ref, wb_ref, pb_ref.at[rows], pf_ref.at[rows])


def _project_in(x, g_ref, wa_ref, wg_ref, wb_ref, pb_ref, pf_ref):
    inv = lax.rsqrt(jnp.mean(x * x, axis=-1, keepdims=True) + RMS_EPS)
    h = (x * inv * g_ref[...]).astype(BF16)
    cw = 2 * LANES

    def put(dst, c, w):
        dst[:, c * cw:(c + 1) * cw] = _dot(h, w).astype(dst.dtype)

    put(pb_ref, PB_AQ // 2, wa_ref[:, 0:cw])
    put(pb_ref, PB_SLC // 2, wa_ref[:, cw + LANES:2 * cw + LANES])
    for c in range(3):
        put(pb_ref, PB_BQ // 2 + c, wb_ref[:, (1 + c) * cw:(2 + c) * cw])
    put(pf_ref, PF_CMP // 2, jnp.concatenate([wa_ref[:, cw:cw + LANES], wg_ref[...]], axis=1))
    put(pf_ref, PF_AZ // 2, wb_ref[:, 0:cw])
    for c in range(6):
        put(pf_ref, PF_BZ // 2 + c, wb_ref[:, (4 + c) * cw:(5 + c) * cw])


def _in_proj(x2, g_pre, w_a, w_g, w_b, layer):
    m, d = x2.shape
    return pl.pallas_call(
        _in_proj_kernel,
        grid=(m // TM_PROJ,),
        in_specs=[
            pl.BlockSpec((TM_PROJ, d), lambda i: (i, 0)),
            pl.BlockSpec((None, 1, d), lambda i: (layer, 0, 0)),
            pl.BlockSpec((None, d, w_a.shape[2]), lambda i: (layer, 0, 0)),
            pl.BlockSpec((None, d, w_g.shape[2]), lambda i: (layer, 0, 0)),
            pl.BlockSpec((None, d, w_b.shape[2]), lambda i: (layer, 0, 0)),
        ],
        out_specs=[
            pl.BlockSpec((TM_PROJ, PB_COLS), lambda i: (i, 0)),
            pl.BlockSpec((TM_PROJ, PF_COLS), lambda i: (i, 0)),
        ],
        out_shape=[
            jax.ShapeDtypeStruct((m, PB_COLS), BF16),
            jax.ShapeDtypeStruct((m, PF_COLS), F32),
        ],
        compiler_params=pltpu.CompilerParams(
            dimension_semantics=("arbitrary",), vmem_limit_bytes=VMEM_LIMIT),
        name="in_proj",
    )(x2, g_pre, w_a, w_g, w_b)


def _gelu_tanh(x):
    return 0.5 * x * (1.0 + jnp.tanh(np.sqrt(2.0 / np.pi).astype(np.float32) * (x + 0.044715 * (x * x * x))))


def _compress_kernel(x_ref, pe_ref, w1_ref, w2_ref, o_ref):
    n_grp = x_ref.shape[0] // CMP_STRIDE
    ng = o_ref.shape[1]
    lo = [None, None]
    hi = [None, None]
    for l0 in range(CMP_STRIDE):
        rows = x_ref[pl.ds(l0, n_grp, stride=CMP_STRIDE), :]
        r_lo = (rows + pe_ref[l0:l0 + 1, :]).astype(BF16)
        r_hi = (rows + pe_ref[CMP_STRIDE + l0:CMP_STRIDE + l0 + 1, :]).astype(BF16)
        for j in range(2):
            cols = slice(j * HEAD_DIM, (j + 1) * HEAD_DIM)
            d_lo = _dot(r_lo[:, cols], w1_ref[j, l0].astype(BF16))
            d_hi = _dot(r_hi[:, cols], w1_ref[j, CMP_STRIDE + l0].astype(BF16))
            lo[j] = d_lo if lo[j] is None else lo[j] + d_lo
            hi[j] = d_hi if hi[j] is None else hi[j] + d_hi
    row = lax.broadcasted_iota(jnp.int32, (n_grp, HEAD_DIM), 0) % ng
    for j in range(2):
        hid = _gelu_tanh(lo[j] + pltpu.roll(hi[j], n_grp - 1, 0)).astype(BF16)
        comp = jnp.where(row < ng - 1, _dot(hid, w2_ref[j]), 0.0)
        o_ref[:, :, j * HEAD_DIM:(j + 1) * HEAD_DIM] = comp.reshape(o_ref.shape[0], ng, HEAD_DIM).astype(BF16)


def _compress(pf, pe_rows, w1, w2, layer, nb, seq):
    ng = seq // CMP_STRIDE
    return pl.pallas_call(
        _compress_kernel,
        grid=(1,),
        in_specs=[
            pl.BlockSpec((nb * seq, LANES), lambda i: (0, PF_CMP)),
            pl.BlockSpec((None, CMP_LEN, LANES), lambda i: (layer, 0, 0)),
            pl.BlockSpec((None, 2, CMP_LEN, HEAD_DIM, CMP_HIDDEN), lambda i: (layer, 0, 0, 0, 0)),
            pl.BlockSpec((None, 2, CMP_HIDDEN, HEAD_DIM), lambda i: (layer, 0, 0, 0)),
        ],
        out_specs=pl.BlockSpec((nb, ng, LANES), lambda i: (0, 0, 0)),
        out_shape=jax.ShapeDtypeStruct((nb, ng, LANES), BF16),
        compiler_params=pltpu.CompilerParams(
            dimension_semantics=("arbitrary",), vmem_limit_bytes=VMEM_LIMIT),
        name="compress",
    )(pf, pe_rows, w1, w2)


def _nsa_kernel(q_ref, slc_ref, win_ref, kc_ref, gate_ref, z_ref, ovl_ref, tri_ref, band_ref, o_ref,
                ksel_ref, vtsel_ref, kwin_ref, vtwin_ref, kct_ref, acc_ref, s_ref):
    tq = q_ref.shape[0] // NSA_TILES
    seq = slc_ref.shape[0]
    tk = s_ref.shape[2]
    tw = band_ref.shape[1]
    nh = N_HEADS
    n_slc = seq // SEL_BLOCK
    step = pl.program_id(1)

    @pl.when(step == 0)
    def _():
        lane_k = lax.broadcasted_iota(jnp.int32, (tk, LANES), 1)
        for j in range(seq // tk):
            rows = slice(j * tk, (j + 1) * tk)
            blk = slc_ref[rows, :]
            key_blk = (j * tk + lax.broadcasted_iota(jnp.int32, (tk, LANES), 0)) // SEL_BLOCK
            onehot = jnp.where(lane_k - HEAD_DIM == key_blk, 1.0, 0.0).astype(BF16)
            ksel_ref[rows, :] = jnp.where(lane_k < HEAD_DIM, blk, onehot)
            vtsel_ref[j] = blk.astype(F32).T[HEAD_DIM:].astype(BF16)
        lane_w = lax.broadcasted_iota(jnp.int32, (WIN, LANES), 1)
        kwin_ref[0:WIN, :] = jnp.where(lane_w == HEAD_DIM, 1.0, 0.0).astype(BF16)
        for j in range(WIN // tw):
            vtwin_ref[j] = jnp.zeros((HEAD_DIM, tw), BF16)
        lane_t = lax.broadcasted_iota(jnp.int32, (tw, LANES), 1)
        for j in range(seq // tw):
            blk = win_ref[j * tw:(j + 1) * tw, :]
            kwin_ref[WIN + j * tw:WIN + (j + 1) * tw, :] = jnp.where(lane_t < HEAD_DIM, blk, jnp.zeros_like(blk))
            vtwin_ref[WIN // tw + j] = blk.astype(F32).T[HEAD_DIM:].astype(BF16)
        kct_ref[...] = kc_ref[...].astype(F32).T[HEAD_DIM:].astype(BF16)

    def tile_program(t_u):
        qi = step * NSA_TILES + t_u
        q0 = qi * tq
        rows_u = slice(t_u * tq, (t_u + 1) * tq)
        q_t = q_ref[rows_u, :].astype(F32).T

        def q_operand(extra):
            return jnp.concatenate([jnp.concatenate([q_t[h * HEAD_DIM:(h + 1) * HEAD_DIM], extra], axis=0)
                                    for h in range(nh)], axis=1).astype(BF16)

        def tile4(x):
            return jnp.concatenate([x] * nh, axis=1)


        span = WIN + tw
        extra_row = lax.broadcasted_iota(jnp.int32, (HEAD_DIM, tq), 0)
        q_win = q_operand(jnp.where(extra_row == 0, NEG, 0.0))
        n_sub = tq // tw

        def window_scores(u):
            q_sub = jnp.concatenate([q_win[:, h * tq + u * tw:h * tq + (u + 1) * tw] for h in range(nh)], axis=1)
            start = pl.multiple_of(q0 + u * tw, tw)
            return _dot(kwin_ref[pl.ds(start, span), :], q_sub)

        kc = kc_ref[...]
        s_c = _dot(kc, q_operand(jnp.zeros((HEAD_DIM, tq), F32)))
        s_win = [window_scores(0)]
        yield

        def window_softmax(u):
            edge = [jnp.concatenate([band_ref[e]] * nh, axis=1) for e in range(2)]
            s_w = s_win[u]
            s_w = jnp.concatenate([s_w[0:tw] + edge[0], s_w[tw:span - tw], s_w[span - tw:] + edge[1]], axis=0)
            e_w = jnp.exp2(s_w - jnp.max(s_w, axis=0, keepdims=True))
            t0 = qi * n_sub + u
            vt_w = jnp.concatenate([vtwin_ref[t0 + d] for d in range(span // tw)], axis=1)
            return _dot(vt_w, e_w.astype(BF16)) / jnp.sum(e_w, axis=0, keepdims=True)

        n_cmp = (seq - CMP_LEN) // CMP_STRIDE + 1
        n_idx = lax.broadcasted_iota(jnp.int32, (LANES, tq), 0)
        t_idx = q0 + lax.broadcasted_iota(jnp.int32, (LANES, tq), 1)
        valid_c = tile4((n_idx * CMP_STRIDE + (CMP_LEN - 1) <= t_idx) & (n_idx < n_cmp))
        s_c = jnp.where(valid_c, s_c, NEG)
        e_c = jnp.exp2(s_c - jnp.max(s_c, axis=0, keepdims=True))
        p_c = jnp.where(valid_c, e_c / jnp.sum(e_c, axis=0, keepdims=True), 0.0)
        o_cmp = _dot(kct_ref[...], p_c.astype(BF16))

        psum = p_c[:, 0:tq]
        for h in range(1, nh):
            psum = psum + p_c[:, h * tq:(h + 1) * tq]
        p_hi = psum.astype(BF16)
        r1 = psum - p_hi.astype(F32)
        p_mid = r1.astype(BF16)
        p_lo = (r1 - p_mid.astype(F32)).astype(BF16)
        ovl = ovl_ref[...]
        imp = _dot(ovl, p_hi) + _dot(ovl, p_mid) + _dot(ovl, p_lo)
        yield

        o_win = []
        for u in range(n_sub - 1):
            s_win.append(window_scores(u + 1))
            o_win.append(window_softmax(u))
        yield

        v = imp[0:n_slc]
        jj = lax.broadcasted_iota(jnp.int32, (n_slc, tq), 0)
        cur = (q0 + lax.broadcasted_iota(jnp.int32, (n_slc, tq), 1)) // SEL_BLOCK
        forced = (jj == 0) | (jj == cur) | (jj == cur - 1)
        v = jnp.where(forced, FORCE, jnp.where(jj <= cur, v, -FORCE))
        sub = 8
        ranks = []
        row8 = lax.broadcasted_iota(jnp.int32, (sub, tq), 0)
        for r0 in range(0, n_slc, sub):
            vr = v[r0:r0 + sub]
            cnt = jnp.zeros((sub, tq), F32)
            for j2 in range(n_slc):
                vj = v[j2:j2 + 1, :]
                if j2 < r0:
                    ahead = vj >= vr
                elif j2 >= r0 + sub:
                    ahead = vj > vr
                else:
                    ahead = (vj > vr) | ((vj == vr) & (row8 > j2 - r0))
                cnt = cnt + jnp.where(ahead, 1.0, 0.0)
            ranks.append(cnt)
        rank = jnp.concatenate(ranks, axis=0)
        sel_neg = jnp.where((rank < min(SEL_TOPK, n_slc)) & (jj <= cur), 0.0, NEG)
        q_sel = q_operand(jnp.concatenate([sel_neg, jnp.zeros((HEAD_DIM - n_slc, tq), F32)], axis=0))
        yield

        n_grp = s_ref.shape[1]
        gw = nh * tq // n_grp

        def scores(c, g):
            keys = ksel_ref[pl.ds(pl.multiple_of(c * tk, tk), tk), :]
            s = _dot(keys, q_sel[:, g * gw:(g + 1) * gw])
            s_ref[t_u, g] = s
            return jnp.max(s, axis=0, keepdims=True)

        def absorb(c, g, s, m_tile, m, l):
            m_new = jnp.maximum(m, m_tile)
            alpha = jnp.exp2(m - m_new)
            p = jnp.exp2(s - m_new)
            l = alpha * l + jnp.sum(p, axis=0, keepdims=True)
            acc_ref[t_u, g] = alpha * acc_ref[t_u, g] + _dot(vtsel_ref[c], p.astype(BF16))
            return m_new, l

        n_chunks = (q0 + tq + tk - 1) // tk
        acc_ref[t_u] = jnp.zeros(acc_ref.shape[1:], F32)
        first_max = [scores(0, g) for g in range(n_grp)]
        o_win.append(window_softmax(n_sub - 1))
        yield

        def sel_step(c, carry):
            out = []
            for g in range(n_grp):
                m, l, m_tile = carry[g]
                s = s_ref[t_u, g]
                m_tile_next = scores(c + 1, g)
                out.append(absorb(c, g, s, m_tile, m, l) + (m_tile_next,))
            return tuple(out)

        stats = tuple((jnp.full((1, gw), NEG, F32), jnp.zeros((1, gw), F32), first_max[g]) for g in range(n_grp))
        stats = lax.fori_loop(0, n_chunks - 1, sel_step, stats)
        yield
        last = n_chunks - 1
        tri = tri_ref[(q0 - last * tk) // tq]
        tri_g = jnp.concatenate([tri] * (gw // tq), axis=1)
        l_last = []
        for g in range(n_grp):
            s_last = s_ref[t_u, g] + tri_g
            l_last.append(absorb(last, g, s_last, jnp.max(s_last, axis=0, keepdims=True), *stats[g][:2])[1])
        yield

        gate_t = _sigmoid(gate_ref[rows_u, :]).T
        z_gate = _silu(z_ref[rows_u, :])
        partial = []
        for h in range(nh):
            cols = slice(h * tq, (h + 1) * tq)
            o_win_h = jnp.concatenate([ow[:, h * tw:(h + 1) * tw] for ow in o_win], axis=1)
            partial.append(gate_t[0 * nh + h:0 * nh + h + 1, :] * o_cmp[:, cols]
                           + gate_t[2 * nh + h:2 * nh + h + 1, :] * o_win_h)
        o_slc = jnp.concatenate([acc_ref[t_u, g] / l_last[g] for g in range(n_grp)], axis=1)
        combs = [partial[h] + gate_t[1 * nh + h:1 * nh + h + 1, :] * o_slc[:, h * tq:(h + 1) * tq] for h in range(nh)]
        for c in range(nh // 2):
            blk = jnp.concatenate([combs[2 * c], combs[2 * c + 1]], axis=0).T
            o_ref[rows_u, c * LANES:(c + 1) * LANES] = (blk * z_gate[:, c * LANES:(c + 1) * LANES]).astype(o_ref.dtype)

    programs = [tile_program(t_u) for t_u in range(NSA_TILES)]
    while programs:
        for prog in list(programs):
            try:
                next(prog)
            except StopIteration:
                programs.remove(prog)


def _nsa(pb, pf, kcvc, ovl, tri, band, nb, seq):
    tq = TQ_NSA * NSA_TILES
    tk = TK_SEL
    nq = seq // tq
    return pl.pallas_call(
        _nsa_kernel,
        grid=(nb, nq),
        in_specs=[
            pl.BlockSpec((tq, 2 * LANES), lambda b, i: (b * nq + i, PB_AQ // 2)),
            pl.BlockSpec((seq, LANES), lambda b, i: (b, PB_SLC)),
            pl.BlockSpec((seq, LANES), lambda b, i: (b, PB_WIN)),
            pl.BlockSpec((None, LANES, LANES), lambda b, i: (b, 0, 0)),
            pl.BlockSpec((tq, LANES), lambda b, i: (b * nq + i, PF_AG)),
            pl.BlockSpec((tq, 2 * LANES), lambda b, i: (b * nq + i, PF_AZ // 2)),
            pl.BlockSpec((LANES, LANES), lambda b, i: (0, 0)),
            pl.BlockSpec(tri.shape, lambda b, i: (0, 0, 0)),
            pl.BlockSpec(band.shape, lambda b, i: (0, 0, 0)),
        ],
        out_specs=pl.BlockSpec((tq, GROUP_W), lambda b, i: (b * nq + i, 0)),
        out_shape=jax.ShapeDtypeStruct((nb * seq, GROUP_W), BF16),
        scratch_shapes=[
            pltpu.VMEM((seq, LANES), BF16),
            pltpu.VMEM((seq // tk, HEAD_DIM, tk), BF16),
            pltpu.VMEM((WIN + seq, LANES), BF16),
            pltpu.VMEM(((WIN + seq) // TW_WIN, HEAD_DIM, TW_WIN), BF16),
            pltpu.VMEM((HEAD_DIM, LANES), BF16),
            pltpu.VMEM((NSA_TILES, SEL_GROUPS, HEAD_DIM, N_HEADS * TQ_NSA // SEL_GROUPS), F32),
            pltpu.VMEM((NSA_TILES, SEL_GROUPS, tk, N_HEADS * TQ_NSA // SEL_GROUPS), F32),
        ],
        compiler_params=pltpu.CompilerParams(
            dimension_semantics=("arbitrary", "arbitrary"), vmem_limit_bytes=VMEM_LIMIT),
        name="nsa",
    )(pb, pb, pb, kcvc, pf, pf, ovl, tri, band)


def _dilated_kernel(q_ref, qnext_ref, k_ref, v_ref, z_ref, bias_ref, o_ref, vt_ref, acc_ref, s_ref, mt_ref):
    t = q_ref.shape[0]
    seq = k_ref.shape[0]
    n_pairs = N_HEADS // 2
    rows_h = DEN_ROWS + HEAD_DIM
    qi = pl.program_id(1)

    @pl.when(qi == 0)
    def _():
        ones_rows = jnp.where(lax.broadcasted_iota(jnp.int32, (DEN_ROWS, t), 0) == 0, 1.0, 0.0).astype(BF16)
        for j in range(seq // t):
            v_t = v_ref[j * t:(j + 1) * t, :].astype(F32).T.astype(BF16)
            for h in range(N_HEADS):
                vt_ref[j, h * rows_h:h * rows_h + DEN_ROWS] = ones_rows
                vt_ref[j, h * rows_h + DEN_ROWS:(h + 1) * rows_h] = v_t[h * HEAD_DIM:(h + 1) * HEAD_DIM]

    upper = lax.broadcasted_iota(jnp.int32, (LANES, t), 0) >= HEAD_DIM

    def query_operands(ref):
        q_t = ref[...].astype(F32).T
        ops = []
        for c in range(n_pairs):
            blk = q_t[c * LANES:(c + 1) * LANES]
            ops.append(jnp.concatenate([jnp.where(upper, 0.0, blk), jnp.where(upper, blk, 0.0)], axis=1).astype(BF16))
        return ops

    qs = query_operands(q_ref)
    qs_next = query_operands(qnext_ref)
    acc_ref[...] = jnp.zeros_like(acc_ref)

    def scores(j, c, q_op, q_tile):
        k0 = pl.multiple_of(j * t, t)
        bias = bias_ref[jnp.minimum(q_tile - j, bias_ref.shape[0] - 1)]
        s = _dot(k_ref[pl.ds(k0, t), c * LANES:(c + 1) * LANES], q_op) + jnp.concatenate([bias, bias], axis=1)
        s_ref[c] = s
        return jnp.max(s, axis=0, keepdims=True)

    @pl.when(qi == 0)
    def _():
        for c in range(n_pairs):
            mt_ref[c] = scores(0, c, qs[c], qi)

    def step(j, carry):
        is_last = j == qi
        j_next = jnp.where(is_last, 0, j + 1)
        q_tile = jnp.where(is_last, qi + 1, qi)
        out = []
        for c in range(n_pairs):
            m, m_tile = carry[c]
            s = s_ref[c]
            m_tile_next = scores(j_next, c, jnp.where(is_last, qs_next[c], qs[c]), q_tile)
            m_new = jnp.maximum(m, m_tile)
            alpha = jnp.exp2(m - m_new)
            pb = jnp.exp2(s - m_new).astype(BF16)
            for e in range(2):
                h = 2 * c + e
                pv = _dot(vt_ref[j, h * rows_h:(h + 1) * rows_h, :], pb[:, e * t:(e + 1) * t])
                acc_ref[h] = alpha[:, e * t:(e + 1) * t] * acc_ref[h] + pv
            out.append((m_new, m_tile_next))
        return tuple(out)

    init = tuple((jnp.full((1, 2 * t), NEG, F32), mt_ref[c]) for c in range(n_pairs))
    final = lax.fori_loop(0, qi + 1, step, init)
    for c in range(n_pairs):
        mt_ref[c] = final[c][1]
    for c in range(n_pairs):
        heads = [acc_ref[2 * c + e] for e in range(2)]
        o_pair = jnp.concatenate([a[DEN_ROWS:] / a[0:1] for a in heads], axis=0).T
        zc = z_ref[:, c * LANES:(c + 1) * LANES]
        o_ref[:, c * LANES:(c + 1) * LANES] = (o_pair * _silu(zc)).astype(o_ref.dtype)


def _dilated(pb, pf, bias_tab, nb, seq):
    t = T_DIL
    nq = seq // t
    return pl.pallas_call(
        _dilated_kernel,
        grid=(nb, nq),
        in_specs=[
            pl.BlockSpec((t, GROUP_W), lambda b, i: (b * nq + i, PB_BQ // 2)),
            pl.BlockSpec((t, GROUP_W), lambda b, i: (b * nq + jnp.minimum(i + 1, nq - 1), PB_BQ // 2)),
            pl.BlockSpec((seq, GROUP_W), lambda b, i: (b, PB_BK // 2)),
            pl.BlockSpec((seq, GROUP_W), lambda b, i: (b, PB_BV // 2)),
            pl.BlockSpec((t, GROUP_W), lambda b, i: (b * nq + i, PF_BZ // 2)),
            pl.BlockSpec((nq, t, t), lambda b, i: (0, 0, 0)),
        ],
        out_specs=pl.BlockSpec((t, GROUP_W), lambda b, i: (b * nq + i, 0)),
        out_shape=jax.ShapeDtypeStruct((nb * seq, GROUP_W), BF16),
        scratch_shapes=[
            pltpu.VMEM((nq, N_HEADS * (DEN_ROWS + HEAD_DIM), t), BF16),
            pltpu.VMEM((N_HEADS, DEN_ROWS + HEAD_DIM, t), F32),
            pltpu.VMEM((N_HEADS // 2, t, 2 * t), F32),
            pltpu.VMEM((N_HEADS // 2, 1, 2 * t), F32),
        ],
        compiler_params=pltpu.CompilerParams(
            dimension_semantics=("arbitrary", "arbitrary"), vmem_limit_bytes=VMEM_LIMIT),
        name="dilated",
    )(pb, pb, pb, pb, pf, bias_tab)


def _pool_sgu_kernel(c_ref, cprev_ref, cz_ref, du_ref, dv_ref, dz_ref,
                     wpool_ref, pscale_ref, lng_ref, lnb_ref, wsp_ref, bsp_ref, yc_ref, yd_ref):
    ts = c_ref.shape[0]
    i = pl.program_id(1)
    lane = lax.broadcasted_iota(jnp.int32, (ts, GROUP_W), 1)

    cur = c_ref[...]
    prev = jnp.where(i > 0, cprev_ref[...], 0.0)
    ext = jnp.concatenate([prev, cur], axis=0)
    t_head = i * ts + lax.broadcasted_iota(jnp.int32, (POOL_HALO, 1), 0)
    pooled = None
    acc = ext
    width = 1
    for g, w in enumerate(POOL_SIZES):
        while width < w:
            acc = acc + pltpu.roll(acc, width, 0)
            width *= 2
        inv_head = 1.0 / jnp.minimum(t_head + 1, w).astype(F32)
        mean_w = jnp.concatenate([acc[POOL_HALO:2 * POOL_HALO] * inv_head,
                                  acc[2 * POOL_HALO:POOL_HALO + ts] * (1.0 / w)], axis=0)
        pooled = mean_w if pooled is None else jnp.where(lane >= g * HEAD_DIM, mean_w, pooled)
    pooled = pooled - cur
    mixed = _dot(pooled.astype(BF16), wpool_ref[...]) * pscale_ref[...]
    yc_ref[...] = (mixed * _silu(cz_ref[...])).astype(yc_ref.dtype)

    v = dv_ref[...]
    mu = jnp.mean(v, axis=-1, keepdims=True)
    var = jnp.mean(jnp.square(v - mu), axis=-1, keepdims=True)
    vn = ((v - mu) * lax.rsqrt(var + LN_EPS) * lng_ref[...] + lnb_ref[...]).astype(BF16)
    r = lax.broadcasted_iota(jnp.int32, (SG_CHUNK, SG_CHUNK), 0)
    cidx = lax.broadcasted_iota(jnp.int32, (SG_CHUNK, SG_CHUNK), 1)
    w_tril = [jnp.where(cidx <= r, wsp_ref[g], 0.0).astype(BF16) for g in range(N_HEADS)]
    lane_c = lax.broadcasted_iota(jnp.int32, (SG_CHUNK, GROUP_W), 1)
    for ci in range(ts // SG_CHUNK):
        rows = slice(ci * SG_CHUNK, (ci + 1) * SG_CHUNK)
        vc = vn[rows]
        zmix = _dot(w_tril[0], vc)
        for g in range(1, N_HEADS):
            zmix = jnp.where(lane_c >= g * HEAD_DIM, _dot(w_tril[g], vc), zmix)
        zfull = zmix + bsp_ref[...]
        yd_ref[rows, :] = (du_ref[rows, :] * zfull * _silu(dz_ref[rows, :])).astype(yd_ref.dtype)


def _pool_sgu(pf, wpool_bd, pool_scale, ln_g, ln_b, w_sp, bsp_exp, layer, nb, seq):
    ts = TS_POOL
    ns = seq // ts
    halo_blocks = ts // POOL_HALO

    def tile(col):
        return pl.BlockSpec((ts, GROUP_W), lambda b, i: (b * ns + i, col // 2))

    def per_layer(shape):
        nd = len(shape)
        return pl.BlockSpec((None,) + shape, lambda b, i: (layer,) + (0,) * nd)

    return pl.pallas_call(
        _pool_sgu_kernel,
        grid=(nb, ns),
        in_specs=[
            tile(PF_CIN),
            pl.BlockSpec((POOL_HALO, GROUP_W),
                         lambda b, i: (jnp.maximum((b * ns + i) * halo_blocks - 1, 0), PF_CIN // 2)),
            tile(PF_CZ), tile(PF_DU), tile(PF_DV), tile(PF_DZ),
            per_layer((GROUP_W, GROUP_W)),
            per_layer((1, GROUP_W)),
            per_layer((1, GROUP_W)),
            per_layer((1, GROUP_W)),
            per_layer((N_HEADS, SG_CHUNK, SG_CHUNK)),
            per_layer((SG_CHUNK, GROUP_W)),
        ],
        out_specs=[
            pl.BlockSpec((ts, GROUP_W), lambda b, i: (b * ns + i, 0)),
            pl.BlockSpec((ts, GROUP_W), lambda b, i: (b * ns + i, 0)),
        ],
        out_shape=[
            jax.ShapeDtypeStruct((nb * seq, GROUP_W), BF16),
            jax.ShapeDtypeStruct((nb * seq, GROUP_W), BF16),
        ],
        compiler_params=pltpu.CompilerParams(
            dimension_semantics=("arbitrary", "arbitrary"), vmem_limit_bytes=VMEM_LIMIT),
        name="pool_sgu",
    )(pf, pf, pf, pf, pf, pf, wpool_bd, pool_scale, ln_g, ln_b, w_sp, bsp_exp)


def _project_out(ya_ref, yb_ref, yc_ref, yd_ref, w_ref, x_ref, g_ref):
    y = jnp.concatenate([ya_ref[...], yb_ref[...], yc_ref[...], yd_ref[...]], axis=1)
    out = _dot(y, w_ref[...])
    inv = lax.rsqrt(jnp.mean(out * out, axis=-1, keepdims=True) + RMS_EPS)
    return x_ref[...] + out * inv * g_ref[...]


def _out_proj_kernel(ya_ref, yb_ref, yc_ref, yd_ref, w_ref, x_ref, g_ref, o_ref):
    o_ref[...] = _project_out(ya_ref, yb_ref, yc_ref, yd_ref, w_ref, x_ref, g_ref)


def _out_in_proj_kernel(ya_ref, yb_ref, yc_ref, yd_ref, wo_ref, x_ref, gpost_ref,
                        gpre_ref, wa_ref, wg_ref, wb_ref, o_ref, pb_ref, pf_ref):
    half = o_ref.shape[0] // 2
    x_new = []
    for r in range(2):
        rows = pl.ds(r * half, half)
        x_new.append(_project_out(ya_ref.at[rows], yb_ref.at[rows], yc_ref.at[rows], yd_ref.at[rows],
                                  wo_ref, x_ref.at[rows], gpost_ref))
        o_ref[rows, :] = x_new[r]
    for r in range(2):
        rows = pl.ds(r * half, half)
        _project_in(x_new[r], gpre_ref, wa_ref, wg_ref, wb_ref, pb_ref.at[rows], pf_ref.at[rows])


def _out_in_proj(ya, yb, yc, yd, w_out, x2, g_post, g_pre, w_a, w_g, w_b, layer):
    m, d = x2.shape
    tm = TM_OUT
    ytile = pl.BlockSpec((tm, GROUP_W), lambda i: (i, 0))
    nxt = layer + 1
    return pl.pallas_call(
        _out_in_proj_kernel,
        grid=(m // tm,),
        in_specs=[
            ytile, ytile, ytile, ytile,
            pl.BlockSpec((None, 4 * GROUP_W, d), lambda i: (layer, 0, 0)),
            pl.BlockSpec((tm, d), lambda i: (i, 0)),
            pl.BlockSpec((None, 1, d), lambda i: (layer, 0, 0)),
            pl.BlockSpec((None, 1, d), lambda i: (nxt, 0, 0)),
            pl.BlockSpec((None, d, w_a.shape[2]), lambda i: (nxt, 0, 0)),
            pl.BlockSpec((None, d, w_g.shape[2]), lambda i: (nxt, 0, 0)),
            pl.BlockSpec((None, d, w_b.shape[2]), lambda i: (nxt, 0, 0)),
        ],
        out_specs=[
            pl.BlockSpec((tm, d), lambda i: (i, 0)),
            pl.BlockSpec((tm, PB_COLS), lambda i: (i, 0)),
            pl.BlockSpec((tm, PF_COLS), lambda i: (i, 0)),
        ],
        out_shape=[
            jax.ShapeDtypeStruct((m, d), F32),
            jax.ShapeDtypeStruct((m, PB_COLS), BF16),
            jax.ShapeDtypeStruct((m, PF_COLS), F32),
        ],
        compiler_params=pltpu.CompilerParams(
            dimension_semantics=("arbitrary",), vmem_limit_bytes=VMEM_LIMIT_FUSED),
        name="out_in_proj",
    )(ya, yb, yc, yd, w_out, x2, g_post, g_pre, w_a, w_g, w_b)


def _out_proj(ya, yb, yc, yd, w_out, x2, g_post, layer):
    m, d = x2.shape
    tm = TM_LAST
    ytile = pl.BlockSpec((tm, GROUP_W), lambda i: (i, 0))
    return pl.pallas_call(
        _out_proj_kernel,
        grid=(m // tm,),
        in_specs=[
            ytile, ytile, ytile, ytile,
            pl.BlockSpec((None, 4 * GROUP_W, d), lambda i: (layer, 0, 0)),
            pl.BlockSpec((tm, d), lambda i: (i, 0)),
            pl.BlockSpec((None, 1, d), lambda i: (layer, 0, 0)),
        ],
        out_specs=pl.BlockSpec((tm, d), lambda i: (i, 0)),
        out_shape=jax.ShapeDtypeStruct((m, d), F32),
        compiler_params=pltpu.CompilerParams(
            dimension_semantics=("arbitrary",), vmem_limit_bytes=VMEM_LIMIT),
        name="out_proj",
    )(ya, yb, yc, yd, w_out, x2, g_post)


def _overlap_t(seq):
    n_cmp = (seq - CMP_LEN) // CMP_STRIDE + 1
    n_slc = seq // SEL_BLOCK
    cs = np.arange(n_cmp) * CMP_STRIDE
    ss = np.arange(n_slc) * SEL_BLOCK
    ov = (cs[None, :] < ss[:, None] + SEL_BLOCK) & (cs[None, :] + CMP_LEN > ss[:, None])
    out = np.zeros((LANES, LANES), np.float32)
    out[:n_slc, :n_cmp] = ov
    return out


def _causal_tri():
    a = np.arange(TK_SEL)[:, None]
    b = np.arange(TQ_NSA)[None, :]
    return np.stack([np.where(a - o * TQ_NSA <= b, 0.0, NEG) for o in range(TK_SEL // TQ_NSA)]).astype(np.float32)


def _window_band():
    a = np.arange(TW_WIN)[:, None]
    b = np.arange(TW_WIN)[None, :]
    return np.stack([np.where(a > b, 0.0, NEG), np.where(a <= b, 0.0, NEG)]).astype(np.float32)


def _dilated_log_multiplicity(seq):
    t = T_DIL
    d0 = np.arange(t)[:, None] - np.arange(t)[None, :]
    tabs = []
    for delta in range(seq // t):
        d = d0 + delta * t
        mult = np.zeros_like(d)
        for window, dil in DILATED_PAIRS:
            mult += (d >= 0) & (d % dil == 0) & (d // dil <= window // dil) & (d // dil <= seq // dil - 1)
        tabs.append(np.where(mult > 0, np.log2(np.maximum(mult, 1)), NEG).T)
    return np.stack(tabs).astype(np.float32)


def _split_w_in(w_in):
    gw = GROUP_W
    n_a = gw + 6 * HEAD_DIM
    n_b = 10 * gw
    assert w_in.shape[-1] == n_a + N_GATES + n_b, w_in.shape
    scale_a = np.ones((n_a,), np.float32)
    scale_a[:gw] = Q_SCALE
    scale_b = np.ones((n_b,), np.float32)
    scale_b[gw:2 * gw] = Q_SCALE
    w_a = (w_in[..., :n_a] * scale_a).astype(BF16)
    w_g = jnp.pad(w_in[..., n_a:n_a + N_GATES], ((0, 0), (0, 0), (0, LANES - N_GATES))).astype(BF16)
    w_b = (w_in[..., n_a + N_GATES:] * scale_b).astype(BF16)
    return w_a, w_g, w_b


def kernel(x, g_pre, w_in, pe_cmp, w_cmp1, w_cmp2, w_pool, pool_scale, sg_ln_g, sg_ln_b, w_sp, b_sp, w_out, g_post):
    nb, seq, d = x.shape
    depth = w_in.shape[0]
    assert seq % TK_SEL == 0 and seq % T_DIL == 0 and seq % TS_POOL == 0
    assert all((nb * seq) % tm == 0 for tm in (TM_PROJ, TM_OUT, TM_LAST))
    assert TQ_NSA % TW_WIN == 0 and TK_SEL % TQ_NSA == 0 and WIN % TW_WIN == 0 and TW_WIN % LANES == 0
    assert seq % (TQ_NSA * NSA_TILES) == 0
    assert seq // SEL_BLOCK <= LANES - HEAD_DIM and (seq - CMP_LEN) // CMP_STRIDE + 1 < LANES
    assert seq % CMP_STRIDE == 0 and CMP_LEN == 2 * CMP_STRIDE

    w_a, w_g, w_b = _split_w_in(w_in)
    w_out_b = w_out.astype(BF16)
    w1_l = w_cmp1.reshape(depth, 2, CMP_LEN, HEAD_DIM, CMP_HIDDEN)
    w2_b = w_cmp2.astype(BF16)
    pe_rows = jnp.concatenate([pe_cmp[:, 0], pe_cmp[:, 1]], axis=-1)
    eye = jnp.eye(N_HEADS, dtype=w_pool.dtype)
    wpool_bd = jnp.einsum('lgcd,gh->lgchd', w_pool, eye).reshape(depth, GROUP_W, GROUP_W).astype(BF16)
    bsp_exp = jnp.repeat(jnp.swapaxes(b_sp, 1, 2), HEAD_DIM, axis=2)
    g_pre3 = g_pre.reshape(depth, 1, d)
    g_post3 = g_post.reshape(depth, 1, d)
    pscale3 = pool_scale.reshape(depth, 1, GROUP_W)
    lng3 = sg_ln_g.reshape(depth, 1, GROUP_W)
    lnb3 = sg_ln_b.reshape(depth, 1, GROUP_W)
    ovl_t = jnp.asarray(_overlap_t(seq), BF16)
    tri = jnp.asarray(_causal_tri())
    band = jnp.asarray(_window_band())
    dil_bias = jnp.asarray(_dilated_log_multiplicity(seq))

    x2 = x.reshape(nb * seq, d)
    pb, pf = _in_proj(x2, g_pre3, w_a, w_g, w_b, 0)
    for layer in range(depth):
        kcvc = _compress(pf, pe_rows, w1_l, w2_b, layer, nb, seq)
        ya = _nsa(pb, pf, kcvc, ovl_t, tri, band, nb, seq)
        yb = _dilated(pb, pf, dil_bias, nb, seq)
        yc, yd = _pool_sgu(pf, wpool_bd, pscale3, lng3, lnb3, w_sp, bsp_exp, layer, nb, seq)
        if layer + 1 < depth:
            x2, pb, pf = _out_in_proj(ya, yb, yc, yd, w_out_b, x2, g_post3, g_pre3, w_a, w_g, w_b, layer)
        else:
            x2 = _out_proj(ya, yb, yc, yd, w_out_b, x2, g_post3, layer)
    return x2.reshape(nb, seq, d)
```

```python
import numpy as np
import jax
import jax.numpy as jnp
from jax import lax
from jax.experimental import pallas as pl
from jax.experimental.pallas import tpu as pltpu

F32 = jnp.float32
BF16 = jnp.bfloat16

HEAD_DIM = 64
N_HEADS = 4
GROUP_W = N_HEADS * HEAD_DIM
CMP_LEN = 32
CMP_STRIDE = 16
CMP_HIDDEN = 256
SEL_BLOCK = 64
SEL_TOPK = 16
WIN = 512
FORCE = 1e4
DILATED_PAIRS = ((128, 1), (512, 4), (2048, 16))
POOL_SIZES = (2, 4, 8, 16)
SG_CHUNK = 128
RMS_EPS = 1e-6
LN_EPS = 1e-5
NEG = -1e30

LANES = 128
V7X_VMEM_BYTES = 64 * 1024 * 1024
VMEM_LIMIT = V7X_VMEM_BYTES * 3 // 4
VMEM_LIMIT_FUSED = V7X_VMEM_BYTES * 7 // 8
N_GATES = 3 * N_HEADS

PB_AQ, PB_SLC, PB_WIN, PB_BQ, PB_BK, PB_BV = 0, 2, 3, 4, 6, 8
PB_COLS = 10 * LANES
PF_CMP, PF_AG, PF_AZ, PF_BZ, PF_CIN, PF_CZ, PF_DU, PF_DV, PF_DZ = 0, 1, 2, 4, 6, 8, 10, 12, 14
PF_COLS = 16 * LANES

TM_PROJ = 512
TM_OUT = 512
TM_LAST = 1024
TQ_NSA = 256
TK_SEL = 256
T_DIL = 256
TS_POOL = 1024
POOL_HALO = 16
TW_WIN = 128
SEL_GROUPS = 2
NSA_TILES = 8
DEN_ROWS = 16

Q_SCALE = float(np.log2(np.e)) * HEAD_DIM ** -0.5


def _sigmoid(x):
    return 0.5 * jnp.tanh(0.5 * x) + 0.5


def _silu(x):
    return x * _sigmoid(x)


def _dot(a, b):
    return jnp.dot(a, b, preferred_element_type=F32)


def _in_proj_kernel(x_ref, g_ref, wa_ref, wg_ref, wb_ref, pb_ref, pf_ref):
    half = x_ref.shape[0] // 2
    for r in range(2):
        rows = pl.ds(r * half, half)
        _project_in(x_ref[rows, :], g_ref, wa_ref, wg_ref, wb_ref, pb_ref.at[rows], pf_ref.at[rows])


def _project_in(x, g_ref, wa_ref, wg_ref, wb_ref, pb_ref, pf_ref):
    inv = lax.rsqrt(jnp.mean(x * x, axis=-1, keepdims=True) + RMS_EPS)
    h = (x * inv * g_ref[...]).astype(BF16)
    cw = 2 * LANES

    def put(dst, c, w):
        dst[:, c * cw:(c + 1) * cw] = _dot(h, w).astype(dst.dtype)

    put(pb_ref, PB_AQ // 2, wa_ref[:, 0:cw])
    put(pb_ref, PB_SLC // 2, wa_ref[:, cw + LANES:2 * cw + LANES])
    for c in range(3):
        put(pb_ref, PB_BQ // 2 + c, wb_ref[:, (1 + c) * cw:(2 + c) * cw])
    put(pf_ref, PF_CMP // 2, jnp.concatenate([wa_ref[:, cw:cw + LANES], wg_ref[...]], axis=1))
    put(pf_ref, PF_AZ // 2, wb_ref[:, 0:cw])
    for c in range(6):
        put(pf_ref, PF_BZ // 2 + c, wb_ref[:, (4 + c) * cw:(5 + c) * cw])


def _in_proj(x2, g_pre, w_a, w_g, w_b, layer):
    m, d = x2.shape
    return pl.pallas_call(
        _in_proj_kernel,
        grid=(m // TM_PROJ,),
        in_specs=[
            pl.BlockSpec((TM_PROJ, d), lambda i: (i, 0)),
            pl.BlockSpec((None, 1, d), lambda i: (layer, 0, 0)),
            pl.BlockSpec((None, d, w_a.shape[2]), lambda i: (layer, 0, 0)),
            pl.BlockSpec((None, d, w_g.shape[2]), lambda i: (layer, 0, 0)),
            pl.BlockSpec((None, d, w_b.shape[2]), lambda i: (layer, 0, 0)),
        ],
        out_specs=[
            pl.BlockSpec((TM_PROJ, PB_COLS), lambda i: (i, 0)),
            pl.BlockSpec((TM_PROJ, PF_COLS), lambda i: (i, 0)),
        ],
        out_shape=[
            jax.ShapeDtypeStruct((m, PB_COLS), BF16),
            jax.ShapeDtypeStruct((m, PF_COLS), F32),
        ],
        compiler_params=pltpu.CompilerParams(
            dimension_semantics=("arbitrary",), vmem_limit_bytes=VMEM_LIMIT),
        name="in_proj",
    )(x2, g_pre, w_a, w_g, w_b)


def _gelu_tanh(x):
    return 0.5 * x * (1.0 + jnp.tanh(np.sqrt(2.0 / np.pi).astype(np.float32) * (x + 0.044715 * (x * x * x))))


def _compress_kernel(x_ref, pe_ref, w1_ref, w2_ref, o_ref):
    n_grp = x_ref.shape[0] // CMP_STRIDE
    ng = o_ref.shape[1]
    lo = [None, None]
    hi = [None, None]
    for l0 in range(CMP_STRIDE):
        rows = x_ref[pl.ds(l0, n_grp, stride=CMP_STRIDE), :]
        r_lo = (rows + pe_ref[l0:l0 + 1, :]).astype(BF16)
        r_hi = (rows + pe_ref[CMP_STRIDE + l0:CMP_STRIDE + l0 + 1, :]).astype(BF16)
        for j in range(2):
            cols = slice(j * HEAD_DIM, (j + 1) * HEAD_DIM)
            d_lo = _dot(r_lo[:, cols], w1_ref[j, l0].astype(BF16))
            d_hi = _dot(r_hi[:, cols], w1_ref[j, CMP_STRIDE + l0].astype(BF16))
            lo[j] = d_lo if lo[j] is None else lo[j] + d_lo
            hi[j] = d_hi if hi[j] is None else hi[j] + d_hi
    row = lax.broadcasted_iota(jnp.int32, (n_grp, HEAD_DIM), 0) % ng
    for j in range(2):
        hid = _gelu_tanh(lo[j] + pltpu.roll(hi[j], n_grp - 1, 0)).astype(BF16)
        comp = jnp.where(row < ng - 1, _dot(hid, w2_ref[j]), 0.0)
        o_ref[:, :, j * HEAD_DIM:(j + 1) * HEAD_DIM] = comp.reshape(o_ref.shape[0], ng, HEAD_DIM).astype(BF16)


def _compress(pf, pe_rows, w1, w2, layer, nb, seq):
    ng = seq // CMP_STRIDE
    return pl.pallas_call(
        _compress_kernel,
        grid=(1,),
        in_specs=[
            pl.BlockSpec((nb * seq, LANES), lambda i: (0, PF_CMP)),
            pl.BlockSpec((None, CMP_LEN, LANES), lambda i: (layer, 0, 0)),
            pl.BlockSpec((None, 2, CMP_LEN, HEAD_DIM, CMP_HIDDEN), lambda i: (layer, 0, 0, 0, 0)),
            pl.BlockSpec((None, 2, CMP_HIDDEN, HEAD_DIM), lambda i: (layer, 0, 0, 0)),
        ],
        out_specs=pl.BlockSpec((nb, ng, LANES), lambda i: (0, 0, 0)),
        out_shape=jax.ShapeDtypeStruct((nb, ng, LANES), BF16),
        compiler_params=pltpu.CompilerParams(
            dimension_semantics=("arbitrary",), vmem_limit_bytes=VMEM_LIMIT),
        name="compress",
    )(pf, pe_rows, w1, w2)


def _nsa_kernel(q_ref, slc_ref, win_ref, kc_ref, gate_ref, z_ref, ovl_ref, tri_ref, band_ref, o_ref,
                ksel_ref, vtsel_ref, kwin_ref, vtwin_ref, kct_ref, acc_ref, s_ref):
    tq = q_ref.shape[0] // NSA_TILES
    seq = slc_ref.shape[0]
    tk = s_ref.shape[2]
    tw = band_ref.shape[1]
    nh = N_HEADS
    n_slc = seq // SEL_BLOCK
    step = pl.program_id(1)

    @pl.when(step == 0)
    def _():
        lane_k = lax.broadcasted_iota(jnp.int32, (tk, LANES), 1)
        for j in range(seq // tk):
            rows = slice(j * tk, (j + 1) * tk)
            blk = slc_ref[rows, :]
            key_blk = (j * tk + lax.broadcasted_iota(jnp.int32, (tk, LANES), 0)) // SEL_BLOCK
            onehot = jnp.where(lane_k - HEAD_DIM == key_blk, 1.0, 0.0).astype(BF16)
            ksel_ref[rows, :] = jnp.where(lane_k < HEAD_DIM, blk, onehot)
            vtsel_ref[j] = blk.astype(F32).T[HEAD_DIM:].astype(BF16)
        lane_w = lax.broadcasted_iota(jnp.int32, (WIN, LANES), 1)
        kwin_ref[0:WIN, :] = jnp.where(lane_w == HEAD_DIM, 1.0, 0.0).astype(BF16)
        for j in range(WIN // tw):
            vtwin_ref[j] = jnp.zeros((HEAD_DIM, tw), BF16)
        lane_t = lax.broadcasted_iota(jnp.int32, (tw, LANES), 1)
        for j in range(seq // tw):
            blk = win_ref[j * tw:(j + 1) * tw, :]
            kwin_ref[WIN + j * tw:WIN + (j + 1) * tw, :] = jnp.where(lane_t < HEAD_DIM, blk, jnp.zeros_like(blk))
            vtwin_ref[WIN // tw + j] = blk.astype(F32).T[HEAD_DIM:].astype(BF16)
        kct_ref[...] = kc_ref[...].astype(F32).T[HEAD_DIM:].astype(BF16)

    def tile_program(t_u):
        qi = step * NSA_TILES + t_u
        q0 = qi * tq
        rows_u = slice(t_u * tq, (t_u + 1) * tq)
        q_t = q_ref[rows_u, :].astype(F32).T

        def q_operand(extra):
            return jnp.concatenate([jnp.concatenate([q_t[h * HEAD_DIM:(h + 1) * HEAD_DIM], extra], axis=0)
                                    for h in range(nh)], axis=1).astype(BF16)

        def tile4(x):
            return jnp.concatenate([x] * nh, axis=1)


        span = WIN + tw
        extra_row = lax.broadcasted_iota(jnp.int32, (HEAD_DIM, tq), 0)
        q_win = q_operand(jnp.where(extra_row == 0, NEG, 0.0))
        n_sub = tq // tw

        def window_scores(u):
            q_sub = jnp.concatenate([q_win[:, h * tq + u * tw:h * tq + (u + 1) * tw] for h in range(nh)], axis=1)
            start = pl.multiple_of(q0 + u * tw, tw)
            return _dot(kwin_ref[pl.ds(start, span), :], q_sub)

        kc = kc_ref[...]
        s_c = _dot(kc, q_operand(jnp.zeros((HEAD_DIM, tq), F32)))
        s_win = [window_scores(0)]
        yield

        def window_softmax(u):
            edge = [jnp.concatenate([band_ref[e]] * nh, axis=1) for e in range(2)]
            s_w = s_win[u]
            s_w = jnp.concatenate([s_w[0:tw] + edge[0], s_w[tw:span - tw], s_w[span - tw:] + edge[1]], axis=0)
            e_w = jnp.exp2(s_w - jnp.max(s_w, axis=0, keepdims=True))
            t0 = qi * n_sub + u
            vt_w = jnp.concatenate([vtwin_ref[t0 + d] for d in range(span // tw)], axis=1)
            return _dot(vt_w, e_w.astype(BF16)) / jnp.sum(e_w, axis=0, keepdims=True)

        n_cmp = (seq - CMP_LEN) // CMP_STRIDE + 1
        n_idx = lax.broadcasted_iota(jnp.int32, (LANES, tq), 0)
        t_idx = q0 + lax.broadcasted_iota(jnp.int32, (LANES, tq), 1)
        valid_c = tile4((n_idx * CMP_STRIDE + (CMP_LEN - 1) <= t_idx) & (n_idx < n_cmp))
        s_c = jnp.where(valid_c, s_c, NEG)
        e_c = jnp.exp2(s_c - jnp.max(s_c, axis=0, keepdims=True))
        p_c = jnp.where(valid_c, e_c / jnp.sum(e_c, axis=0, keepdims=True), 0.0)
        o_cmp = _dot(kct_ref[...], p_c.astype(BF16))

        psum = p_c[:, 0:tq]
        for h in range(1, nh):
            psum = psum + p_c[:, h * tq:(h + 1) * tq]
        p_hi = psum.astype(BF16)
        r1 = psum - p_hi.astype(F32)
        p_mid = r1.astype(BF16)
        p_lo = (r1 - p_mid.astype(F32)).astype(BF16)
        ovl = ovl_ref[...]
        imp = _dot(ovl, p_hi) + _dot(ovl, p_mid) + _dot(ovl, p_lo)
        yield

        o_win = []
        for u in range(n_sub - 1):
            s_win.append(window_scores(u + 1))
            o_win.append(window_softmax(u))
        yield

        v = imp[0:n_slc]
        jj = lax.broadcasted_iota(jnp.int32, (n_slc, tq), 0)
        cur = (q0 + lax.broadcasted_iota(jnp.int32, (n_slc, tq), 1)) // SEL_BLOCK
        forced = (jj == 0) | (jj == cur) | (jj == cur - 1)
        v = jnp.where(forced, FORCE, jnp.where(jj <= cur, v, -FORCE))
        sub = 8
        ranks = []
        row8 = lax.broadcasted_iota(jnp.int32, (sub, tq), 0)
        for r0 in range(0, n_slc, sub):
            vr = v[r0:r0 + sub]
            cnt = jnp.zeros((sub, tq), F32)
            for j2 in range(n_slc):
                vj = v[j2:j2 + 1, :]
                if j2 < r0:
                    ahead = vj >= vr
                elif j2 >= r0 + sub:
                    ahead = vj > vr
                else:
                    ahead = (vj > vr) | ((vj == vr) & (row8 > j2 - r0))
                cnt = cnt + jnp.where(ahead, 1.0, 0.0)
            ranks.append(cnt)
        rank = jnp.concatenate(ranks, axis=0)
        sel_neg = jnp.where((rank < min(SEL_TOPK, n_slc)) & (jj <= cur), 0.0, NEG)
        q_sel = q_operand(jnp.concatenate([sel_neg, jnp.zeros((HEAD_DIM - n_slc, tq), F32)], axis=0))
        yield

        n_grp = s_ref.shape[1]
        gw = nh * tq // n_grp

        def scores(c, g):
            keys = ksel_ref[pl.ds(pl.multiple_of(c * tk, tk), tk), :]
            s = _dot(keys, q_sel[:, g * gw:(g + 1) * gw])
            s_ref[t_u, g] = s
            return jnp.max(s, axis=0, keepdims=True)

        def absorb(c, g, s, m_tile, m, l):
            m_new = jnp.maximum(m, m_tile)
            alpha = jnp.exp2(m - m_new)
            p = jnp.exp2(s - m_new)
            l = alpha * l + jnp.sum(p, axis=0, keepdims=True)
            acc_ref[t_u, g] = alpha * acc_ref[t_u, g] + _dot(vtsel_ref[c], p.astype(BF16))
            return m_new, l

        n_chunks = (q0 + tq + tk - 1) // tk
        acc_ref[t_u] = jnp.zeros(acc_ref.shape[1:], F32)
        first_max = [scores(0, g) for g in range(n_grp)]
        o_win.append(window_softmax(n_sub - 1))
        yield

        def sel_step(c, carry):
            out = []
            for g in range(n_grp):
                m, l, m_tile = carry[g]
                s = s_ref[t_u, g]
                m_tile_next = scores(c + 1, g)
                out.append(absorb(c, g, s, m_tile, m, l) + (m_tile_next,))
            return tuple(out)

        stats = tuple((jnp.full((1, gw), NEG, F32), jnp.zeros((1, gw), F32), first_max[g]) for g in range(n_grp))
        stats = lax.fori_loop(0, n_chunks - 1, sel_step, stats)
        yield
        last = n_chunks - 1
        tri = tri_ref[(q0 - last * tk) // tq]
        tri_g = jnp.concatenate([tri] * (gw // tq), axis=1)
        l_last = []
        for g in range(n_grp):
            s_last = s_ref[t_u, g] + tri_g
            l_last.append(absorb(last, g, s_last, jnp.max(s_last, axis=0, keepdims=True), *stats[g][:2])[1])
        yield

        gate_t = _sigmoid(gate_ref[rows_u, :]).T
        z_gate = _silu(z_ref[rows_u, :])
        partial = []
        for h in range(nh):
            cols = slice(h * tq, (h + 1) * tq)
            o_win_h = jnp.concatenate([ow[:, h * tw:(h + 1) * tw] for ow in o_win], axis=1)
            partial.append(gate_t[0 * nh + h:0 * nh + h + 1, :] * o_cmp[:, cols]
                           + gate_t[2 * nh + h:2 * nh + h + 1, :] * o_win_h)
        o_slc = jnp.concatenate([acc_ref[t_u, g] / l_last[g] for g in range(n_grp)], axis=1)
        combs = [partial[h] + gate_t[1 * nh + h:1 * nh + h + 1, :] * o_slc[:, h * tq:(h + 1) * tq] for h in range(nh)]
        for c in range(nh // 2):
            blk = jnp.concatenate([combs[2 * c], combs[2 * c + 1]], axis=0).T
            o_ref[rows_u, c * LANES:(c + 1) * LANES] = (blk * z_gate[:, c * LANES:(c + 1) * LANES]).astype(o_ref.dtype)

    programs = [tile_program(t_u) for t_u in range(NSA_TILES)]
    while programs:
        for prog in list(programs):
            try:
                next(prog)
            except StopIteration:
                programs.remove(prog)


def _nsa(pb, pf, kcvc, ovl, tri, band, nb, seq):
    tq = TQ_NSA * NSA_TILES
    tk = TK_SEL
    nq = seq // tq
    return pl.pallas_call(
        _nsa_kernel,
        grid=(nb, nq),
        in_specs=[
            pl.BlockSpec((tq, 2 * LANES), lambda b, i: (b * nq + i, PB_AQ // 2)),
            pl.BlockSpec((seq, LANES), lambda b, i: (b, PB_SLC)),
            pl.BlockSpec((seq, LANES), lambda b, i: (b, PB_WIN)),
            pl.BlockSpec((None, LANES, LANES), lambda b, i: (b, 0, 0)),
            pl.BlockSpec((tq, LANES), lambda b, i: (b * nq + i, PF_AG)),
            pl.BlockSpec((tq, 2 * LANES), lambda b, i: (b * nq + i, PF_AZ // 2)),
            pl.BlockSpec((LANES, LANES), lambda b, i: (0, 0)),
            pl.BlockSpec(tri.shape, lambda b, i: (0, 0, 0)),
            pl.BlockSpec(band.shape, lambda b, i: (0, 0, 0)),
        ],
        out_specs=pl.BlockSpec((tq, GROUP_W), lambda b, i: (b * nq + i, 0)),
        out_shape=jax.ShapeDtypeStruct((nb * seq, GROUP_W), BF16),
        scratch_shapes=[
            pltpu.VMEM((seq, LANES), BF16),
            pltpu.VMEM((seq // tk, HEAD_DIM, tk), BF16),
            pltpu.VMEM((WIN + seq, LANES), BF16),
            pltpu.VMEM(((WIN + seq) // TW_WIN, HEAD_DIM, TW_WIN), BF16),
            pltpu.VMEM((HEAD_DIM, LANES), BF16),
            pltpu.VMEM((NSA_TILES, SEL_GROUPS, HEAD_DIM, N_HEADS * TQ_NSA // SEL_GROUPS), F32),
            pltpu.VMEM((NSA_TILES, SEL_GROUPS, tk, N_HEADS * TQ_NSA // SEL_GROUPS), F32),
        ],
        compiler_params=pltpu.CompilerParams(
            dimension_semantics=("arbitrary", "arbitrary"), vmem_limit_bytes=VMEM_LIMIT),
        name="nsa",
    )(pb, pb, pb, kcvc, pf, pf, ovl, tri, band)


def _dilated_kernel(q_ref, qnext_ref, k_ref, v_ref, z_ref, bias_ref, o_ref, vt_ref, acc_ref, s_ref, mt_ref):
    t = q_ref.shape[0]
    seq = k_ref.shape[0]
    n_pairs = N_HEADS // 2
    rows_h = DEN_ROWS + HEAD_DIM
    qi = pl.program_id(1)

    @pl.when(qi == 0)
    def _():
        ones_rows = jnp.where(lax.broadcasted_iota(jnp.int32, (DEN_ROWS, t), 0) == 0, 1.0, 0.0).astype(BF16)
        for j in range(seq // t):
            v_t = v_ref[j * t:(j + 1) * t, :].astype(F32).T.astype(BF16)
            for h in range(N_HEADS):
                vt_ref[j, h * rows_h:h * rows_h + DEN_ROWS] = ones_rows
                vt_ref[j, h * rows_h + DEN_ROWS:(h + 1) * rows_h] = v_t[h * HEAD_DIM:(h + 1) * HEAD_DIM]

    upper = lax.broadcasted_iota(jnp.int32, (LANES, t), 0) >= HEAD_DIM

    def query_operands(ref):
        q_t = ref[...].astype(F32).T
        ops = []
        for c in range(n_pairs):
            blk = q_t[c * LANES:(c + 1) * LANES]
            ops.append(jnp.concatenate([jnp.where(upper, 0.0, blk), jnp.where(upper, blk, 0.0)], axis=1).astype(BF16))
        return ops

    qs = query_operands(q_ref)
    qs_next = query_operands(qnext_ref)
    acc_ref[...] = jnp.zeros_like(acc_ref)

    def scores(j, c, q_op, q_tile):
        k0 = pl.multiple_of(j * t, t)
        bias = bias_ref[jnp.minimum(q_tile - j, bias_ref.shape[0] - 1)]
        s = _dot(k_ref[pl.ds(k0, t), c * LANES:(c + 1) * LANES], q_op) + jnp.concatenate([bias, bias], axis=1)
        s_ref[c] = s
        return jnp.max(s, axis=0, keepdims=True)

    @pl.when(qi == 0)
    def _():
        for c in range(n_pairs):
            mt_ref[c] = scores(0, c, qs[c], qi)

    def step(j, carry):
        is_last = j == qi
        j_next = jnp.where(is_last, 0, j + 1)
        q_tile = jnp.where(is_last, qi + 1, qi)
        out = []
        for c in range(n_pairs):
            m, m_tile = carry[c]
            s = s_ref[c]
            m_tile_next = scores(j_next, c, jnp.where(is_last, qs_next[c], qs[c]), q_tile)
            m_new = jnp.maximum(m, m_tile)
            alpha = jnp.exp2(m - m_new)
            pb = jnp.exp2(s - m_new).astype(BF16)
            for e in range(2):
                h = 2 * c + e
                pv = _dot(vt_ref[j, h * rows_h:(h + 1) * rows_h, :], pb[:, e * t:(e + 1) * t])
                acc_ref[h] = alpha[:, e * t:(e + 1) * t] * acc_ref[h] + pv
            out.append((m_new, m_tile_next))
        return tuple(out)

    init = tuple((jnp.full((1, 2 * t), NEG, F32), mt_ref[c]) for c in range(n_pairs))
    final = lax.fori_loop(0, qi + 1, step, init)
    for c in range(n_pairs):
        mt_ref[c] = final[c][1]
    for c in range(n_pairs):
        heads = [acc_ref[2 * c + e] for e in range(2)]
        o_pair = jnp.concatenate([a[DEN_ROWS:] / a[0:1] for a in heads], axis=0).T
        zc = z_ref[:, c * LANES:(c + 1) * LANES]
        o_ref[:, c * LANES:(c + 1) * LANES] = (o_pair * _silu(zc)).astype(o_ref.dtype)


def _dilated(pb, pf, bias_tab, nb, seq):
    t = T_DIL
    nq = seq // t
    return pl.pallas_call(
        _dilated_kernel,
        grid=(nb, nq),
        in_specs=[
            pl.BlockSpec((t, GROUP_W), lambda b, i: (b * nq + i, PB_BQ // 2)),
            pl.BlockSpec((t, GROUP_W), lambda b, i: (b * nq + jnp.minimum(i + 1, nq - 1), PB_BQ // 2)),
            pl.BlockSpec((seq, GROUP_W), lambda b, i: (b, PB_BK // 2)),
            pl.BlockSpec((seq, GROUP_W), lambda b, i: (b, PB_BV // 2)),
            pl.BlockSpec((t, GROUP_W), lambda b, i: (b * nq + i, PF_BZ // 2)),
            pl.BlockSpec((nq, t, t), lambda b, i: (0, 0, 0)),
        ],
        out_specs=pl.BlockSpec((t, GROUP_W), lambda b, i: (b * nq + i, 0)),
        out_shape=jax.ShapeDtypeStruct((nb * seq, GROUP_W), BF16),
        scratch_shapes=[
            pltpu.VMEM((nq, N_HEADS * (DEN_ROWS + HEAD_DIM), t), BF16),
            pltpu.VMEM((N_HEADS, DEN_ROWS + HEAD_DIM, t), F32),
            pltpu.VMEM((N_HEADS // 2, t, 2 * t), F32),
            pltpu.VMEM((N_HEADS // 2, 1, 2 * t), F32),
        ],
        compiler_params=pltpu.CompilerParams(
            dimension_semantics=("arbitrary", "arbitrary"), vmem_limit_bytes=VMEM_LIMIT),
        name="dilated",
    )(pb, pb, pb, pb, pf, bias_tab)


def _pool_sgu_kernel(c_ref, cprev_ref, cz_ref, du_ref, dv_ref, dz_ref,
                     wpool_ref, pscale_ref, lng_ref, lnb_ref, wsp_ref, bsp_ref, yc_ref, yd_ref):
    ts = c_ref.shape[0]
    i = pl.program_id(1)
    lane = lax.broadcasted_iota(jnp.int32, (ts, GROUP_W), 1)

    cur = c_ref[...]
    prev = jnp.where(i > 0, cprev_ref[...], 0.0)
    ext = jnp.concatenate([prev, cur], axis=0)
    t_head = i * ts + lax.broadcasted_iota(jnp.int32, (POOL_HALO, 1), 0)
    pooled = None
    acc = ext
    width = 1
    for g, w in enumerate(POOL_SIZES):
        while width < w:
            acc = acc + pltpu.roll(acc, width, 0)
            width *= 2
        inv_head = 1.0 / jnp.minimum(t_head + 1, w).astype(F32)
        mean_w = jnp.concatenate([acc[POOL_HALO:2 * POOL_HALO] * inv_head,
                                  acc[2 * POOL_HALO:POOL_HALO + ts] * (1.0 / w)], axis=0)
        pooled = mean_w if pooled is None else jnp.where(lane >= g * HEAD_DIM, mean_w, pooled)
    pooled = pooled - cur
    mixed = _dot(pooled.astype(BF16), wpool_ref[...]) * pscale_ref[...]
    yc_ref[...] = (mixed * _silu(cz_ref[...])).astype(yc_ref.dtype)

    v = dv_ref[...]
    mu = jnp.mean(v, axis=-1, keepdims=True)
    var = jnp.mean(jnp.square(v - mu), axis=-1, keepdims=True)
    vn = ((v - mu) * lax.rsqrt(var + LN_EPS) * lng_ref[...] + lnb_ref[...]).astype(BF16)
    r = lax.broadcasted_iota(jnp.int32, (SG_CHUNK, SG_CHUNK), 0)
    cidx = lax.broadcasted_iota(jnp.int32, (SG_CHUNK, SG_CHUNK), 1)
    w_tril = [jnp.where(cidx <= r, wsp_ref[g], 0.0).astype(BF16) for g in range(N_HEADS)]
    lane_c = lax.broadcasted_iota(jnp.int32, (SG_CHUNK, GROUP_W), 1)
    for ci in range(ts // SG_CHUNK):
        rows = slice(ci * SG_CHUNK, (ci + 1) * SG_CHUNK)
        vc = vn[rows]
        zmix = _dot(w_tril[0], vc)
        for g in range(1, N_HEADS):
            zmix = jnp.where(lane_c >= g * HEAD_DIM, _dot(w_tril[g], vc), zmix)
        zfull = zmix + bsp_ref[...]
        yd_ref[rows, :] = (du_ref[rows, :] * zfull * _silu(dz_ref[rows, :])).astype(yd_ref.dtype)


def _pool_sgu(pf, wpool_bd, pool_scale, ln_g, ln_b, w_sp, bsp_exp, layer, nb, seq):
    ts = TS_POOL
    ns = seq // ts
    halo_blocks = ts // POOL_HALO

    def tile(col):
        return pl.BlockSpec((ts, GROUP_W), lambda b, i: (b * ns + i, col // 2))

    def per_layer(shape):
        nd = len(shape)
        return pl.BlockSpec((None,) + shape, lambda b, i: (layer,) + (0,) * nd)

    return pl.pallas_call(
        _pool_sgu_kernel,
        grid=(nb, ns),
        in_specs=[
            tile(PF_CIN),
            pl.BlockSpec((POOL_HALO, GROUP_W),
                         lambda b, i: (jnp.maximum((b * ns + i) * halo_blocks - 1, 0), PF_CIN // 2)),
            tile(PF_CZ), tile(PF_DU), tile(PF_DV), tile(PF_DZ),
            per_layer((GROUP_W, GROUP_W)),
            per_layer((1, GROUP_W)),
            per_layer((1, GROUP_W)),
            per_layer((1, GROUP_W)),
            per_layer((N_HEADS, SG_CHUNK, SG_CHUNK)),
            per_layer((SG_CHUNK, GROUP_W)),
        ],
        out_specs=[
            pl.BlockSpec((ts, GROUP_W), lambda b, i: (b * ns + i, 0)),
            pl.BlockSpec((ts, GROUP_W), lambda b, i: (b * ns + i, 0)),
        ],
        out_shape=[
            jax.ShapeDtypeStruct((nb * seq, GROUP_W), BF16),
            jax.ShapeDtypeStruct((nb * seq, GROUP_W), BF16),
        ],
        compiler_params=pltpu.CompilerParams(
            dimension_semantics=("arbitrary", "arbitrary"), vmem_limit_bytes=VMEM_LIMIT),
        name="pool_sgu",
    )(pf, pf, pf, pf, pf, pf, wpool_bd, pool_scale, ln_g, ln_b, w_sp, bsp_exp)


def _project_out(ya_ref, yb_ref, yc_ref, yd_ref, w_ref, x_ref, g_ref):
    y = jnp.concatenate([ya_ref[...], yb_ref[...], yc_ref[...], yd_ref[...]], axis=1)
    out = _dot(y, w_ref[...])
    inv = lax.rsqrt(jnp.mean(out * out, axis=-1, keepdims=True) + RMS_EPS)
    return x_ref[...] + out * inv * g_ref[...]


def _out_proj_kernel(ya_ref, yb_ref, yc_ref, yd_ref, w_ref, x_ref, g_ref, o_ref):
    o_ref[...] = _project_out(ya_ref, yb_ref, yc_ref, yd_ref, w_ref, x_ref, g_ref)


def _out_in_proj_kernel(ya_ref, yb_ref, yc_ref, yd_ref, wo_ref, x_ref, gpost_ref,
                        gpre_ref, wa_ref, wg_ref, wb_ref, o_ref, pb_ref, pf_ref):
    half = o_ref.shape[0] // 2
    x_new = []
    for r in range(2):
        rows = pl.ds(r * half, half)
        x_new.append(_project_out(ya_ref.at[rows], yb_ref.at[rows], yc_ref.at[rows], yd_ref.at[rows],
                                  wo_ref, x_ref.at[rows], gpost_ref))
        o_ref[rows, :] = x_new[r]
    for r in range(2):
        rows = pl.ds(r * half, half)
        _project_in(x_new[r], gpre_ref, wa_ref, wg_ref, wb_ref, pb_ref.at[rows], pf_ref.at[rows])


def _out_in_proj(ya, yb, yc, yd, w_out, x2, g_post, g_pre, w_a, w_g, w_b, layer):
    m, d = x2.shape
    tm = TM_OUT
    ytile = pl.BlockSpec((tm, GROUP_W), lambda i: (i, 0))
    nxt = layer + 1
    return pl.pallas_call(
        _out_in_proj_kernel,
        grid=(m // tm,),
        in_specs=[
            ytile, ytile, ytile, ytile,
            pl.BlockSpec((None, 4 * GROUP_W, d), lambda i: (layer, 0, 0)),
            pl.BlockSpec((tm, d), lambda i: (i, 0)),
            pl.BlockSpec((None, 1, d), lambda i: (layer, 0, 0)),
            pl.BlockSpec((None, 1, d), lambda i: (nxt, 0, 0)),
            pl.BlockSpec((None, d, w_a.shape[2]), lambda i: (nxt, 0, 0)),
            pl.BlockSpec((None, d, w_g.shape[2]), lambda i: (nxt, 0, 0)),
            pl.BlockSpec((None, d, w_b.shape[2]), lambda i: (nxt, 0, 0)),
        ],
        out_specs=[
            pl.BlockSpec((tm, d), lambda i: (i, 0)),
            pl.BlockSpec((tm, PB_COLS), lambda i: (i, 0)),
            pl.BlockSpec((tm, PF_COLS), lambda i: (i, 0)),
        ],
        out_shape=[
            jax.ShapeDtypeStruct((m, d), F32),
            jax.ShapeDtypeStruct((m, PB_COLS), BF16),
            jax.ShapeDtypeStruct((m, PF_COLS), F32),
        ],
        compiler_params=pltpu.CompilerParams(
            dimension_semantics=("arbitrary",), vmem_limit_bytes=VMEM_LIMIT_FUSED),
        name="out_in_proj",
    )(ya, yb, yc, yd, w_out, x2, g_post, g_pre, w_a, w_g, w_b)


def _out_proj(ya, yb, yc, yd, w_out, x2, g_post, layer):
    m, d = x2.shape
    tm = TM_LAST
    ytile = pl.BlockSpec((tm, GROUP_W), lambda i: (i, 0))
    return pl.pallas_call(
        _out_proj_kernel,
        grid=(m // tm,),
        in_specs=[
            ytile, ytile, ytile, ytile,
            pl.BlockSpec((None, 4 * GROUP_W, d), lambda i: (layer, 0, 0)),
            pl.BlockSpec((tm, d), lambda i: (i, 0)),
            pl.BlockSpec((None, 1, d), lambda i: (layer, 0, 0)),
        ],
        out_specs=pl.BlockSpec((tm, d), lambda i: (i, 0)),
        out_shape=jax.ShapeDtypeStruct((m, d), F32),
        compiler_params=pltpu.CompilerParams(
            dimension_semantics=("arbitrary",), vmem_limit_bytes=VMEM_LIMIT),
        name="out_proj",
    )(ya, yb, yc, yd, w_out, x2, g_post)


def _overlap_t(seq):
    n_cmp = (seq - CMP_LEN) // CMP_STRIDE + 1
    n_slc = seq // SEL_BLOCK
    cs = np.arange(n_cmp) * CMP_STRIDE
    ss = np.arange(n_slc) * SEL_BLOCK
    ov = (cs[None, :] < ss[:, None] + SEL_BLOCK) & (cs[None, :] + CMP_LEN > ss[:, None])
    out = np.zeros((LANES, LANES), np.float32)
    out[:n_slc, :n_cmp] = ov
    return out


def _causal_tri():
    a = np.arange(TK_SEL)[:, None]
    b = np.arange(TQ_NSA)[None, :]
    return np.stack([np.where(a - o * TQ_NSA <= b, 0.0, NEG) for o in range(TK_SEL // TQ_NSA)]).astype(np.float32)


def _window_band():
    a = np.arange(TW_WIN)[:, None]
    b = np.arange(TW_WIN)[None, :]
    return np.stack([np.where(a > b, 0.0, NEG), np.where(a <= b, 0.0, NEG)]).astype(np.float32)


def _dilated_log_multiplicity(seq):
    t = T_DIL
    d0 = np.arange(t)[:, None] - np.arange(t)[None, :]
    tabs = []
    for delta in range(seq // t):
        d = d0 + delta * t
        mult = np.zeros_like(d)
        for window, dil in DILATED_PAIRS:
            mult += (d >= 0) & (d % dil == 0) & (d // dil <= window // dil) & (d // dil <= seq // dil - 1)
        tabs.append(np.where(mult > 0, np.log2(np.maximum(mult, 1)), NEG).T)
    return np.stack(tabs).astype(np.float32)


def _split_w_in(w_in):
    gw = GROUP_W
    n_a = gw + 6 * HEAD_DIM
    n_b = 10 * gw
    assert w_in.shape[-1] == n_a + N_GATES + n_b, w_in.shape
    scale_a = np.ones((n_a,), np.float32)
    scale_a[:gw] = Q_SCALE
    scale_b = np.ones((n_b,), np.float32)
    scale_b[gw:2 * gw] = Q_SCALE
    w_a = (w_in[..., :n_a] * scale_a).astype(BF16)
    w_g = jnp.pad(w_in[..., n_a:n_a + N_GATES], ((0, 0), (0, 0), (0, LANES - N_GATES))).astype(BF16)
    w_b = (w_in[..., n_a + N_GATES:] * scale_b).astype(BF16)
    return w_a, w_g, w_b


def kernel(x, g_pre, w_in, pe_cmp, w_cmp1, w_cmp2, w_pool, pool_scale, sg_ln_g, sg_ln_b, w_sp, b_sp, w_out, g_post):
    nb, seq, d = x.shape
    depth = w_in.shape[0]
    assert seq % TK_SEL == 0 and seq % T_DIL == 0 and seq % TS_POOL == 0
    assert all((nb * seq) % tm == 0 for tm in (TM_PROJ, TM_OUT, TM_LAST))
    assert TQ_NSA % TW_WIN == 0 and TK_SEL % TQ_NSA == 0 and WIN % TW_WIN == 0 and TW_WIN % LANES == 0
    assert seq % (TQ_NSA * NSA_TILES) == 0
    assert seq // SEL_BLOCK <= LANES - HEAD_DIM and (seq - CMP_LEN) // CMP_STRIDE + 1 < LANES
    assert seq % CMP_STRIDE == 0 and CMP_LEN == 2 * CMP_STRIDE

    w_a, w_g, w_b = _split_w_in(w_in)
    w_out_b = w_out.astype(BF16)
    w1_l = w_cmp1.reshape(depth, 2, CMP_LEN, HEAD_DIM, CMP_HIDDEN)
    w2_b = w_cmp2.astype(BF16)
    pe_rows = jnp.concatenate([pe_cmp[:, 0], pe_cmp[:, 1]], axis=-1)
    eye = jnp.eye(N_HEADS, dtype=w_pool.dtype)
    wpool_bd = jnp.einsum('lgcd,gh->lgchd', w_pool, eye).reshape(depth, GROUP_W, GROUP_W).astype(BF16)
    bsp_exp = jnp.repeat(jnp.swapaxes(b_sp, 1, 2), HEAD_DIM, axis=2)
    g_pre3 = g_pre.reshape(depth, 1, d)
    g_post3 = g_post.reshape(depth, 1, d)
    pscale3 = pool_scale.reshape(depth, 1, GROUP_W)
    lng3 = sg_ln_g.reshape(depth, 1, GROUP_W)
    lnb3 = sg_ln_b.reshape(depth, 1, GROUP_W)
    ovl_t = jnp.asarray(_overlap_t(seq), BF16)
    tri = jnp.asarray(_causal_tri())
    band = jnp.asarray(_window_band())
    dil_bias = jnp.asarray(_dilated_log_multiplicity(seq))

    x2 = x.reshape(nb * seq, d)
    pb, pf = _in_proj(x2, g_pre3, w_a, w_g, w_b, 0)
    for layer in range(depth):
        kcvc = _compress(pf, pe_rows, w1_l, w2_b, layer, nb, seq)
        ya = _nsa(pb, pf, kcvc, ovl_t, tri, band, nb, seq)
        yb = _dilated(pb, pf, dil_bias, nb, seq)
        yc, yd = _pool_sgu(pf, wpool_bd, pscale3, lng3, lnb3, w_sp, bsp_exp, layer, nb, seq)
        if layer + 1 < depth:
            x2, pb, pf = _out_in_proj(ya, yb, yc, yd, w_out_b, x2, g_post3, g_pre3, w_a, w_g, w_b, layer)
        else:
            x2 = _out_proj(ya, yb, yc, yd, w_out_b, x2, g_post3, layer)
    return x2.reshape(nb, seq, d)
```

```python
import numpy as np
import jax
import jax.numpy as jnp
from jax import lax
from jax.experimental import pallas as pl
from jax.experimental.pallas import tpu as pltpu

F32 = jnp.float32
BF16 = jnp.bfloat16

HEAD_DIM = 64
N_HEADS = 4
GROUP_W = N_HEADS * HEAD_DIM
CMP_LEN = 32
CMP_STRIDE = 16
CMP_HIDDEN = 256
SEL_BLOCK = 64
SEL_TOPK = 16
WIN = 512
FORCE = 1e4
DILATED_PAIRS = ((128, 1), (512, 4), (2048, 16))
POOL_SIZES = (2, 4, 8, 16)
SG_CHUNK = 128
RMS_EPS = 1e-6
LN_EPS = 1e-5
NEG = -1e30

LANES = 128
V7X_VMEM_BYTES = 64 * 1024 * 1024
VMEM_LIMIT = V7X_VMEM_BYTES * 3 // 4
VMEM_LIMIT_FUSED = V7X_VMEM_BYTES * 7 // 8
N_GATES = 3 * N_HEADS

PB_AQ, PB_SLC, PB_WIN, PB_BQ, PB_BK, PB_BV = 0, 2, 3, 4, 6, 8
PB_COLS = 10 * LANES
PF_CMP, PF_AG, PF_AZ, PF_BZ, PF_CIN, PF_CZ, PF_DU, PF_DV, PF_DZ = 0, 1, 2, 4, 6, 8, 10, 12, 14
PF_COLS = 16 * LANES

TM_PROJ = 512
TM_OUT = 512
TM_LAST = 1024
TQ_NSA = 256
TK_SEL = 256
T_DIL = 256
TS_POOL = 1024
POOL_HALO = 16
TW_WIN = 128
SEL_GROUPS = 2
NSA_TILES = 8
DIL_TILES = 8
DEN_ROWS = 16

Q_SCALE = float(np.log2(np.e)) * HEAD_DIM ** -0.5


def _sigmoid(x):
    return 0.5 * jnp.tanh(0.5 * x) + 0.5


def _silu(x):
    return x * _sigmoid(x)


def _dot(a, b):
    return jnp.dot(a, b, preferred_element_type=F32)


def _in_proj_kernel(x_ref, g_ref, wa_ref, wg_ref, wb_ref, pb_ref, pf_ref):
    half = x_ref.shape[0] // 2
    for r in range(2):
        rows = pl.ds(r * half, half)
        _project_in(x_ref[rows, :], g_ref, wa_ref, wg_ref, wb_ref, pb_ref.at[rows], pf_ref.at[rows])


def _project_in(x, g_ref, wa_ref, wg_ref, wb_ref, pb_ref, pf_ref):
    inv = lax.rsqrt(jnp.mean(x * x, axis=-1, keepdims=True) + RMS_EPS)
    h = (x * inv * g_ref[...]).astype(BF16)
    cw = 2 * LANES

    def put(dst, c, w):
        dst[:, c * cw:(c + 1) * cw] = _dot(h, w).astype(dst.dtype)

    put(pb_ref, PB_AQ // 2, wa_ref[:, 0:cw])
    put(pb_ref, PB_SLC // 2, wa_ref[:, cw + LANES:2 * cw + LANES])
    for c in range(3):
        put(pb_ref, PB_BQ // 2 + c, wb_ref[:, (1 + c) * cw:(2 + c) * cw])
    put(pf_ref, PF_CMP // 2, jnp.concatenate([wa_ref[:, cw:cw + LANES], wg_ref[...]], axis=1))
    put(pf_ref, PF_AZ // 2, wb_ref[:, 0:cw])
    for c in range(6):
        put(pf_ref, PF_BZ // 2 + c, wb_ref[:, (4 + c) * cw:(5 + c) * cw])


def _in_proj(x2, g_pre, w_a, w_g, w_b, layer):
    m, d = x2.shape
    return pl.pallas_call(
        _in_proj_kernel,
        grid=(m // TM_PROJ,),
        in_specs=[
            pl.BlockSpec((TM_PROJ, d), lambda i: (i, 0)),
            pl.BlockSpec((None, 1, d), lambda i: (layer, 0, 0)),
            pl.BlockSpec((None, d, w_a.shape[2]), lambda i: (layer, 0, 0)),
            pl.BlockSpec((None, d, w_g.shape[2]), lambda i: (layer, 0, 0)),
            pl.BlockSpec((None, d, w_b.shape[2]), lambda i: (layer, 0, 0)),
        ],
        out_specs=[
            pl.BlockSpec((TM_PROJ, PB_COLS), lambda i: (i, 0)),
            pl.BlockSpec((TM_PROJ, PF_COLS), lambda i: (i, 0)),
        ],
        out_shape=[
            jax.ShapeDtypeStruct((m, PB_COLS), BF16),
            jax.ShapeDtypeStruct((m, PF_COLS), F32),
        ],
        compiler_params=pltpu.CompilerParams(
            dimension_semantics=("arbitrary",), vmem_limit_bytes=VMEM_LIMIT),
        name="in_proj",
    )(x2, g_pre, w_a, w_g, w_b)


def _gelu_tanh(x):
    return 0.5 * x * (1.0 + jnp.tanh(np.sqrt(2.0 / np.pi).astype(np.float32) * (x + 0.044715 * (x * x * x))))


def _compress_kernel(x_ref, pe_ref, w1_ref, w2_ref, o_ref):
    n_grp = x_ref.shape[0] // CMP_STRIDE
    ng = o_ref.shape[1]
    lo = [None, None]
    hi = [None, None]
    for l0 in range(CMP_STRIDE):
        rows = x_ref[pl.ds(l0, n_grp, stride=CMP_STRIDE), :]
        r_lo = (rows + pe_ref[l0:l0 + 1, :]).astype(BF16)
        r_hi = (rows + pe_ref[CMP_STRIDE + l0:CMP_STRIDE + l0 + 1, :]).astype(BF16)
        for j in range(2):
            cols = slice(j * HEAD_DIM, (j + 1) * HEAD_DIM)
            d_lo = _dot(r_lo[:, cols], w1_ref[j, l0].astype(BF16))
            d_hi = _dot(r_hi[:, cols], w1_ref[j, CMP_STRIDE + l0].astype(BF16))
            lo[j] = d_lo if lo[j] is None else lo[j] + d_lo
            hi[j] = d_hi if hi[j] is None else hi[j] + d_hi
    row = lax.broadcasted_iota(jnp.int32, (n_grp, HEAD_DIM), 0) % ng
    for j in range(2):
        hid = _gelu_tanh(lo[j] + pltpu.roll(hi[j], n_grp - 1, 0)).astype(BF16)
        comp = jnp.where(row < ng - 1, _dot(hid, w2_ref[j]), 0.0)
        o_ref[:, :, j * HEAD_DIM:(j + 1) * HEAD_DIM] = comp.reshape(o_ref.shape[0], ng, HEAD_DIM).astype(BF16)


def _compress(pf, pe_rows, w1, w2, layer, nb, seq):
    ng = seq // CMP_STRIDE
    return pl.pallas_call(
        _compress_kernel,
        grid=(1,),
        in_specs=[
            pl.BlockSpec((nb * seq, LANES), lambda i: (0, PF_CMP)),
            pl.BlockSpec((None, CMP_LEN, LANES), lambda i: (layer, 0, 0)),
            pl.BlockSpec((None, 2, CMP_LEN, HEAD_DIM, CMP_HIDDEN), lambda i: (layer, 0, 0, 0, 0)),
            pl.BlockSpec((None, 2, CMP_HIDDEN, HEAD_DIM), lambda i: (layer, 0, 0, 0)),
        ],
        out_specs=pl.BlockSpec((nb, ng, LANES), lambda i: (0, 0, 0)),
        out_shape=jax.ShapeDtypeStruct((nb, ng, LANES), BF16),
        compiler_params=pltpu.CompilerParams(
            dimension_semantics=("arbitrary",), vmem_limit_bytes=VMEM_LIMIT),
        name="compress",
    )(pf, pe_rows, w1, w2)


def _nsa_kernel(q_ref, slc_ref, win_ref, kc_ref, gate_ref, z_ref, ovl_ref, tri_ref, band_ref, o_ref,
                ksel_ref, vtsel_ref, kwin_ref, vtwin_ref, kct_ref, acc_ref, s_ref):
    tq = q_ref.shape[0] // NSA_TILES
    seq = slc_ref.shape[0]
    tk = s_ref.shape[2]
    tw = band_ref.shape[1]
    nh = N_HEADS
    n_slc = seq // SEL_BLOCK
    step = pl.program_id(1)

    @pl.when(step == 0)
    def _():
        lane_k = lax.broadcasted_iota(jnp.int32, (tk, LANES), 1)
        for j in range(seq // tk):
            rows = slice(j * tk, (j + 1) * tk)
            blk = slc_ref[rows, :]
            key_blk = (j * tk + lax.broadcasted_iota(jnp.int32, (tk, LANES), 0)) // SEL_BLOCK
            onehot = jnp.where(lane_k - HEAD_DIM == key_blk, 1.0, 0.0).astype(BF16)
            ksel_ref[rows, :] = jnp.where(lane_k < HEAD_DIM, blk, onehot)
            vtsel_ref[j] = blk.astype(F32).T[HEAD_DIM:].astype(BF16)
        lane_w = lax.broadcasted_iota(jnp.int32, (WIN, LANES), 1)
        kwin_ref[0:WIN, :] = jnp.where(lane_w == HEAD_DIM, 1.0, 0.0).astype(BF16)
        for j in range(WIN // tw):
            vtwin_ref[j] = jnp.zeros((HEAD_DIM, tw), BF16)
        lane_t = lax.broadcasted_iota(jnp.int32, (tw, LANES), 1)
        for j in range(seq // tw):
            blk = win_ref[j * tw:(j + 1) * tw, :]
            kwin_ref[WIN + j * tw:WIN + (j + 1) * tw, :] = jnp.where(lane_t < HEAD_DIM, blk, jnp.zeros_like(blk))
            vtwin_ref[WIN // tw + j] = blk.astype(F32).T[HEAD_DIM:].astype(BF16)
        kct_ref[...] = kc_ref[...].astype(F32).T[HEAD_DIM:].astype(BF16)

    def tile_program(t_u):
        qi = step * NSA_TILES + t_u
        q0 = qi * tq
        rows_u = slice(t_u * tq, (t_u + 1) * tq)
        q_t = q_ref[rows_u, :].astype(F32).T

        def q_operand(extra):
            return jnp.concatenate([jnp.concatenate([q_t[h * HEAD_DIM:(h + 1) * HEAD_DIM], extra], axis=0)
                                    for h in range(nh)], axis=1).astype(BF16)

        def tile4(x):
            return jnp.concatenate([x] * nh, axis=1)


        span = WIN + tw
        extra_row = lax.broadcasted_iota(jnp.int32, (HEAD_DIM, tq), 0)
        q_win = q_operand(jnp.where(extra_row == 0, NEG, 0.0))
        n_sub = tq // tw

        def window_scores(u):
            q_sub = jnp.concatenate([q_win[:, h * tq + u * tw:h * tq + (u + 1) * tw] for h in range(nh)], axis=1)
            start = pl.multiple_of(q0 + u * tw, tw)
            return _dot(kwin_ref[pl.ds(start, span), :], q_sub)

        kc = kc_ref[...]
        s_c = _dot(kc, q_operand(jnp.zeros((HEAD_DIM, tq), F32)))
        s_win = [window_scores(0)]
        yield

        def window_softmax(u):
            edge = [jnp.concatenate([band_ref[e]] * nh, axis=1) for e in range(2)]
            s_w = s_win[u]
            s_w = jnp.concatenate([s_w[0:tw] + edge[0], s_w[tw:span - tw], s_w[span - tw:] + edge[1]], axis=0)
            e_w = jnp.exp2(s_w - jnp.max(s_w, axis=0, keepdims=True))
            t0 = qi * n_sub + u
            vt_w = jnp.concatenate([vtwin_ref[t0 + d] for d in range(span // tw)], axis=1)
            return _dot(vt_w, e_w.astype(BF16)) / jnp.sum(e_w, axis=0, keepdims=True)

        n_cmp = (seq - CMP_LEN) // CMP_STRIDE + 1
        n_idx = lax.broadcasted_iota(jnp.int32, (LANES, tq), 0)
        t_idx = q0 + lax.broadcasted_iota(jnp.int32, (LANES, tq), 1)
        valid_c = tile4((n_idx * CMP_STRIDE + (CMP_LEN - 1) <= t_idx) & (n_idx < n_cmp))
        s_c = jnp.where(valid_c, s_c, NEG)
        e_c = jnp.exp2(s_c - jnp.max(s_c, axis=0, keepdims=True))
        p_c = jnp.where(valid_c, e_c / jnp.sum(e_c, axis=0, keepdims=True), 0.0)
        o_cmp = _dot(kct_ref[...], p_c.astype(BF16))

        psum = p_c[:, 0:tq]
        for h in range(1, nh):
            psum = psum + p_c[:, h * tq:(h + 1) * tq]
        p_hi = psum.astype(BF16)
        r1 = psum - p_hi.astype(F32)
        p_mid = r1.astype(BF16)
        p_lo = (r1 - p_mid.astype(F32)).astype(BF16)
        ovl = ovl_ref[...]
        imp = _dot(ovl, p_hi) + _dot(ovl, p_mid) + _dot(ovl, p_lo)
        yield

        o_win = []
        for u in range(n_sub - 1):
            s_win.append(window_scores(u + 1))
            o_win.append(window_softmax(u))
        yield

        v = imp[0:n_slc]
        jj = lax.broadcasted_iota(jnp.int32, (n_slc, tq), 0)
        cur = (q0 + lax.broadcasted_iota(jnp.int32, (n_slc, tq), 1)) // SEL_BLOCK
        forced = (jj == 0) | (jj == cur) | (jj == cur - 1)
        v = jnp.where(forced, FORCE, jnp.where(jj <= cur, v, -FORCE))
        sub = 8
        ranks = []
        row8 = lax.broadcasted_iota(jnp.int32, (sub, tq), 0)
        for r0 in range(0, n_slc, sub):
            vr = v[r0:r0 + sub]
            cnt = jnp.zeros((sub, tq), F32)
            for j2 in range(n_slc):
                vj = v[j2:j2 + 1, :]
                if j2 < r0:
                    ahead = vj >= vr
                elif j2 >= r0 + sub:
                    ahead = vj > vr
                else:
                    ahead = (vj > vr) | ((vj == vr) & (row8 > j2 - r0))
                cnt = cnt + jnp.where(ahead, 1.0, 0.0)
            ranks.append(cnt)
        rank = jnp.concatenate(ranks, axis=0)
        sel_neg = jnp.where((rank < min(SEL_TOPK, n_slc)) & (jj <= cur), 0.0, NEG)
        q_sel = q_operand(jnp.concatenate([sel_neg, jnp.zeros((HEAD_DIM - n_slc, tq), F32)], axis=0))
        yield

        n_grp = s_ref.shape[1]
        gw = nh * tq // n_grp

        def scores(c, g):
            keys = ksel_ref[pl.ds(pl.multiple_of(c * tk, tk), tk), :]
            s = _dot(keys, q_sel[:, g * gw:(g + 1) * gw])
            s_ref[t_u, g] = s
            return jnp.max(s, axis=0, keepdims=True)

        def absorb(c, g, s, m_tile, m, l):
            m_new = jnp.maximum(m, m_tile)
            alpha = jnp.exp2(m - m_new)
            p = jnp.exp2(s - m_new)
            l = alpha * l + jnp.sum(p, axis=0, keepdims=True)
            acc_ref[t_u, g] = alpha * acc_ref[t_u, g] + _dot(vtsel_ref[c], p.astype(BF16))
            return m_new, l

        n_chunks = (q0 + tq + tk - 1) // tk
        acc_ref[t_u] = jnp.zeros(acc_ref.shape[1:], F32)
        first_max = [scores(0, g) for g in range(n_grp)]
        o_win.append(window_softmax(n_sub - 1))
        yield

        def sel_step(c, carry):
            out = []
            for g in range(n_grp):
                m, l, m_tile = carry[g]
                s = s_ref[t_u, g]
                m_tile_next = scores(c + 1, g)
                out.append(absorb(c, g, s, m_tile, m, l) + (m_tile_next,))
            return tuple(out)

        stats = tuple((jnp.full((1, gw), NEG, F32), jnp.zeros((1, gw), F32), first_max[g]) for g in range(n_grp))
        stats = lax.fori_loop(0, n_chunks - 1, sel_step, stats)
        yield
        last = n_chunks - 1
        tri = tri_ref[(q0 - last * tk) // tq]
        tri_g = jnp.concatenate([tri] * (gw // tq), axis=1)
        l_last = []
        for g in range(n_grp):
            s_last = s_ref[t_u, g] + tri_g
            l_last.append(absorb(last, g, s_last, jnp.max(s_last, axis=0, keepdims=True), *stats[g][:2])[1])
        yield

        gate_t = _sigmoid(gate_ref[rows_u, :]).T
        z_gate = _silu(z_ref[rows_u, :])
        partial = []
        for h in range(nh):
            cols = slice(h * tq, (h + 1) * tq)
            o_win_h = jnp.concatenate([ow[:, h * tw:(h + 1) * tw] for ow in o_win], axis=1)
            partial.append(gate_t[0 * nh + h:0 * nh + h + 1, :] * o_cmp[:, cols]
                           + gate_t[2 * nh + h:2 * nh + h + 1, :] * o_win_h)
        o_slc = jnp.concatenate([acc_ref[t_u, g] / l_last[g] for g in range(n_grp)], axis=1)
        combs = [partial[h] + gate_t[1 * nh + h:1 * nh + h + 1, :] * o_slc[:, h * tq:(h + 1) * tq] for h in range(nh)]
        for c in range(nh // 2):
            blk = jnp.concatenate([combs[2 * c], combs[2 * c + 1]], axis=0).T
            o_ref[rows_u, c * LANES:(c + 1) * LANES] = (blk * z_gate[:, c * LANES:(c + 1) * LANES]).astype(o_ref.dtype)

    programs = [tile_program(t_u) for t_u in range(NSA_TILES)]
    while programs:
        for prog in list(programs):
            try:
                next(prog)
            except StopIteration:
                programs.remove(prog)


def _nsa(pb, pf, kcvc, ovl, tri, band, nb, seq):
    tq = TQ_NSA * NSA_TILES
    tk = TK_SEL
    nq = seq // tq
    return pl.pallas_call(
        _nsa_kernel,
        grid=(nb, nq),
        in_specs=[
            pl.BlockSpec((tq, 2 * LANES), lambda b, i: (b * nq + i, PB_AQ // 2)),
            pl.BlockSpec((seq, LANES), lambda b, i: (b, PB_SLC)),
            pl.BlockSpec((seq, LANES), lambda b, i: (b, PB_WIN)),
            pl.BlockSpec((None, LANES, LANES), lambda b, i: (b, 0, 0)),
            pl.BlockSpec((tq, LANES), lambda b, i: (b * nq + i, PF_AG)),
            pl.BlockSpec((tq, 2 * LANES), lambda b, i: (b * nq + i, PF_AZ // 2)),
            pl.BlockSpec((LANES, LANES), lambda b, i: (0, 0)),
            pl.BlockSpec(tri.shape, lambda b, i: (0, 0, 0)),
            pl.BlockSpec(band.shape, lambda b, i: (0, 0, 0)),
        ],
        out_specs=pl.BlockSpec((tq, GROUP_W), lambda b, i: (b * nq + i, 0)),
        out_shape=jax.ShapeDtypeStruct((nb * seq, GROUP_W), BF16),
        scratch_shapes=[
            pltpu.VMEM((seq, LANES), BF16),
            pltpu.VMEM((seq // tk, HEAD_DIM, tk), BF16),
            pltpu.VMEM((WIN + seq, LANES), BF16),
            pltpu.VMEM(((WIN + seq) // TW_WIN, HEAD_DIM, TW_WIN), BF16),
            pltpu.VMEM((HEAD_DIM, LANES), BF16),
            pltpu.VMEM((NSA_TILES, SEL_GROUPS, HEAD_DIM, N_HEADS * TQ_NSA // SEL_GROUPS), F32),
            pltpu.VMEM((NSA_TILES, SEL_GROUPS, tk, N_HEADS * TQ_NSA // SEL_GROUPS), F32),
        ],
        compiler_params=pltpu.CompilerParams(
            dimension_semantics=("arbitrary", "arbitrary"), vmem_limit_bytes=VMEM_LIMIT),
        name="nsa",
    )(pb, pb, pb, kcvc, pf, pf, ovl, tri, band)


def _dilated_kernel(q_ref, qnext_ref, k_ref, v_ref, z_ref, bias_ref, o_ref, vt_ref, acc_ref, s_ref, mt_ref):
    n_tiles = acc_ref.shape[0]
    t = q_ref.shape[0] // n_tiles
    seq = k_ref.shape[0]
    n_pairs = N_HEADS // 2
    rows_h = DEN_ROWS + HEAD_DIM
    step = pl.program_id(1)

    @pl.when(step == 0)
    def _():
        ones_rows = jnp.where(lax.broadcasted_iota(jnp.int32, (DEN_ROWS, t), 0) == 0, 1.0, 0.0).astype(BF16)
        for j in range(seq // t):
            v_t = v_ref[j * t:(j + 1) * t, :].astype(F32).T.astype(BF16)
            for h in range(N_HEADS):
                vt_ref[j, h * rows_h:h * rows_h + DEN_ROWS] = ones_rows
                vt_ref[j, h * rows_h + DEN_ROWS:(h + 1) * rows_h] = v_t[h * HEAD_DIM:(h + 1) * HEAD_DIM]

    upper = lax.broadcasted_iota(jnp.int32, (LANES, t), 0) >= HEAD_DIM

    def query_operands(q):
        q_t = q.astype(F32).T
        ops = []
        for c in range(n_pairs):
            blk = q_t[c * LANES:(c + 1) * LANES]
            ops.append(jnp.concatenate([jnp.where(upper, 0.0, blk), jnp.where(upper, blk, 0.0)], axis=1).astype(BF16))
        return ops

    operands = [query_operands(q_ref[u * t:(u + 1) * t, :]) for u in range(n_tiles)] + [query_operands(qnext_ref[...])]
    acc_ref[...] = jnp.zeros_like(acc_ref)

    def scores(j, c, q_op, q_tile):
        k0 = pl.multiple_of(j * t, t)
        bias = bias_ref[jnp.minimum(q_tile - j, bias_ref.shape[0] - 1)]
        s = _dot(k_ref[pl.ds(k0, t), c * LANES:(c + 1) * LANES], q_op) + jnp.concatenate([bias, bias], axis=1)
        s_ref[c] = s
        return jnp.max(s, axis=0, keepdims=True)

    @pl.when(step == 0)
    def _():
        for c in range(n_pairs):
            mt_ref[c] = scores(0, c, operands[0][c], 0)

    tile_max = [mt_ref[c] for c in range(n_pairs)]
    for u in range(n_tiles):
        qi = step * n_tiles + u
        qs, qs_next = operands[u], operands[u + 1]

        def tile_step(j, carry, u=u, qi=qi, qs=qs, qs_next=qs_next):
            is_last = j == qi
            j_next = jnp.where(is_last, 0, j + 1)
            q_tile = jnp.where(is_last, qi + 1, qi)
            out = []
            for c in range(n_pairs):
                m, m_tile = carry[c]
                s = s_ref[c]
                m_tile_next = scores(j_next, c, jnp.where(is_last, qs_next[c], qs[c]), q_tile)
                m_new = jnp.maximum(m, m_tile)
                alpha = jnp.exp2(m - m_new)
                pb = jnp.exp2(s - m_new).astype(BF16)
                for e in range(2):
                    h = 2 * c + e
                    pv = _dot(vt_ref[j, h * rows_h:(h + 1) * rows_h, :], pb[:, e * t:(e + 1) * t])
                    acc_ref[u, h] = alpha[:, e * t:(e + 1) * t] * acc_ref[u, h] + pv
                out.append((m_new, m_tile_next))
            return tuple(out)

        init = tuple((jnp.full((1, 2 * t), NEG, F32), tile_max[c]) for c in range(n_pairs))
        final = lax.fori_loop(0, qi + 1, tile_step, init)
        tile_max = [final[c][1] for c in range(n_pairs)]
    for c in range(n_pairs):
        mt_ref[c] = tile_max[c]

    for u in range(n_tiles):
        rows = slice(u * t, (u + 1) * t)
        for c in range(n_pairs):
            heads = [acc_ref[u, 2 * c + e] for e in range(2)]
            o_pair = jnp.concatenate([a[DEN_ROWS:] / a[0:1] for a in heads], axis=0).T
            zc = z_ref[rows, c * LANES:(c + 1) * LANES]
            o_ref[rows, c * LANES:(c + 1) * LANES] = (o_pair * _silu(zc)).astype(o_ref.dtype)


def _dilated(pb, pf, bias_tab, nb, seq):
    t = T_DIL
    n_key_tiles = seq // t
    ts = t * DIL_TILES
    nq = seq // ts
    return pl.pallas_call(
        _dilated_kernel,
        grid=(nb, nq),
        in_specs=[
            pl.BlockSpec((ts, GROUP_W), lambda b, i: (b * nq + i, PB_BQ // 2)),
            pl.BlockSpec((t, GROUP_W),
                         lambda b, i: (b * n_key_tiles + jnp.minimum((i + 1) * DIL_TILES, n_key_tiles - 1), PB_BQ // 2)),
            pl.BlockSpec((seq, GROUP_W), lambda b, i: (b, PB_BK // 2)),
            pl.BlockSpec((seq, GROUP_W), lambda b, i: (b, PB_BV // 2)),
            pl.BlockSpec((ts, GROUP_W), lambda b, i: (b * nq + i, PF_BZ // 2)),
            pl.BlockSpec((n_key_tiles, t, t), lambda b, i: (0, 0, 0)),
        ],
        out_specs=pl.BlockSpec((ts, GROUP_W), lambda b, i: (b * nq + i, 0)),
        out_shape=jax.ShapeDtypeStruct((nb * seq, GROUP_W), BF16),
        scratch_shapes=[
            pltpu.VMEM((n_key_tiles, N_HEADS * (DEN_ROWS + HEAD_DIM), t), BF16),
            pltpu.VMEM((DIL_TILES, N_HEADS, DEN_ROWS + HEAD_DIM, t), F32),
            pltpu.VMEM((N_HEADS // 2, t, 2 * t), F32),
            pltpu.VMEM((N_HEADS // 2, 1, 2 * t), F32),
        ],
        compiler_params=pltpu.CompilerParams(
            dimension_semantics=("arbitrary", "arbitrary"), vmem_limit_bytes=VMEM_LIMIT),
        name="dilated",
    )(pb, pb, pb, pb, pf, bias_tab)


def _pool_sgu_kernel(c_ref, cprev_ref, cz_ref, du_ref, dv_ref, dz_ref,
                     wpool_ref, pscale_ref, lng_ref, lnb_ref, wsp_ref, bsp_ref, yc_ref, yd_ref):
    ts = c_ref.shape[0]
    i = pl.program_id(1)
    lane = lax.broadcasted_iota(jnp.int32, (ts, GROUP_W), 1)

    cur = c_ref[...]
    prev = jnp.where(i > 0, cprev_ref[...], 0.0)
    ext = jnp.concatenate([prev, cur], axis=0)
    t_head = i * ts + lax.broadcasted_iota(jnp.int32, (POOL_HALO, 1), 0)
    pooled = None
    acc = ext
    width = 1
    for g, w in enumerate(POOL_SIZES):
        while width < w:
            acc = acc + pltpu.roll(acc, width, 0)
            width *= 2
        inv_head = 1.0 / jnp.minimum(t_head + 1, w).astype(F32)
        mean_w = jnp.concatenate([acc[POOL_HALO:2 * POOL_HALO] * inv_head,
                                  acc[2 * POOL_HALO:POOL_HALO + ts] * (1.0 / w)], axis=0)
        pooled = mean_w if pooled is None else jnp.where(lane >= g * HEAD_DIM, mean_w, pooled)
    pooled = pooled - cur
    mixed = _dot(pooled.astype(BF16), wpool_ref[...]) * pscale_ref[...]
    yc_ref[...] = (mixed * _silu(cz_ref[...])).astype(yc_ref.dtype)

    v = dv_ref[...]
    mu = jnp.mean(v, axis=-1, keepdims=True)
    var = jnp.mean(jnp.square(v - mu), axis=-1, keepdims=True)
    vn = ((v - mu) * lax.rsqrt(var + LN_EPS) * lng_ref[...] + lnb_ref[...]).astype(BF16)
    r = lax.broadcasted_iota(jnp.int32, (SG_CHUNK, SG_CHUNK), 0)
    cidx = lax.broadcasted_iota(jnp.int32, (SG_CHUNK, SG_CHUNK), 1)
    w_tril = [jnp.where(cidx <= r, wsp_ref[g], 0.0).astype(BF16) for g in range(N_HEADS)]
    lane_c = lax.broadcasted_iota(jnp.int32, (SG_CHUNK, GROUP_W), 1)
    for ci in range(ts // SG_CHUNK):
        rows = slice(ci * SG_CHUNK, (ci + 1) * SG_CHUNK)
        vc = vn[rows]
        zmix = _dot(w_tril[0], vc)
        for g in range(1, N_HEADS):
            zmix = jnp.where(lane_c >= g * HEAD_DIM, _dot(w_tril[g], vc), zmix)
        zfull = zmix + bsp_ref[...]
        yd_ref[rows, :] = (du_ref[rows, :] * zfull * _silu(dz_ref[rows, :])).astype(yd_ref.dtype)


def _pool_sgu(pf, wpool_bd, pool_scale, ln_g, ln_b, w_sp, bsp_exp, layer, nb, seq):
    ts = TS_POOL
    ns = seq // ts
    halo_blocks = ts // POOL_HALO

    def tile(col):
        return pl.BlockSpec((ts, GROUP_W), lambda b, i: (b * ns + i, col // 2))

    def per_layer(shape):
        nd = len(shape)
        return pl.BlockSpec((None,) + shape, lambda b, i: (layer,) + (0,) * nd)

    return pl.pallas_call(
        _pool_sgu_kernel,
        grid=(nb, ns),
        in_specs=[
            tile(PF_CIN),
            pl.BlockSpec((POOL_HALO, GROUP_W),
                         lambda b, i: (jnp.maximum((b * ns + i) * halo_blocks - 1, 0), PF_CIN // 2)),
            tile(PF_CZ), tile(PF_DU), tile(PF_DV), tile(PF_DZ),
            per_layer((GROUP_W, GROUP_W)),
            per_layer((1, GROUP_W)),
            per_layer((1, GROUP_W)),
            per_layer((1, GROUP_W)),
            per_layer((N_HEADS, SG_CHUNK, SG_CHUNK)),
            per_layer((SG_CHUNK, GROUP_W)),
        ],
        out_specs=[
            pl.BlockSpec((ts, GROUP_W), lambda b, i: (b * ns + i, 0)),
            pl.BlockSpec((ts, GROUP_W), lambda b, i: (b * ns + i, 0)),
        ],
        out_shape=[
            jax.ShapeDtypeStruct((nb * seq, GROUP_W), BF16),
            jax.ShapeDtypeStruct((nb * seq, GROUP_W), BF16),
        ],
        compiler_params=pltpu.CompilerParams(
            dimension_semantics=("arbitrary", "arbitrary"), vmem_limit_bytes=VMEM_LIMIT),
        name="pool_sgu",
    )(pf, pf, pf, pf, pf, pf, wpool_bd, pool_scale, ln_g, ln_b, w_sp, bsp_exp)


def _project_out(ya_ref, yb_ref, yc_ref, yd_ref, w_ref, x_ref, g_ref):
    y = jnp.concatenate([ya_ref[...], yb_ref[...], yc_ref[...], yd_ref[...]], axis=1)
    out = _dot(y, w_ref[...])
    inv = lax.rsqrt(jnp.mean(out * out, axis=-1, keepdims=True) + RMS_EPS)
    return x_ref[...] + out * inv * g_ref[...]


def _out_proj_kernel(ya_ref, yb_ref, yc_ref, yd_ref, w_ref, x_ref, g_ref, o_ref):
    o_ref[...] = _project_out(ya_ref, yb_ref, yc_ref, yd_ref, w_ref, x_ref, g_ref)


def _out_in_proj_kernel(ya_ref, yb_ref, yc_ref, yd_ref, wo_ref, x_ref, gpost_ref,
                        gpre_ref, wa_ref, wg_ref, wb_ref, o_ref, pb_ref, pf_ref):
    half = o_ref.shape[0] // 2
    x_new = []
    for r in range(2):
        rows = pl.ds(r * half, half)
        x_new.append(_project_out(ya_ref.at[rows], yb_ref.at[rows], yc_ref.at[rows], yd_ref.at[rows],
                                  wo_ref, x_ref.at[rows], gpost_ref))
        o_ref[rows, :] = x_new[r]
    for r in range(2):
        rows = pl.ds(r * half, half)
        _project_in(x_new[r], gpre_ref, wa_ref, wg_ref, wb_ref, pb_ref.at[rows], pf_ref.at[rows])


def _out_in_proj(ya, yb, yc, yd, w_out, x2, g_post, g_pre, w_a, w_g, w_b, layer):
    m, d = x2.shape
    tm = TM_OUT
    ytile = pl.BlockSpec((tm, GROUP_W), lambda i: (i, 0))
    nxt = layer + 1
    return pl.pallas_call(
        _out_in_proj_kernel,
        grid=(m // tm,),
        in_specs=[
            ytile, ytile, ytile, ytile,
            pl.BlockSpec((None, 4 * GROUP_W, d), lambda i: (layer, 0, 0)),
            pl.BlockSpec((tm, d), lambda i: (i, 0)),
            pl.BlockSpec((None, 1, d), lambda i: (layer, 0, 0)),
            pl.BlockSpec((None, 1, d), lambda i: (nxt, 0, 0)),
            pl.BlockSpec((None, d, w_a.shape[2]), lambda i: (nxt, 0, 0)),
            pl.BlockSpec((None, d, w_g.shape[2]), lambda i: (nxt, 0, 0)),
            pl.BlockSpec((None, d, w_b.shape[2]), lambda i: (nxt, 0, 0)),
        ],
        out_specs=[
            pl.BlockSpec((tm, d), lambda i: (i, 0)),
            pl.BlockSpec((tm, PB_COLS), lambda i: (i, 0)),
            pl.BlockSpec((tm, PF_COLS), lambda i: (i, 0)),
        ],
        out_shape=[
            jax.ShapeDtypeStruct((m, d), F32),
            jax.ShapeDtypeStruct((m, PB_COLS), BF16),
            jax.ShapeDtypeStruct((m, PF_COLS), F32),
        ],
        compiler_params=pltpu.CompilerParams(
            dimension_semantics=("arbitrary",), vmem_limit_bytes=VMEM_LIMIT_FUSED),
        name="out_in_proj",
    )(ya, yb, yc, yd, w_out, x2, g_post, g_pre, w_a, w_g, w_b)


def _out_proj(ya, yb, yc, yd, w_out, x2, g_post, layer):
    m, d = x2.shape
    tm = TM_LAST
    ytile = pl.BlockSpec((tm, GROUP_W), lambda i: (i, 0))
    return pl.pallas_call(
        _out_proj_kernel,
        grid=(m // tm,),
        in_specs=[
            ytile, ytile, ytile, ytile,
            pl.BlockSpec((None, 4 * GROUP_W, d), lambda i: (layer, 0, 0)),
            pl.BlockSpec((tm, d), lambda i: (i, 0)),
            pl.BlockSpec((None, 1, d), lambda i: (layer, 0, 0)),
        ],
        out_specs=pl.BlockSpec((tm, d), lambda i: (i, 0)),
        out_shape=jax.ShapeDtypeStruct((m, d), F32),
        compiler_params=pltpu.CompilerParams(
            dimension_semantics=("arbitrary",), vmem_limit_bytes=VMEM_LIMIT),
        name="out_proj",
    )(ya, yb, yc, yd, w_out, x2, g_post)


def _overlap_t(seq):
    n_cmp = (seq - CMP_LEN) // CMP_STRIDE + 1
    n_slc = seq // SEL_BLOCK
    cs = np.arange(n_cmp) * CMP_STRIDE
    ss = np.arange(n_slc) * SEL_BLOCK
    ov = (cs[None, :] < ss[:, None] + SEL_BLOCK) & (cs[None, :] + CMP_LEN > ss[:, None])
    out = np.zeros((LANES, LANES), np.float32)
    out[:n_slc, :n_cmp] = ov
    return out


def _causal_tri():
    a = np.arange(TK_SEL)[:, None]
    b = np.arange(TQ_NSA)[None, :]
    return np.stack([np.where(a - o * TQ_NSA <= b, 0.0, NEG) for o in range(TK_SEL // TQ_NSA)]).astype(np.float32)


def _window_band():
    a = np.arange(TW_WIN)[:, None]
    b = np.arange(TW_WIN)[None, :]
    return np.stack([np.where(a > b, 0.0, NEG), np.where(a <= b, 0.0, NEG)]).astype(np.float32)


def _dilated_log_multiplicity(seq):
    t = T_DIL
    d0 = np.arange(t)[:, None] - np.arange(t)[None, :]
    tabs = []
    for delta in range(seq // t):
        d = d0 + delta * t
        mult = np.zeros_like(d)
        for window, dil in DILATED_PAIRS:
            mult += (d >= 0) & (d % dil == 0) & (d // dil <= window // dil) & (d // dil <= seq // dil - 1)
        tabs.append(np.where(mult > 0, np.log2(np.maximum(mult, 1)), NEG).T)
    return np.stack(tabs).astype(np.float32)


def _split_w_in(w_in):
    gw = GROUP_W
    n_a = gw + 6 * HEAD_DIM
    n_b = 10 * gw
    assert w_in.shape[-1] == n_a + N_GATES + n_b, w_in.shape
    scale_a = np.ones((n_a,), np.float32)
    scale_a[:gw] = Q_SCALE
    scale_b = np.ones((n_b,), np.float32)
    scale_b[gw:2 * gw] = Q_SCALE
    w_a = (w_in[..., :n_a] * scale_a).astype(BF16)
    w_g = jnp.pad(w_in[..., n_a:n_a + N_GATES], ((0, 0), (0, 0), (0, LANES - N_GATES))).astype(BF16)
    w_b = (w_in[..., n_a + N_GATES:] * scale_b).astype(BF16)
    return w_a, w_g, w_b


def kernel(x, g_pre, w_in, pe_cmp, w_cmp1, w_cmp2, w_pool, pool_scale, sg_ln_g, sg_ln_b, w_sp, b_sp, w_out, g_post):
    nb, seq, d = x.shape
    depth = w_in.shape[0]
    assert seq % TK_SEL == 0 and seq % T_DIL == 0 and seq % TS_POOL == 0
    assert all((nb * seq) % tm == 0 for tm in (TM_PROJ, TM_OUT, TM_LAST))
    assert TQ_NSA % TW_WIN == 0 and TK_SEL % TQ_NSA == 0 and WIN % TW_WIN == 0 and TW_WIN % LANES == 0
    assert seq % (TQ_NSA * NSA_TILES) == 0 and seq % (T_DIL * DIL_TILES) == 0
    assert seq // SEL_BLOCK <= LANES - HEAD_DIM and (seq - CMP_LEN) // CMP_STRIDE + 1 < LANES
    assert seq % CMP_STRIDE == 0 and CMP_LEN == 2 * CMP_STRIDE

    w_a, w_g, w_b = _split_w_in(w_in)
    w_out_b = w_out.astype(BF16)
    w1_l = w_cmp1.reshape(depth, 2, CMP_LEN, HEAD_DIM, CMP_HIDDEN)
    w2_b = w_cmp2.astype(BF16)
    pe_rows = jnp.concatenate([pe_cmp[:, 0], pe_cmp[:, 1]], axis=-1)
    eye = jnp.eye(N_HEADS, dtype=w_pool.dtype)
    wpool_bd = jnp.einsum('lgcd,gh->lgchd', w_pool, eye).reshape(depth, GROUP_W, GROUP_W).astype(BF16)
    bsp_exp = jnp.repeat(jnp.swapaxes(b_sp, 1, 2), HEAD_DIM, axis=2)
    g_pre3 = g_pre.reshape(depth, 1, d)
    g_post3 = g_post.reshape(depth, 1, d)
    pscale3 = pool_scale.reshape(depth, 1, GROUP_W)
    lng3 = sg_ln_g.reshape(depth, 1, GROUP_W)
    lnb3 = sg_ln_b.reshape(depth, 1, GROUP_W)
    ovl_t = jnp.asarray(_overlap_t(seq), BF16)
    tri = jnp.asarray(_causal_tri())
    band = jnp.asarray(_window_band())
    dil_bias = jnp.asarray(_dilated_log_multiplicity(seq))

    x2 = x.reshape(nb * seq, d)
    pb, pf = _in_proj(x2, g_pre3, w_a, w_g, w_b, 0)
    for layer in range(depth):
        kcvc = _compress(pf, pe_rows, w1_l, w2_b, layer, nb, seq)
        ya = _nsa(pb, pf, kcvc, ovl_t, tri, band, nb, seq)
        yb = _dilated(pb, pf, dil_bias, nb, seq)
        yc, yd = _pool_sgu(pf, wpool_bd, pscale3, lng3, lnb3, w_sp, bsp_exp, layer, nb, seq)
        if layer + 1 < depth:
            x2, pb, pf = _out_in_proj(ya, yb, yc, yd, w_out_b, x2, g_post3, g_pre3, w_a, w_g, w_b, layer)
        else:
            x2 = _out_proj(ya, yb, yc, yd, w_out_b, x2, g_post3, layer)
    return x2.reshape(nb, seq, d)
```

```python
import numpy as np
import jax
import jax.numpy as jnp
from jax import lax
from jax.experimental import pallas as pl
from jax.experimental.pallas import tpu as pltpu

F32 = jnp.float32
BF16 = jnp.bfloat16

HEAD_DIM = 64
N_HEADS = 4
GROUP_W = N_HEADS * HEAD_DIM
CMP_LEN = 32
CMP_STRIDE = 16
CMP_HIDDEN = 256
SEL_BLOCK = 64
SEL_TOPK = 16
WIN = 512
FORCE = 1e4
DILATED_PAIRS = ((128, 1), (512, 4), (2048, 16))
POOL_SIZES = (2, 4, 8, 16)
SG_CHUNK = 128
RMS_EPS = 1e-6
LN_EPS = 1e-5
NEG = -1e30

LANES = 128
V7X_VMEM_BYTES = 64 * 1024 * 1024
VMEM_LIMIT = V7X_VMEM_BYTES * 3 // 4
VMEM_LIMIT_FUSED = V7X_VMEM_BYTES * 7 // 8
N_GATES = 3 * N_HEADS

PB_AQ, PB_SLC, PB_WIN, PB_BQ, PB_BK, PB_BV = 0, 2, 3, 4, 6, 8
PB_COLS = 10 * LANES
PF_CMP, PF_AG, PF_AZ, PF_BZ, PF_CIN, PF_CZ, PF_DU, PF_DV, PF_DZ = 0, 1, 2, 4, 6, 8, 10, 12, 14
PF_COLS = 16 * LANES

TM_PROJ = 512
TM_OUT = 512
TM_LAST = 1024
TQ_NSA = 256
TK_SEL = 256
T_DIL = 256
TS_POOL = 2048
POOL_HALO = 16
TW_WIN = 128
SEL_GROUPS = 2
NSA_TILES = 8
DIL_TILES = 8
DEN_ROWS = 16

Q_SCALE = float(np.log2(np.e)) * HEAD_DIM ** -0.5


def _sigmoid(x):
    return 0.5 * jnp.tanh(0.5 * x) + 0.5


def _silu(x):
    return x * _sigmoid(x)


def _dot(a, b):
    return jnp.dot(a, b, preferred_element_type=F32)


def _in_proj_kernel(x_ref, g_ref, wa_ref, wg_ref, wb_ref, pb_ref, pf_ref):
    half = x_ref.shape[0] // 2
    for r in range(2):
        rows = pl.ds(r * half, half)
        _project_in(x_ref[rows, :], g_ref, wa_ref, wg_ref, wb_ref, pb_ref.at[rows], pf_ref.at[rows])


def _project_in(x, g_ref, wa_ref, wg_ref, wb_ref, pb_ref, pf_ref):
    inv = lax.rsqrt(jnp.mean(x * x, axis=-1, keepdims=True) + RMS_EPS)
    h = (x * inv * g_ref[...]).astype(BF16)
    cw = 2 * LANES

    def put(dst, c, w):
        dst[:, c * cw:(c + 1) * cw] = _dot(h, w).astype(dst.dtype)

    put(pb_ref, PB_AQ // 2, wa_ref[:, 0:cw])
    put(pb_ref, PB_SLC // 2, wa_ref[:, cw + LANES:2 * cw + LANES])
    for c in range(3):
        put(pb_ref, PB_BQ // 2 + c, wb_ref[:, (1 + c) * cw:(2 + c) * cw])
    put(pf_ref, PF_CMP // 2, jnp.concatenate([wa_ref[:, cw:cw + LANES], wg_ref[...]], axis=1))
    put(pf_ref, PF_AZ // 2, wb_ref[:, 0:cw])
    for c in range(6):
        put(pf_ref, PF_BZ // 2 + c, wb_ref[:, (4 + c) * cw:(5 + c) * cw])


def _in_proj(x2, g_pre, w_a, w_g, w_b, layer):
    m, d = x2.shape
    return pl.pallas_call(
        _in_proj_kernel,
        grid=(m // TM_PROJ,),
        in_specs=[
            pl.BlockSpec((TM_PROJ, d), lambda i: (i, 0)),
            pl.BlockSpec((None, 1, d), lambda i: (layer, 0, 0)),
            pl.BlockSpec((None, d, w_a.shape[2]), lambda i: (layer, 0, 0)),
            pl.BlockSpec((None, d, w_g.shape[2]), lambda i: (layer, 0, 0)),
            pl.BlockSpec((None, d, w_b.shape[2]), lambda i: (layer, 0, 0)),
        ],
        out_specs=[
            pl.BlockSpec((TM_PROJ, PB_COLS), lambda i: (i, 0)),
            pl.BlockSpec((TM_PROJ, PF_COLS), lambda i: (i, 0)),
        ],
        out_shape=[
            jax.ShapeDtypeStruct((m, PB_COLS), BF16),
            jax.ShapeDtypeStruct((m, PF_COLS), F32),
        ],
        compiler_params=pltpu.CompilerParams(
            dimension_semantics=("arbitrary",), vmem_limit_bytes=VMEM_LIMIT),
        name="in_proj",
    )(x2, g_pre, w_a, w_g, w_b)


def _gelu_tanh(x):
    return 0.5 * x * (1.0 + jnp.tanh(np.sqrt(2.0 / np.pi).astype(np.float32) * (x + 0.044715 * (x * x * x))))


def _compress_kernel(x_ref, pe_ref, w1_ref, w2_ref, o_ref):
    n_grp = x_ref.shape[0] // CMP_STRIDE
    ng = o_ref.shape[1]
    lo = [None, None]
    hi = [None, None]
    for l0 in range(CMP_STRIDE):
        rows = x_ref[pl.ds(l0, n_grp, stride=CMP_STRIDE), :]
        r_lo = (rows + pe_ref[l0:l0 + 1, :]).astype(BF16)
        r_hi = (rows + pe_ref[CMP_STRIDE + l0:CMP_STRIDE + l0 + 1, :]).astype(BF16)
        for j in range(2):
            cols = slice(j * HEAD_DIM, (j + 1) * HEAD_DIM)
            d_lo = _dot(r_lo[:, cols], w1_ref[j, l0].astype(BF16))
            d_hi = _dot(r_hi[:, cols], w1_ref[j, CMP_STRIDE + l0].astype(BF16))
            lo[j] = d_lo if lo[j] is None else lo[j] + d_lo
            hi[j] = d_hi if hi[j] is None else hi[j] + d_hi
    row = lax.broadcasted_iota(jnp.int32, (n_grp, HEAD_DIM), 0) % ng
    for j in range(2):
        hid = _gelu_tanh(lo[j] + pltpu.roll(hi[j], n_grp - 1, 0)).astype(BF16)
        comp = jnp.where(row < ng - 1, _dot(hid, w2_ref[j]), 0.0)
        o_ref[:, :, j * HEAD_DIM:(j + 1) * HEAD_DIM] = comp.reshape(o_ref.shape[0], ng, HEAD_DIM).astype(BF16)


def _compress(pf, pe_rows, w1, w2, layer, nb, seq):
    ng = seq // CMP_STRIDE
    return pl.pallas_call(
        _compress_kernel,
        grid=(1,),
        in_specs=[
            pl.BlockSpec((nb * seq, LANES), lambda i: (0, PF_CMP)),
            pl.BlockSpec((None, CMP_LEN, LANES), lambda i: (layer, 0, 0)),
            pl.BlockSpec((None, 2, CMP_LEN, HEAD_DIM, CMP_HIDDEN), lambda i: (layer, 0, 0, 0, 0)),
            pl.BlockSpec((None, 2, CMP_HIDDEN, HEAD_DIM), lambda i: (layer, 0, 0, 0)),
        ],
        out_specs=pl.BlockSpec((nb, ng, LANES), lambda i: (0, 0, 0)),
        out_shape=jax.ShapeDtypeStruct((nb, ng, LANES), BF16),
        compiler_params=pltpu.CompilerParams(
            dimension_semantics=("arbitrary",), vmem_limit_bytes=VMEM_LIMIT),
        name="compress",
    )(pf, pe_rows, w1, w2)


def _nsa_kernel(q_ref, slc_ref, win_ref, kc_ref, gate_ref, z_ref, ovl_ref, tri_ref, band_ref, o_ref,
                ksel_ref, vtsel_ref, kwin_ref, vtwin_ref, kct_ref, acc_ref, s_ref):
    tq = q_ref.shape[0] // NSA_TILES
    seq = slc_ref.shape[0]
    tk = s_ref.shape[2]
    tw = band_ref.shape[1]
    nh = N_HEADS
    n_slc = seq // SEL_BLOCK
    step = pl.program_id(1)

    @pl.when(step == 0)
    def _():
        lane_k = lax.broadcasted_iota(jnp.int32, (tk, LANES), 1)
        for j in range(seq // tk):
            rows = slice(j * tk, (j + 1) * tk)
            blk = slc_ref[rows, :]
            key_blk = (j * tk + lax.broadcasted_iota(jnp.int32, (tk, LANES), 0)) // SEL_BLOCK
            onehot = jnp.where(lane_k - HEAD_DIM == key_blk, 1.0, 0.0).astype(BF16)
            ksel_ref[rows, :] = jnp.where(lane_k < HEAD_DIM, blk, onehot)
            vtsel_ref[j] = blk.astype(F32).T[HEAD_DIM:].astype(BF16)
        lane_w = lax.broadcasted_iota(jnp.int32, (WIN, LANES), 1)
        kwin_ref[0:WIN, :] = jnp.where(lane_w == HEAD_DIM, 1.0, 0.0).astype(BF16)
        for j in range(WIN // tw):
            vtwin_ref[j] = jnp.zeros((HEAD_DIM, tw), BF16)
        lane_t = lax.broadcasted_iota(jnp.int32, (tw, LANES), 1)
        for j in range(seq // tw):
            blk = win_ref[j * tw:(j + 1) * tw, :]
            kwin_ref[WIN + j * tw:WIN + (j + 1) * tw, :] = jnp.where(lane_t < HEAD_DIM, blk, jnp.zeros_like(blk))
            vtwin_ref[WIN // tw + j] = blk.astype(F32).T[HEAD_DIM:].astype(BF16)
        kct_ref[...] = kc_ref[...].astype(F32).T[HEAD_DIM:].astype(BF16)

    def tile_program(t_u):
        qi = step * NSA_TILES + t_u
        q0 = qi * tq
        rows_u = slice(t_u * tq, (t_u + 1) * tq)
        q_t = q_ref[rows_u, :].astype(F32).T

        def q_operand(extra):
            return jnp.concatenate([jnp.concatenate([q_t[h * HEAD_DIM:(h + 1) * HEAD_DIM], extra], axis=0)
                                    for h in range(nh)], axis=1).astype(BF16)

        def tile4(x):
            return jnp.concatenate([x] * nh, axis=1)


        span = WIN + tw
        extra_row = lax.broadcasted_iota(jnp.int32, (HEAD_DIM, tq), 0)
        q_win = q_operand(jnp.where(extra_row == 0, NEG, 0.0))
        n_sub = tq // tw

        def window_scores(u):
            q_sub = jnp.concatenate([q_win[:, h * tq + u * tw:h * tq + (u + 1) * tw] for h in range(nh)], axis=1)
            start = pl.multiple_of(q0 + u * tw, tw)
            return _dot(kwin_ref[pl.ds(start, span), :], q_sub)

        kc = kc_ref[...]
        s_c = _dot(kc, q_operand(jnp.zeros((HEAD_DIM, tq), F32)))
        s_win = [window_scores(0)]
        yield

        def window_softmax(u):
            edge = [jnp.concatenate([band_ref[e]] * nh, axis=1) for e in range(2)]
            s_w = s_win[u]
            s_w = jnp.concatenate([s_w[0:tw] + edge[0], s_w[tw:span - tw], s_w[span - tw:] + edge[1]], axis=0)
            e_w = jnp.exp2(s_w - jnp.max(s_w, axis=0, keepdims=True))
            t0 = qi * n_sub + u
            vt_w = jnp.concatenate([vtwin_ref[t0 + d] for d in range(span // tw)], axis=1)
            return _dot(vt_w, e_w.astype(BF16)) / jnp.sum(e_w, axis=0, keepdims=True)

        n_cmp = (seq - CMP_LEN) // CMP_STRIDE + 1
        n_idx = lax.broadcasted_iota(jnp.int32, (LANES, tq), 0)
        t_idx = q0 + lax.broadcasted_iota(jnp.int32, (LANES, tq), 1)
        valid_c = tile4((n_idx * CMP_STRIDE + (CMP_LEN - 1) <= t_idx) & (n_idx < n_cmp))
        s_c = jnp.where(valid_c, s_c, NEG)
        e_c = jnp.exp2(s_c - jnp.max(s_c, axis=0, keepdims=True))
        p_c = jnp.where(valid_c, e_c / jnp.sum(e_c, axis=0, keepdims=True), 0.0)
        o_cmp = _dot(kct_ref[...], p_c.astype(BF16))

        psum = p_c[:, 0:tq]
        for h in range(1, nh):
            psum = psum + p_c[:, h * tq:(h + 1) * tq]
        p_hi = psum.astype(BF16)
        r1 = psum - p_hi.astype(F32)
        p_mid = r1.astype(BF16)
        p_lo = (r1 - p_mid.astype(F32)).astype(BF16)
        ovl = ovl_ref[...]
        imp = _dot(ovl, p_hi) + _dot(ovl, p_mid) + _dot(ovl, p_lo)
        yield

        o_win = []
        for u in range(n_sub - 1):
            s_win.append(window_scores(u + 1))
            o_win.append(window_softmax(u))
        yield

        v = imp[0:n_slc]
        jj = lax.broadcasted_iota(jnp.int32, (n_slc, tq), 0)
        cur = (q0 + lax.broadcasted_iota(jnp.int32, (n_slc, tq), 1)) // SEL_BLOCK
        forced = (jj == 0) | (jj == cur) | (jj == cur - 1)
        v = jnp.where(forced, FORCE, jnp.where(jj <= cur, v, -FORCE))
        sub = 8
        ranks = []
        row8 = lax.broadcasted_iota(jnp.int32, (sub, tq), 0)
        for r0 in range(0, n_slc, sub):
            vr = v[r0:r0 + sub]
            cnt = jnp.zeros((sub, tq), F32)
            for j2 in range(n_slc):
                vj = v[j2:j2 + 1, :]
                if j2 < r0:
                    ahead = vj >= vr
                elif j2 >= r0 + sub:
                    ahead = vj > vr
                else:
                    ahead = (vj > vr) | ((vj == vr) & (row8 > j2 - r0))
                cnt = cnt + jnp.where(ahead, 1.0, 0.0)
            ranks.append(cnt)
        rank = jnp.concatenate(ranks, axis=0)
        sel_neg = jnp.where((rank < min(SEL_TOPK, n_slc)) & (jj <= cur), 0.0, NEG)
        q_sel = q_operand(jnp.concatenate([sel_neg, jnp.zeros((HEAD_DIM - n_slc, tq), F32)], axis=0))
        yield

        n_grp = s_ref.shape[1]
        gw = nh * tq // n_grp

        def scores(c, g):
            keys = ksel_ref[pl.ds(pl.multiple_of(c * tk, tk), tk), :]
            s = _dot(keys, q_sel[:, g * gw:(g + 1) * gw])
            s_ref[t_u, g] = s
            return jnp.max(s, axis=0, keepdims=True)

        def absorb(c, g, s, m_tile, m, l):
            m_new = jnp.maximum(m, m_tile)
            alpha = jnp.exp2(m - m_new)
            p = jnp.exp2(s - m_new)
            l = alpha * l + jnp.sum(p, axis=0, keepdims=True)
            acc_ref[t_u, g] = alpha * acc_ref[t_u, g] + _dot(vtsel_ref[c], p.astype(BF16))
            return m_new, l

        n_chunks = (q0 + tq + tk - 1) // tk
        acc_ref[t_u] = jnp.zeros(acc_ref.shape[1:], F32)
        first_max = [scores(0, g) for g in range(n_grp)]
        o_win.append(window_softmax(n_sub - 1))
        yield

        def sel_step(c, carry):
            out = []
            for g in range(n_grp):
                m, l, m_tile = carry[g]
                s = s_ref[t_u, g]
                m_tile_next = scores(c + 1, g)
                out.append(absorb(c, g, s, m_tile, m, l) + (m_tile_next,))
            return tuple(out)

        stats = tuple((jnp.full((1, gw), NEG, F32), jnp.zeros((1, gw), F32), first_max[g]) for g in range(n_grp))
        stats = lax.fori_loop(0, n_chunks - 1, sel_step, stats)
        yield
        last = n_chunks - 1
        tri = tri_ref[(q0 - last * tk) // tq]
        tri_g = jnp.concatenate([tri] * (gw // tq), axis=1)
        l_last = []
        for g in range(n_grp):
            s_last = s_ref[t_u, g] + tri_g
            l_last.append(absorb(last, g, s_last, jnp.max(s_last, axis=0, keepdims=True), *stats[g][:2])[1])
        yield

        gate_t = _sigmoid(gate_ref[rows_u, :]).T
        z_gate = _silu(z_ref[rows_u, :])
        partial = []
        for h in range(nh):
            cols = slice(h * tq, (h + 1) * tq)
            o_win_h = jnp.concatenate([ow[:, h * tw:(h + 1) * tw] for ow in o_win], axis=1)
            partial.append(gate_t[0 * nh + h:0 * nh + h + 1, :] * o_cmp[:, cols]
                           + gate_t[2 * nh + h:2 * nh + h + 1, :] * o_win_h)
        o_slc = jnp.concatenate([acc_ref[t_u, g] / l_last[g] for g in range(n_grp)], axis=1)
        combs = [partial[h] + gate_t[1 * nh + h:1 * nh + h + 1, :] * o_slc[:, h * tq:(h + 1) * tq] for h in range(nh)]
        for c in range(nh // 2):
            blk = jnp.concatenate([combs[2 * c], combs[2 * c + 1]], axis=0).T
            o_ref[rows_u, c * LANES:(c + 1) * LANES] = (blk * z_gate[:, c * LANES:(c + 1) * LANES]).astype(o_ref.dtype)

    programs = [tile_program(t_u) for t_u in range(NSA_TILES)]
    while programs:
        for prog in list(programs):
            try:
                next(prog)
            except StopIteration:
                programs.remove(prog)


def _nsa(pb, pf, kcvc, ovl, tri, band, nb, seq):
    tq = TQ_NSA * NSA_TILES
    tk = TK_SEL
    nq = seq // tq
    return pl.pallas_call(
        _nsa_kernel,
        grid=(nb, nq),
        in_specs=[
            pl.BlockSpec((tq, 2 * LANES), lambda b, i: (b * nq + i, PB_AQ // 2)),
            pl.BlockSpec((seq, LANES), lambda b, i: (b, PB_SLC)),
            pl.BlockSpec((seq, LANES), lambda b, i: (b, PB_WIN)),
            pl.BlockSpec((None, LANES, LANES), lambda b, i: (b, 0, 0)),
            pl.BlockSpec((tq, LANES), lambda b, i: (b * nq + i, PF_AG)),
            pl.BlockSpec((tq, 2 * LANES), lambda b, i: (b * nq + i, PF_AZ // 2)),
            pl.BlockSpec((LANES, LANES), lambda b, i: (0, 0)),
            pl.BlockSpec(tri.shape, lambda b, i: (0, 0, 0)),
            pl.BlockSpec(band.shape, lambda b, i: (0, 0, 0)),
        ],
        out_specs=pl.BlockSpec((tq, GROUP_W), lambda b, i: (b * nq + i, 0)),
        out_shape=jax.ShapeDtypeStruct((nb * seq, GROUP_W), BF16),
        scratch_shapes=[
            pltpu.VMEM((seq, LANES), BF16),
            pltpu.VMEM((seq // tk, HEAD_DIM, tk), BF16),
            pltpu.VMEM((WIN + seq, LANES), BF16),
            pltpu.VMEM(((WIN + seq) // TW_WIN, HEAD_DIM, TW_WIN), BF16),
            pltpu.VMEM((HEAD_DIM, LANES), BF16),
            pltpu.VMEM((NSA_TILES, SEL_GROUPS, HEAD_DIM, N_HEADS * TQ_NSA // SEL_GROUPS), F32),
            pltpu.VMEM((NSA_TILES, SEL_GROUPS, tk, N_HEADS * TQ_NSA // SEL_GROUPS), F32),
        ],
        compiler_params=pltpu.CompilerParams(
            dimension_semantics=("arbitrary", "arbitrary"), vmem_limit_bytes=VMEM_LIMIT),
        name="nsa",
    )(pb, pb, pb, kcvc, pf, pf, ovl, tri, band)


def _dilated_kernel(q_ref, qnext_ref, k_ref, v_ref, z_ref, bias_ref, o_ref, vt_ref, acc_ref, s_ref, mt_ref):
    n_tiles = acc_ref.shape[0]
    t = q_ref.shape[0] // n_tiles
    seq = k_ref.shape[0]
    n_pairs = N_HEADS // 2
    rows_h = DEN_ROWS + HEAD_DIM
    step = pl.program_id(1)

    @pl.when(step == 0)
    def _():
        ones_rows = jnp.where(lax.broadcasted_iota(jnp.int32, (DEN_ROWS, t), 0) == 0, 1.0, 0.0).astype(BF16)
        for j in range(seq // t):
            v_t = v_ref[j * t:(j + 1) * t, :].astype(F32).T.astype(BF16)
            for h in range(N_HEADS):
                vt_ref[j, h * rows_h:h * rows_h + DEN_ROWS] = ones_rows
                vt_ref[j, h * rows_h + DEN_ROWS:(h + 1) * rows_h] = v_t[h * HEAD_DIM:(h + 1) * HEAD_DIM]

    upper = lax.broadcasted_iota(jnp.int32, (LANES, t), 0) >= HEAD_DIM

    def query_operands(q):
        q_t = q.astype(F32).T
        ops = []
        for c in range(n_pairs):
            blk = q_t[c * LANES:(c + 1) * LANES]
            ops.append(jnp.concatenate([jnp.where(upper, 0.0, blk), jnp.where(upper, blk, 0.0)], axis=1).astype(BF16))
        return ops

    operands = [query_operands(q_ref[u * t:(u + 1) * t, :]) for u in range(n_tiles)] + [query_operands(qnext_ref[...])]
    acc_ref[...] = jnp.zeros_like(acc_ref)

    def scores(j, c, q_op, q_tile):
        k0 = pl.multiple_of(j * t, t)
        bias = bias_ref[jnp.minimum(q_tile - j, bias_ref.shape[0] - 1)]
        s = _dot(k_ref[pl.ds(k0, t), c * LANES:(c + 1) * LANES], q_op) + jnp.concatenate([bias, bias], axis=1)
        s_ref[c] = s
        return jnp.max(s, axis=0, keepdims=True)

    @pl.when(step == 0)
    def _():
        for c in range(n_pairs):
            mt_ref[c] = scores(0, c, operands[0][c], 0)

    tile_max = [mt_ref[c] for c in range(n_pairs)]
    for u in range(n_tiles):
        qi = step * n_tiles + u
        qs, qs_next = operands[u], operands[u + 1]

        def tile_step(j, carry, u=u, qi=qi, qs=qs, qs_next=qs_next):
            is_last = j == qi
            j_next = jnp.where(is_last, 0, j + 1)
            q_tile = jnp.where(is_last, qi + 1, qi)
            out = []
            for c in range(n_pairs):
                m, m_tile = carry[c]
                s = s_ref[c]
                m_tile_next = scores(j_next, c, jnp.where(is_last, qs_next[c], qs[c]), q_tile)
                m_new = jnp.maximum(m, m_tile)
                alpha = jnp.exp2(m - m_new)
                pb = jnp.exp2(s - m_new).astype(BF16)
                for e in range(2):
                    h = 2 * c + e
                    pv = _dot(vt_ref[j, h * rows_h:(h + 1) * rows_h, :], pb[:, e * t:(e + 1) * t])
                    acc_ref[u, h] = alpha[:, e * t:(e + 1) * t] * acc_ref[u, h] + pv
                out.append((m_new, m_tile_next))
            return tuple(out)

        init = tuple((jnp.full((1, 2 * t), NEG, F32), tile_max[c]) for c in range(n_pairs))
        final = lax.fori_loop(0, qi + 1, tile_step, init)
        tile_max = [final[c][1] for c in range(n_pairs)]
    for c in range(n_pairs):
        mt_ref[c] = tile_max[c]

    for u in range(n_tiles):
        rows = slice(u * t, (u + 1) * t)
        for c in range(n_pairs):
            heads = [acc_ref[u, 2 * c + e] for e in range(2)]
            o_pair = jnp.concatenate([a[DEN_ROWS:] / a[0:1] for a in heads], axis=0).T
            zc = z_ref[rows, c * LANES:(c + 1) * LANES]
            o_ref[rows, c * LANES:(c + 1) * LANES] = (o_pair * _silu(zc)).astype(o_ref.dtype)


def _dilated(pb, pf, bias_tab, nb, seq):
    t = T_DIL
    n_key_tiles = seq // t
    ts = t * DIL_TILES
    nq = seq // ts
    return pl.pallas_call(
        _dilated_kernel,
        grid=(nb, nq),
        in_specs=[
            pl.BlockSpec((ts, GROUP_W), lambda b, i: (b * nq + i, PB_BQ // 2)),
            pl.BlockSpec((t, GROUP_W),
                         lambda b, i: (b * n_key_tiles + jnp.minimum((i + 1) * DIL_TILES, n_key_tiles - 1), PB_BQ // 2)),
            pl.BlockSpec((seq, GROUP_W), lambda b, i: (b, PB_BK // 2)),
            pl.BlockSpec((seq, GROUP_W), lambda b, i: (b, PB_BV // 2)),
            pl.BlockSpec((ts, GROUP_W), lambda b, i: (b * nq + i, PF_BZ // 2)),
            pl.BlockSpec((n_key_tiles, t, t), lambda b, i: (0, 0, 0)),
        ],
        out_specs=pl.BlockSpec((ts, GROUP_W), lambda b, i: (b * nq + i, 0)),
        out_shape=jax.ShapeDtypeStruct((nb * seq, GROUP_W), BF16),
        scratch_shapes=[
            pltpu.VMEM((n_key_tiles, N_HEADS * (DEN_ROWS + HEAD_DIM), t), BF16),
            pltpu.VMEM((DIL_TILES, N_HEADS, DEN_ROWS + HEAD_DIM, t), F32),
            pltpu.VMEM((N_HEADS // 2, t, 2 * t), F32),
            pltpu.VMEM((N_HEADS // 2, 1, 2 * t), F32),
        ],
        compiler_params=pltpu.CompilerParams(
            dimension_semantics=("arbitrary", "arbitrary"), vmem_limit_bytes=VMEM_LIMIT),
        name="dilated",
    )(pb, pb, pb, pb, pf, bias_tab)


def _pool_sgu_kernel(c_ref, cprev_ref, cz_ref, du_ref, dv_ref, dz_ref,
                     wpool_ref, pscale_ref, lng_ref, lnb_ref, wsp_ref, bsp_ref, yc_ref, yd_ref):
    ts = c_ref.shape[0]
    i = pl.program_id(1)
    lane = lax.broadcasted_iota(jnp.int32, (ts, GROUP_W), 1)

    cur = c_ref[...]
    prev = jnp.where(i > 0, cprev_ref[...], 0.0)
    ext = jnp.concatenate([prev, cur], axis=0)
    t_head = i * ts + lax.broadcasted_iota(jnp.int32, (POOL_HALO, 1), 0)
    pooled = None
    acc = ext
    width = 1
    for g, w in enumerate(POOL_SIZES):
        while width < w:
            acc = acc + pltpu.roll(acc, width, 0)
            width *= 2
        inv_head = 1.0 / jnp.minimum(t_head + 1, w).astype(F32)
        mean_w = jnp.concatenate([acc[POOL_HALO:2 * POOL_HALO] * inv_head,
                                  acc[2 * POOL_HALO:POOL_HALO + ts] * (1.0 / w)], axis=0)
        pooled = mean_w if pooled is None else jnp.where(lane >= g * HEAD_DIM, mean_w, pooled)
    pooled = pooled - cur
    mixed = _dot(pooled.astype(BF16), wpool_ref[...]) * pscale_ref[...]
    yc_ref[...] = (mixed * _silu(cz_ref[...])).astype(yc_ref.dtype)

    v = dv_ref[...]
    mu = jnp.mean(v, axis=-1, keepdims=True)
    var = jnp.mean(jnp.square(v - mu), axis=-1, keepdims=True)
    vn = ((v - mu) * lax.rsqrt(var + LN_EPS) * lng_ref[...] + lnb_ref[...]).astype(BF16)
    r = lax.broadcasted_iota(jnp.int32, (SG_CHUNK, SG_CHUNK), 0)
    cidx = lax.broadcasted_iota(jnp.int32, (SG_CHUNK, SG_CHUNK), 1)
    w_tril = [jnp.where(cidx <= r, wsp_ref[g], 0.0).astype(BF16) for g in range(N_HEADS)]
    lane_c = lax.broadcasted_iota(jnp.int32, (SG_CHUNK, GROUP_W), 1)
    for ci in range(ts // SG_CHUNK):
        rows = slice(ci * SG_CHUNK, (ci + 1) * SG_CHUNK)
        vc = vn[rows]
        zmix = _dot(w_tril[0], vc)
        for g in range(1, N_HEADS):
            zmix = jnp.where(lane_c >= g * HEAD_DIM, _dot(w_tril[g], vc), zmix)
        zfull = zmix + bsp_ref[...]
        yd_ref[rows, :] = (du_ref[rows, :] * zfull * _silu(dz_ref[rows, :])).astype(yd_ref.dtype)


def _pool_sgu(pf, wpool_bd, pool_scale, ln_g, ln_b, w_sp, bsp_exp, layer, nb, seq):
    ts = TS_POOL
    ns = seq // ts
    halo_blocks = ts // POOL_HALO

    def tile(col):
        return pl.BlockSpec((ts, GROUP_W), lambda b, i: (b * ns + i, col // 2))

    def per_layer(shape):
        nd = len(shape)
        return pl.BlockSpec((None,) + shape, lambda b, i: (layer,) + (0,) * nd)

    return pl.pallas_call(
        _pool_sgu_kernel,
        grid=(nb, ns),
        in_specs=[
            tile(PF_CIN),
            pl.BlockSpec((POOL_HALO, GROUP_W),
                         lambda b, i: (jnp.maximum((b * ns + i) * halo_blocks - 1, 0), PF_CIN // 2)),
            tile(PF_CZ), tile(PF_DU), tile(PF_DV), tile(PF_DZ),
            per_layer((GROUP_W, GROUP_W)),
            per_layer((1, GROUP_W)),
            per_layer((1, GROUP_W)),
            per_layer((1, GROUP_W)),
            per_layer((N_HEADS, SG_CHUNK, SG_CHUNK)),
            per_layer((SG_CHUNK, GROUP_W)),
        ],
        out_specs=[
            pl.BlockSpec((ts, GROUP_W), lambda b, i: (b * ns + i, 0)),
            pl.BlockSpec((ts, GROUP_W), lambda b, i: (b * ns + i, 0)),
        ],
        out_shape=[
            jax.ShapeDtypeStruct((nb * seq, GROUP_W), BF16),
            jax.ShapeDtypeStruct((nb * seq, GROUP_W), BF16),
        ],
        compiler_params=pltpu.CompilerParams(
            dimension_semantics=("arbitrary", "arbitrary"), vmem_limit_bytes=VMEM_LIMIT),
        name="pool_sgu",
    )(pf, pf, pf, pf, pf, pf, wpool_bd, pool_scale, ln_g, ln_b, w_sp, bsp_exp)


def _project_out(ya_ref, yb_ref, yc_ref, yd_ref, w_ref, x_ref, g_ref):
    y = jnp.concatenate([ya_ref[...], yb_ref[...], yc_ref[...], yd_ref[...]], axis=1)
    out = _dot(y, w_ref[...])
    inv = lax.rsqrt(jnp.mean(out * out, axis=-1, keepdims=True) + RMS_EPS)
    return x_ref[...] + out * inv * g_ref[...]


def _out_proj_kernel(ya_ref, yb_ref, yc_ref, yd_ref, w_ref, x_ref, g_ref, o_ref):
    o_ref[...] = _project_out(ya_ref, yb_ref, yc_ref, yd_ref, w_ref, x_ref, g_ref)


def _out_in_proj_kernel(ya_ref, yb_ref, yc_ref, yd_ref, wo_ref, x_ref, gpost_ref,
                        gpre_ref, wa_ref, wg_ref, wb_ref, o_ref, pb_ref, pf_ref):
    half = o_ref.shape[0] // 2
    x_new = []
    for r in range(2):
        rows = pl.ds(r * half, half)
        x_new.append(_project_out(ya_ref.at[rows], yb_ref.at[rows], yc_ref.at[rows], yd_ref.at[rows],
                                  wo_ref, x_ref.at[rows], gpost_ref))
        o_ref[rows, :] = x_new[r]
    for r in range(2):
        rows = pl.ds(r * half, half)
        _project_in(x_new[r], gpre_ref, wa_ref, wg_ref, wb_ref, pb_ref.at[rows], pf_ref.at[rows])


def _out_in_proj(ya, yb, yc, yd, w_out, x2, g_post, g_pre, w_a, w_g, w_b, layer):
    m, d = x2.shape
    tm = TM_OUT
    ytile = pl.BlockSpec((tm, GROUP_W), lambda i: (i, 0))
    nxt = layer + 1
    return pl.pallas_call(
        _out_in_proj_kernel,
        grid=(m // tm,),
        in_specs=[
            ytile, ytile, ytile, ytile,
            pl.BlockSpec((None, 4 * GROUP_W, d), lambda i: (layer, 0, 0)),
            pl.BlockSpec((tm, d), lambda i: (i, 0)),
            pl.BlockSpec((None, 1, d), lambda i: (layer, 0, 0)),
            pl.BlockSpec((None, 1, d), lambda i: (nxt, 0, 0)),
            pl.BlockSpec((None, d, w_a.shape[2]), lambda i: (nxt, 0, 0)),
            pl.BlockSpec((None, d, w_g.shape[2]), lambda i: (nxt, 0, 0)),
            pl.BlockSpec((None, d, w_b.shape[2]), lambda i: (nxt, 0, 0)),
        ],
        out_specs=[
            pl.BlockSpec((tm, d), lambda i: (i, 0)),
            pl.BlockSpec((tm, PB_COLS), lambda i: (i, 0)),
            pl.BlockSpec((tm, PF_COLS), lambda i: (i, 0)),
        ],
        out_shape=[
            jax.ShapeDtypeStruct((m, d), F32),
            jax.ShapeDtypeStruct((m, PB_COLS), BF16),
            jax.ShapeDtypeStruct((m, PF_COLS), F32),
        ],
        compiler_params=pltpu.CompilerParams(
            dimension_semantics=("arbitrary",), vmem_limit_bytes=VMEM_LIMIT_FUSED),
        name="out_in_proj",
    )(ya, yb, yc, yd, w_out, x2, g_post, g_pre, w_a, w_g, w_b)


def _out_proj(ya, yb, yc, yd, w_out, x2, g_post, layer):
    m, d = x2.shape
    tm = TM_LAST
    ytile = pl.BlockSpec((tm, GROUP_W), lambda i: (i, 0))
    return pl.pallas_call(
        _out_proj_kernel,
        grid=(m // tm,),
        in_specs=[
            ytile, ytile, ytile, ytile,
            pl.BlockSpec((None, 4 * GROUP_W, d), lambda i: (layer, 0, 0)),
            pl.BlockSpec((tm, d), lambda i: (i, 0)),
            pl.BlockSpec((None, 1, d), lambda i: (layer, 0, 0)),
        ],
        out_specs=pl.BlockSpec((tm, d), lambda i: (i, 0)),
        out_shape=jax.ShapeDtypeStruct((m, d), F32),
        compiler_params=pltpu.CompilerParams(
            dimension_semantics=("arbitrary",), vmem_limit_bytes=VMEM_LIMIT),
        name="out_proj",
    )(ya, yb, yc, yd, w_out, x2, g_post)


def _overlap_t(seq):
    n_cmp = (seq - CMP_LEN) // CMP_STRIDE + 1
    n_slc = seq // SEL_BLOCK
    cs = np.arange(n_cmp) * CMP_STRIDE
    ss = np.arange(n_slc) * SEL_BLOCK
    ov = (cs[None, :] < ss[:, None] + SEL_BLOCK) & (cs[None, :] + CMP_LEN > ss[:, None])
    out = np.zeros((LANES, LANES), np.float32)
    out[:n_slc, :n_cmp] = ov
    return out


def _causal_tri():
    a = np.arange(TK_SEL)[:, None]
    b = np.arange(TQ_NSA)[None, :]
    return np.stack([np.where(a - o * TQ_NSA <= b, 0.0, NEG) for o in range(TK_SEL // TQ_NSA)]).astype(np.float32)


def _window_band():
    a = np.arange(TW_WIN)[:, None]
    b = np.arange(TW_WIN)[None, :]
    return np.stack([np.where(a > b, 0.0, NEG), np.where(a <= b, 0.0, NEG)]).astype(np.float32)


def _dilated_log_multiplicity(seq):
    t = T_DIL
    d0 = np.arange(t)[:, None] - np.arange(t)[None, :]
    tabs = []
    for delta in range(seq // t):
        d = d0 + delta * t
        mult = np.zeros_like(d)
        for window, dil in DILATED_PAIRS:
            mult += (d >= 0) & (d % dil == 0) & (d // dil <= window // dil) & (d // dil <= seq // dil - 1)
        tabs.append(np.where(mult > 0, np.log2(np.maximum(mult, 1)), NEG).T)
    return np.stack(tabs).astype(np.float32)


def _split_w_in(w_in):
    gw = GROUP_W
    n_a = gw + 6 * HEAD_DIM
    n_b = 10 * gw
    assert w_in.shape[-1] == n_a + N_GATES + n_b, w_in.shape
    scale_a = np.ones((n_a,), np.float32)
    scale_a[:gw] = Q_SCALE
    scale_b = np.ones((n_b,), np.float32)
    scale_b[gw:2 * gw] = Q_SCALE
    w_a = (w_in[..., :n_a] * scale_a).astype(BF16)
    w_g = jnp.pad(w_in[..., n_a:n_a + N_GATES], ((0, 0), (0, 0), (0, LANES - N_GATES))).astype(BF16)
    w_b = (w_in[..., n_a + N_GATES:] * scale_b).astype(BF16)
    return w_a, w_g, w_b


def kernel(x, g_pre, w_in, pe_cmp, w_cmp1, w_cmp2, w_pool, pool_scale, sg_ln_g, sg_ln_b, w_sp, b_sp, w_out, g_post):
    nb, seq, d = x.shape
    depth = w_in.shape[0]
    assert seq % TK_SEL == 0 and seq % T_DIL == 0 and seq % TS_POOL == 0
    assert all((nb * seq) % tm == 0 for tm in (TM_PROJ, TM_OUT, TM_LAST))
    assert TQ_NSA % TW_WIN == 0 and TK_SEL % TQ_NSA == 0 and WIN % TW_WIN == 0 and TW_WIN % LANES == 0
    assert seq % (TQ_NSA * NSA_TILES) == 0 and seq % (T_DIL * DIL_TILES) == 0
    assert seq // SEL_BLOCK <= LANES - HEAD_DIM and (seq - CMP_LEN) // CMP_STRIDE + 1 < LANES
    assert seq % CMP_STRIDE == 0 and CMP_LEN == 2 * CMP_STRIDE

    w_a, w_g, w_b = _split_w_in(w_in)
    w_out_b = w_out.astype(BF16)
    w1_l = w_cmp1.reshape(depth, 2, CMP_LEN, HEAD_DIM, CMP_HIDDEN)
    w2_b = w_cmp2.astype(BF16)
    pe_rows = jnp.concatenate([pe_cmp[:, 0], pe_cmp[:, 1]], axis=-1)
    eye = jnp.eye(N_HEADS, dtype=w_pool.dtype)
    wpool_bd = jnp.einsum('lgcd,gh->lgchd', w_pool, eye).reshape(depth, GROUP_W, GROUP_W).astype(BF16)
    bsp_exp = jnp.repeat(jnp.swapaxes(b_sp, 1, 2), HEAD_DIM, axis=2)
    g_pre3 = g_pre.reshape(depth, 1, d)
    g_post3 = g_post.reshape(depth, 1, d)
    pscale3 = pool_scale.reshape(depth, 1, GROUP_W)
    lng3 = sg_ln_g.reshape(depth, 1, GROUP_W)
    lnb3 = sg_ln_b.reshape(depth, 1, GROUP_W)
    ovl_t = jnp.asarray(_overlap_t(seq), BF16)
    tri = jnp.asarray(_causal_tri())
    band = jnp.asarray(_window_band())
    dil_bias = jnp.asarray(_dilated_log_multiplicity(seq))

    x2 = x.reshape(nb * seq, d)
    pb, pf = _in_proj(x2, g_pre3, w_a, w_g, w_b, 0)
    for layer in range(depth):
        kcvc = _compress(pf, pe_rows, w1_l, w2_b, layer, nb, seq)
        ya = _nsa(pb, pf, kcvc, ovl_t, tri, band, nb, seq)
        yb = _dilated(pb, pf, dil_bias, nb, seq)
        yc, yd = _pool_sgu(pf, wpool_bd, pscale3, lng3, lnb3, w_sp, bsp_exp, layer, nb, seq)
        if layer + 1 < depth:
            x2, pb, pf = _out_in_proj(ya, yb, yc, yd, w_out_b, x2, g_post3, g_pre3, w_a, w_g, w_b, layer)
        else:
            x2 = _out_proj(ya, yb, yc, yd, w_out_b, x2, g_post3, layer)
    return x2.reshape(nb, seq, d)
```
